```python
import jax
import jax.numpy as jnp
from jax import lax
import numpy as np

D_MODEL = 1024
BATCH = 4
SEQ = 4096
DEPTH = 4

GRID_W = 64
CTX_LEN = 256
N_MIXERS = 3
ROPE_BASE = 10000.0
EPS = 1e-6

RET_DK = 256
RET_DV = 512
RET_HEADS = D_MODEL // RET_DK
RET_CHUNK = 128

MLA_DN = 128
MLA_DR = 64
MLA_DV = 128
MLA_HEADS = D_MODEL // 128
MLA_Q_LORA = D_MODEL // 4
MLA_KV_LORA = D_MODEL // 4
ATTN_BLOCK = 128

SGU_DIM = 3 * D_MODEL
SGU_GROUPS = 8
SGU_CHUNK = 128

D_FF = ((8 * D_MODEL // 3 + 255) // 256) * 256
N_EXPERTS = 8
MOE_TOP_K = 2
EXPERT_D_FF = 7 * D_MODEL // 2

N_RET = (DEPTH + 2) // 3
N_MLA = (DEPTH + 1) // 3
N_SGU = DEPTH // 3
N_DENSE = (DEPTH + 1) // 2
N_MOE = DEPTH // 2

kernel_name = 'hybrid_flow_backbone'


def rmsnorm(x, g):
    xf = x.astype(jnp.float32)
    y = xf * lax.rsqrt(jnp.mean(jnp.square(xf), axis=-1, keepdims=True) + EPS)
    return (y * g).astype(x.dtype)


def modulate(x, g, shift, scale):
    return rmsnorm(x, g) * (1.0 + scale) + shift


def adaln(cond, w, b):
    m = jax.nn.silu(cond) @ w + b
    return jnp.split(m[:, None, :], 6, axis=-1)


def axial_rope_tables(rows, n_ctx, rot_dim):
    row = jnp.repeat(jnp.arange(rows, dtype=jnp.float32), GRID_W)
    col = jnp.tile(jnp.arange(GRID_W, dtype=jnp.float32), rows)
    axis_dim = rot_dim // 2
    inv_freq = ROPE_BASE ** (-jnp.arange(0, axis_dim, 2, dtype=jnp.float32) / axis_dim)
    ang_r = row[:, None] * inv_freq
    ang_c = col[:, None] * inv_freq
    ang = jnp.concatenate([ang_r, ang_r, ang_c, ang_c], axis=-1)
    cos = jnp.concatenate([jnp.ones((n_ctx, rot_dim), jnp.float32), jnp.cos(ang)], axis=0)
    sin = jnp.concatenate([jnp.zeros((n_ctx, rot_dim), jnp.float32), jnp.sin(ang)], axis=0)
    return cos[:, None, :], sin[:, None, :]


def apply_rope(x, cos, sin):
    a1, a2, b1, b2 = jnp.split(x, 4, axis=-1)
    rot = jnp.concatenate([-a2, a1, -b2, b1], axis=-1)
    return (x * cos + rot * sin).astype(x.dtype)


def chunk_retention(q, k, v, log_gamma, s0):
    B, H, L, _ = q.shape
    n = L // RET_CHUNK
    pos = jnp.arange(RET_CHUNK, dtype=jnp.float32)
    lg = log_gamma[:, None]
    q_decay = jnp.exp(lg * (pos + 1.0))[..., None]
    k_decay = jnp.exp(lg * (RET_CHUNK - 1.0 - pos))[..., None]
    diff = pos[:, None] - pos[None, :]
    intra = jnp.where(diff >= 0, jnp.exp(lg[:, :, None] * jnp.maximum(diff, 0.0)), 0.0)
    chunk_decay = jnp.exp(log_gamma * RET_CHUNK)[:, None, None]

    def blocks(a):
        return jnp.moveaxis(a.reshape(B, H, n, RET_CHUNK, a.shape[-1]), 2, 0)

    def step(s, qkv):
        qb, kb, vb = qkv
        scores = jnp.einsum('bhqd,bhkd->bhqk', qb, kb) * intra
        o = (jnp.einsum('bhqk,bhkv->bhqv', scores, vb)
             + jnp.einsum('bhqd,bhdv->bhqv', qb * q_decay, s))
        s = s * chunk_decay + jnp.einsum('bhkd,bhkv->bhdv', kb * k_decay, vb)
        return s, o

    s, o = lax.scan(step, s0, (blocks(q), blocks(k), blocks(v)))
    o = jnp.moveaxis(o, 0, 2).reshape(B, H, L, v.shape[-1])
    return o, s


def head_groupnorm(o, g):
    o = jnp.moveaxis(o, 1, 2)
    mu = jnp.mean(o, axis=-1, keepdims=True)
    var = jnp.mean(jnp.square(o - mu), axis=-1, keepdims=True)
    y = (o - mu) * lax.rsqrt(var + EPS)
    return y.reshape(y.shape[0], y.shape[1], -1) * g


def retention_mixer(hc, hl, w_in, w_o, decay_f, decay_b, gn_f, gn_b, cos, sin, with_ctx):
    B, C, _ = hc.shape
    h = jnp.concatenate([hc, hl], axis=1)
    T = h.shape[1]
    qk_w, v_w = RET_HEADS * RET_DK, RET_HEADS * RET_DV
    q, k, v, gf, gb = jnp.split(h @ w_in, [qk_w, 2 * qk_w, 2 * qk_w + v_w, 2 * qk_w + 2 * v_w], axis=-1)
    q = apply_rope(q.reshape(B, T, RET_HEADS, RET_DK), cos, sin)
    k = apply_rope(k.reshape(B, T, RET_HEADS, RET_DK), cos, sin) * (RET_DK ** -0.5)
    v = v.reshape(B, T, RET_HEADS, RET_DV)
    q, k, v = [jnp.moveaxis(a, 1, 2).astype(jnp.float32) for a in (q, k, v)]
    zero = jnp.zeros((B, RET_HEADS, RET_DK, RET_DV), jnp.float32)

    def direction(decay_logit, reverse):
        lg = jax.nn.log_sigmoid(decay_logit.astype(jnp.float32))
        f = (lambda a: jnp.flip(a, axis=2)) if reverse else (lambda a: a)
        oc, s_ctx = chunk_retention(f(q[:, :, :C]), f(k[:, :, :C]), f(v[:, :, :C]), lg, zero)
        ol, _ = chunk_retention(f(q[:, :, C:]), f(k[:, :, C:]), f(v[:, :, C:]), lg, s_ctx)
        return jnp.concatenate([f(oc), f(ol)], axis=2) if with_ctx else f(ol)

    o_f = direction(decay_f, False)
    o_b = direction(decay_b, True)
    if not with_ctx:
        gf, gb = gf[:, C:], gb[:, C:]
    mixed = jax.nn.silu(gf) * head_groupnorm(o_f, gn_f) + jax.nn.silu(gb) * head_groupnorm(o_b, gn_b)
    y = mixed.astype(h.dtype) @ w_o
    return (y[:, :C], y[:, C:]) if with_ctx else (None, y)


def softmax_attend(q, k, v, scale):
    s = jnp.einsum('bqhd,bkhd->bhqk', q, k).astype(jnp.float32) * scale
    p = jax.nn.softmax(s, axis=-1).astype(v.dtype)
    return jnp.einsum('bhqk,bkhd->bqhd', p, v)


def mla_mixer(hc, hl, w_down, g_q, g_kv, w_uq, w_ukv, w_o, cos, sin, with_ctx):
    B, C, _ = hc.shape
    h = jnp.concatenate([hc, hl], axis=1)
    T = h.shape[1]
    L = T - C
    cq, ckv, kr = jnp.split(h @ w_down, [MLA_Q_LORA, MLA_Q_LORA + MLA_KV_LORA], axis=-1)
    q = (rmsnorm(cq, g_q) @ w_uq).reshape(B, T, MLA_HEADS, MLA_DN + MLA_DR)
    q = jnp.concatenate([q[..., :MLA_DN], apply_rope(q[..., MLA_DN:], cos, sin)], axis=-1)
    kv = (rmsnorm(ckv, g_kv) @ w_ukv).reshape(B, T, MLA_HEADS, MLA_DN + MLA_DV)
    k_rope = apply_rope(kr[:, :, None, :], cos, sin)
    k = jnp.concatenate([kv[..., :MLA_DN], jnp.broadcast_to(k_rope, (B, T, MLA_HEADS, MLA_DR))], axis=-1)
    v = kv[..., MLA_DN:]
    scale = (MLA_DN + MLA_DR) ** -0.5
    nb = L // ATTN_BLOCK
    qblocks = jnp.moveaxis(q[:, C:].reshape(B, nb, ATTN_BLOCK, MLA_HEADS, MLA_DN + MLA_DR), 1, 0)
    ol = lax.map(lambda qb: softmax_attend(qb, k, v, scale), qblocks)
    ol = jnp.moveaxis(ol, 0, 1).reshape(B, L, MLA_HEADS * MLA_DV)
    if with_ctx:
        oc = softmax_attend(q[:, :C], k[:, :C], v[:, :C], scale).reshape(B, C, MLA_HEADS * MLA_DV)
        y = jnp.concatenate([oc, ol], axis=1) @ w_o
        return y[:, :C], y[:, C:]
    return None, ol @ w_o


def sgu_mixer(hc, hl, w_in, g_v, w_s, b_s, w_out, with_ctx):
    C = hc.shape[1]
    h = jnp.concatenate([hc, hl], axis=1) if with_ctx else hl
    B, T, _ = h.shape
    u, v = jnp.split(jax.nn.gelu(h @ w_in), 2, axis=-1)
    v = rmsnorm(v, g_v).reshape(B, T // SGU_CHUNK, SGU_CHUNK, SGU_GROUPS, SGU_DIM // SGU_GROUPS)
    z = jnp.einsum('gpq,bnqgc->bnpgc', w_s, v) + jnp.swapaxes(b_s, 0, 1)[:, :, None]
    y = (u * z.reshape(B, T, SGU_DIM)) @ w_out
    return (y[:, :C], y[:, C:]) if with_ctx else (None, y)


def swiglu(h, w_gu, w_down):
    g, u = jnp.split(h @ w_gu, 2, axis=-1)
    return (jax.nn.silu(g) * u) @ w_down


def moe_swiglu(h, w_router, w_gu, w_down):
    B, T, D = h.shape
    hf = h.reshape(B * T, D)
    logits = (hf @ w_router).astype(jnp.float32)
    top_v, top_i = lax.top_k(logits, MOE_TOP_K)
    top_w = jax.nn.softmax(top_v, axis=-1)
    gates = jnp.sum(jax.nn.one_hot(top_i, N_EXPERTS, dtype=jnp.float32) * top_w[..., None], axis=1)
    gates = gates.astype(hf.dtype)
    y = jnp.zeros_like(hf)
    for e in range(N_EXPERTS):
        y = y + gates[:, e:e + 1] * swiglu(hf, w_gu[e], w_down[e])
    return y.reshape(B, T, D)


def setup_inputs(seed: int = 0) -> dict:
    key = jax.random.key(seed)
    keys = iter(jax.random.split(key, 40))
    f32 = jnp.float32

    def nrm(shape, fan_in, scale=1.0):
        return jax.random.normal(next(keys), shape, f32) * (scale * fan_in ** -0.5)

    def gain(shape):
        return 1.0 + 0.02 * jax.random.normal(next(keys), shape, f32)

    gamma0 = 1.0 - 2.0 ** (-5.0 - jnp.arange(RET_HEADS, dtype=f32))
    logit0 = jnp.log(gamma0) - jnp.log1p(-gamma0)
    return {
        'x': jax.random.normal(next(keys), (BATCH, SEQ, D_MODEL), f32),
        'c': jax.random.normal(next(keys), (BATCH, D_MODEL), f32),
        'ctx': jax.random.normal(next(keys), (BATCH, CTX_LEN, D_MODEL), f32),
        'c_ctx': jax.random.normal(next(keys), (D_MODEL,), f32),
        'ada_w': nrm((DEPTH, D_MODEL, 6 * D_MODEL), D_MODEL, 0.5),
        'ada_b': 0.02 * jax.random.normal(next(keys), (DEPTH, 6 * D_MODEL), f32),
        'norm_mix_g': gain((DEPTH, D_MODEL)),
        'norm_ffn_g': gain((DEPTH, D_MODEL)),
        'final_norm_g': gain((D_MODEL,)),
        'ret_w_in': nrm((N_RET, D_MODEL, 2 * RET_HEADS * RET_DK + 3 * RET_HEADS * RET_DV), D_MODEL),
        'ret_w_o': nrm((N_RET, RET_HEADS * RET_DV, D_MODEL), RET_HEADS * RET_DV),
        'ret_decay_f': logit0 + 0.1 * jax.random.normal(next(keys), (N_RET, RET_HEADS), f32),
        'ret_decay_b': logit0 + 0.1 * jax.random.normal(next(keys), (N_RET, RET_HEADS), f32),
        'ret_gn_f': gain((N_RET, RET_HEADS * RET_DV)),
        'ret_gn_b': gain((N_RET, RET_HEADS * RET_DV)),
        'mla_w_down': nrm((N_MLA, D_MODEL, MLA_Q_LORA + MLA_KV_LORA + MLA_DR), D_MODEL),
        'mla_g_q': gain((N_MLA, MLA_Q_LORA)),
        'mla_g_kv': gain((N_MLA, MLA_KV_LORA)),
        'mla_w_uq': nrm((N_MLA, MLA_Q_LORA, MLA_HEADS * (MLA_DN + MLA_DR)), MLA_Q_LORA),
        'mla_w_ukv': nrm((N_MLA, MLA_KV_LORA, MLA_HEADS * (MLA_DN + MLA_DV)), MLA_KV_LORA),
        'mla_w_o': nrm((N_MLA, MLA_HEADS * MLA_DV, D_MODEL), MLA_HEADS * MLA_DV),
        'sgu_w_in': nrm((N_SGU, D_MODEL, 2 * SGU_DIM), D_MODEL),
        'sgu_g_v': gain((N_SGU, SGU_DIM)),
        'sgu_w_s': nrm((N_SGU, SGU_GROUPS, SGU_CHUNK, SGU_CHUNK), SGU_CHUNK),
        'sgu_b_s': gain((N_SGU, SGU_GROUPS, SGU_CHUNK)),
        'sgu_w_out': nrm((N_SGU, SGU_DIM, D_MODEL), SGU_DIM),
        'ffn_w_gu': nrm((N_DENSE, D_MODEL, 2 * D_FF), D_MODEL),
        'ffn_w_down': nrm((N_DENSE, D_FF, D_MODEL), D_FF),
        'moe_w_router': nrm((N_MOE, D_MODEL, N_EXPERTS), D_MODEL),
        'moe_w_gu': nrm((N_MOE, N_EXPERTS, D_MODEL, 2 * EXPERT_D_FF), D_MODEL),
        'moe_w_down': nrm((N_MOE, N_EXPERTS, EXPERT_D_FF, D_MODEL), EXPERT_D_FF),
    }


def reference(x, c, ctx, c_ctx, ada_w, ada_b, norm_mix_g, norm_ffn_g, final_norm_g,
              ret_w_in, ret_w_o, ret_decay_f, ret_decay_b, ret_gn_f, ret_gn_b,
              mla_w_down, mla_g_q, mla_g_kv, mla_w_uq, mla_w_ukv, mla_w_o,
              sgu_w_in, sgu_g_v, sgu_w_s, sgu_b_s, sgu_w_out,
              ffn_w_gu, ffn_w_down, moe_w_router, moe_w_gu, moe_w_down):
    L = x.shape[1]
    C = ctx.shape[1]
    rows = L // GRID_W
    ret_cos, ret_sin = axial_rope_tables(rows, C, RET_DK)
    mla_cos, mla_sin = axial_rope_tables(rows, C, MLA_DR)
    xl, xc = x, ctx
    for i in range(DEPTH):
        with_ctx = i < DEPTH - 1
        ml = adaln(c, ada_w[i], ada_b[i])
        mc = adaln(c_ctx[None, :], ada_w[i], ada_b[i])
        hl = modulate(xl, norm_mix_g[i], ml[0], ml[1])
        hc = modulate(xc, norm_mix_g[i], mc[0], mc[1])
        kind, j = i % N_MIXERS, i // N_MIXERS
        if kind == 0:
            yc, yl = retention_mixer(hc, hl, ret_w_in[j], ret_w_o[j], ret_decay_f[j], ret_decay_b[j],
                                     ret_gn_f[j], ret_gn_b[j], ret_cos, ret_sin, with_ctx)
        elif kind == 1:
            yc, yl = mla_mixer(hc, hl, mla_w_down[j], mla_g_q[j], mla_g_kv[j], mla_w_uq[j],
                               mla_w_ukv[j], mla_w_o[j], mla_cos, mla_sin, with_ctx)
        else:
            yc, yl = sgu_mixer(hc, hl, sgu_w_in[j], sgu_g_v[j], sgu_w_s[j], sgu_b_s[j],
                               sgu_w_out[j], with_ctx)
        xl = xl + ml[2] * yl
        hl = modulate(xl, norm_ffn_g[i], ml[3], ml[4])
        if with_ctx:
            xc = xc + mc[2] * yc
            hc = modulate(xc, norm_ffn_g[i], mc[3], mc[4])
            h = jnp.concatenate([hc, hl], axis=1)
        else:
            h = hl
        f = i // 2
        if i % 2 == 0:
            y = swiglu(h, ffn_w_gu[f], ffn_w_down[f])
        else:
            y = moe_swiglu(h, moe_w_router[f], moe_w_gu[f], moe_w_down[f])
        if with_ctx:
            xc = xc + mc[5] * y[:, :C]
            xl = xl + ml[5] * y[:, C:]
        else:
            xl = xl + ml[5] * y
    return rmsnorm(xl, final_norm_g)
```

```python
import functools

import jax
import jax.numpy as jnp
from jax import lax
from jax.experimental import pallas as pl
from jax.experimental.pallas import tpu as pltpu

F32 = jnp.float32
BF16 = jnp.bfloat16
HIGHEST = lax.Precision.HIGHEST

GRID_W = 64
ROPE_BASE = 10000.0
EPS = 1e-6
RET_DK = 256
RET_DV = 512
MLA_DN = 128
MLA_DR = 64
MLA_DV = 128
SGU_GROUPS = 8
SGU_CHUNK = 128
N_EXPERTS = 8
N_MIXERS = 3

TM = 1024
RET_CHUNK = 256
ATTN_TQ = 512
ATTN_TK = 512
MOE_TM = 512
MOE_BF = 896
VMEM_LIMIT = 56 * 1024 * 1024


def _cparams(sem):
    return pltpu.CompilerParams(dimension_semantics=sem, vmem_limit_bytes=VMEM_LIMIT)


def _silu(x):
    return x * jax.nn.sigmoid(x)


def _rope(x, cos, sin_signed, half):
    w = x.shape[-1]
    lane = lax.broadcasted_iota(jnp.int32, x.shape, 1)
    first = (lane % (2 * half)) < half
    rot = jnp.where(first, pltpu.roll(x, w - half, 1), pltpu.roll(x, half, 1))
    return x * cos + rot * sin_signed


def _rope_tables(rows, n_ctx_rows, rot_dim):
    row = jnp.repeat(jnp.arange(rows, dtype=F32), GRID_W)
    col = jnp.tile(jnp.arange(GRID_W, dtype=F32), rows)
    axis_dim = rot_dim // 2
    inv_freq = ROPE_BASE ** (-jnp.arange(0, axis_dim, 2, dtype=F32) / axis_dim)
    ang_r = row[:, None] * inv_freq
    ang_c = col[:, None] * inv_freq
    ang = jnp.concatenate([ang_r, ang_r, ang_c, ang_c], axis=-1)
    quarter = rot_dim // 4
    sign = jnp.where((jnp.arange(rot_dim) % (2 * quarter)) < quarter, -1.0, 1.0).astype(F32)
    cos = jnp.concatenate([jnp.cos(ang), jnp.ones((n_ctx_rows, rot_dim), F32)], axis=0)
    sin = jnp.concatenate([jnp.sin(ang) * sign, jnp.zeros((n_ctx_rows, rot_dim), F32)], axis=0)
    return cos, sin


def _adaln_kernel(cond_ref, w_ref, b_ref, o_ref):
    s = _silu(cond_ref[...])
    o_ref[...] = jnp.dot(s, w_ref[...], precision=HIGHEST, preferred_element_type=F32) + b_ref[...]


def _adaln(cond, ada_w, ada_b):
    depth, d, n6 = ada_w.shape
    bn = n6 // 4
    return pl.pallas_call(
        _adaln_kernel,
        grid=(depth, n6 // bn),
        in_specs=[
            pl.BlockSpec((8, d), lambda l, j: (0, 0)),
            pl.BlockSpec((None, d, bn), lambda l, j: (l, 0, j)),
            pl.BlockSpec((None, 1, bn), lambda l, j: (l, 0, j)),
        ],
        out_specs=pl.BlockSpec((None, 8, bn), lambda l, j: (l, 0, j)),
        out_shape=jax.ShapeDtypeStruct((depth, 8, n6), F32),
        compiler_params=_cparams(("arbitrary", "arbitrary")),
        name="adaln",
    )(cond, ada_w, ada_b.reshape(depth, 1, n6))


def _modnorm(x_ref, g_ref, shift_ref, scale_ref):
    xf = x_ref[...]
    y = xf * lax.rsqrt(jnp.mean(xf * xf, axis=-1, keepdims=True) + EPS)
    return (y * g_ref[...]) * (1.0 + scale_ref[...]) + shift_ref[...]


def _proj_kernel(*refs, mode):
    if mode == "ret":
        x_ref, g_ref, sh_ref, sc_ref, w_ref, cos_ref, sin_ref, o_ref, h_scr = refs
    elif mode == "swiglu":
        x_ref, g_ref, sh_ref, sc_ref, w_ref, w2_ref, o_ref, h_scr = refs
    else:
        x_ref, g_ref, sh_ref, sc_ref, w_ref, o_ref, h_scr = refs
    j = pl.program_id(1)

    @pl.when(j == 0)
    def _():
        h_scr[...] = _modnorm(x_ref, g_ref, sh_ref, sc_ref).astype(BF16)

    y = jnp.dot(h_scr[...], w_ref[...], preferred_element_type=F32)
    if mode == "f32":
        o_ref[...] = y
    elif mode == "gelu":
        o_ref[...] = jax.nn.gelu(y).astype(o_ref.dtype)
    elif mode == "swiglu":
        u = jnp.dot(h_scr[...], w2_ref[...], preferred_element_type=F32)
        o_ref[...] = (_silu(y) * u).astype(o_ref.dtype)
    elif mode == "ret":
        bn = y.shape[1]
        n_q = 1024 // bn

        @pl.when(j < 2 * n_q)
        def _():
            kscale = jnp.where(j >= n_q, RET_DK ** -0.5, 1.0).astype(F32)
            for s in range(bn // RET_DK):
                sl = slice(s * RET_DK, (s + 1) * RET_DK)
                r = _rope(y[:, sl], cos_ref[...], sin_ref[...], RET_DK // 4)
                o_ref[:, sl] = (r * kscale).astype(o_ref.dtype)

        @pl.when((j >= 2 * n_q) & (j < 4 * n_q))
        def _():
            o_ref[...] = y.astype(o_ref.dtype)

        @pl.when(j >= 4 * n_q)
        def _():
            o_ref[...] = _silu(y).astype(o_ref.dtype)


def _mod_specs(layer, k_shift, tiles_per_batch, n_batch):
    def spec(k):
        return pl.BlockSpec((None, None, 1, 1024),
                            lambda i, j: (layer * 6 + k, jnp.minimum(i // tiles_per_batch, n_batch), 0, 0))
    return spec(k_shift), spec(k_shift + 1)


def _proj(x, g, mods, layer, k_shift, w, *, mode, bn, out_dtype, n_rows, geom, w2=None, rope=None):
    d = x.shape[1]
    n = w.shape[1]
    nt = n_rows // TM
    tpb, n_batch = geom
    sh_spec, sc_spec = _mod_specs(layer, k_shift, tpb, n_batch)
    in_specs = [
        pl.BlockSpec((TM, d), lambda i, j: (i, 0)),
        pl.BlockSpec((1, d), lambda i, j: (0, 0)),
        sh_spec, sc_spec,
        pl.BlockSpec((d, bn), lambda i, j: (0, j)),
    ]
    args = [x, g.reshape(1, d), mods, mods, w]
    if mode == "swiglu":
        in_specs.append(pl.BlockSpec((d, bn), lambda i, j: (0, j)))
        args.append(w2)
    if mode == "ret":
        cos, sin = rope
        n_lat_tiles = tpb * n_batch

        def pos_idx(i, j):
            return (jnp.where(i < n_lat_tiles, i % tpb, tpb + (i - n_lat_tiles)), 0)
        in_specs += [pl.BlockSpec((TM, RET_DK), pos_idx), pl.BlockSpec((TM, RET_DK), pos_idx)]
        args += [cos, sin]
    return pl.pallas_call(
        functools.partial(_proj_kernel, mode=mode),
        grid=(nt, n // bn),
        in_specs=in_specs,
        out_specs=pl.BlockSpec((TM, bn), lambda i, j: (i, j)),
        out_shape=jax.ShapeDtypeStruct((n_rows, n), out_dtype),
        scratch_shapes=[pltpu.VMEM((TM, d), BF16)],
        compiler_params=_cparams(("arbitrary", "arbitrary")),
        name="proj_" + mode,
    )(*args)


def _outproj_kernel(*refs, n_a):
    a_refs = refs[:n_a]
    w_ref, x_ref, gate_ref, o_ref = refs[n_a:]
    if n_a == 1:
        a = a_refs[0][...]
    else:
        a = (a_refs[0][...].astype(F32) + a_refs[1][...].astype(F32)).astype(BF16)
    y = jnp.dot(a, w_ref[...], preferred_element_type=F32)
    o_ref[...] = x_ref[...] + gate_ref[...] * y


def _outproj(a_list, a_specs, w, x, mods, layer, k_gate, *, n_rows, geom):
    k, d = w.shape
    nt = n_rows // TM
    tpb, n_batch = geom
    gate_spec = pl.BlockSpec((None, None, 1, d),
                             lambda i: (layer * 6 + k_gate, jnp.minimum(i // tpb, n_batch), 0, 0))
    return pl.pallas_call(
        functools.partial(_outproj_kernel, n_a=len(a_list)),
        grid=(nt,),
        in_specs=list(a_specs) + [
            pl.BlockSpec((k, d), lambda i: (0, 0)),
            pl.BlockSpec((TM, d), lambda i: (i, 0)),
            gate_spec,
        ],
        out_specs=pl.BlockSpec((TM, d), lambda i: (i, 0)),
        out_shape=jax.ShapeDtypeStruct(x.shape, F32),
        input_output_aliases={len(a_list) + 1: 0},
        compiler_params=_cparams(("arbitrary",)),
        name="outproj",
    )(*a_list, w, x, mods)


def _ret_kernel(lg_ref, q_ref, k_ref, v_ref, gate_ref, gn_ref, o_ref, s_scr):
    d = pl.program_id(0)
    h = pl.program_id(2)
    s = pl.program_id(3)
    c = RET_CHUNK
    lg = lg_ref[d, h]

    @pl.when(s == 0)
    def _():
        s_scr[...] = jnp.zeros_like(s_scr)

    fwd = d == 0
    ii = lax.broadcasted_iota(jnp.int32, (c, c), 0)
    jj = lax.broadcasted_iota(jnp.int32, (c, c), 1)
    diff = jnp.where(fwd, ii - jj, jj - ii).astype(F32)
    intra = jnp.where(diff >= 0, jnp.exp(lg * jnp.maximum(diff, 0.0)), 0.0)
    pos = lax.broadcasted_iota(jnp.int32, (c, 1), 0).astype(F32)
    fpos = jnp.where(fwd, pos, (c - 1.0) - pos)
    q_decay = jnp.exp(lg * (fpos + 1.0))
    k_decay = jnp.exp(lg * ((c - 1.0) - fpos))
    chunk_decay = jnp.exp(lg * float(c))

    q = q_ref[...]
    k = k_ref[...]
    v = v_ref[...]
    state = s_scr[...]
    scores = lax.dot_general(q, k, (((1,), (1,)), ((), ())), preferred_element_type=F32) * intra
    o = jnp.dot(scores.astype(BF16), v, preferred_element_type=F32)
    o = o + q_decay * jnp.dot(q, state.astype(BF16), preferred_element_type=F32)
    kd = (k.astype(F32) * k_decay).astype(BF16)
    s_scr[...] = state * chunk_decay + lax.dot_general(
        kd, v, (((0,), (0,)), ((), ())), preferred_element_type=F32)

    mu = jnp.mean(o, axis=-1, keepdims=True)
    oc = o - mu
    var = jnp.mean(oc * oc, axis=-1, keepdims=True)
    y = oc * lax.rsqrt(var + EPS) * gn_ref[...]
    o_ref[...] = (gate_ref[...].astype(F32) * y).astype(o_ref.dtype)


def _retention(p, lg, gn, n_batch, n_lat, n_ctx):
    m = p.shape[0]
    heads = 1024 // RET_DK
    c = RET_CHUNK
    lat_chunks = n_lat // c
    ctx_chunks = n_ctx // c
    assert ctx_chunks == 1
    steps = ctx_chunks + lat_chunks
    lat_blk0 = 0
    ctx_blk0 = (n_batch * n_lat) // c

    def row_blk(d, b, s):
        t = jnp.where(d == 0, s - 1, lat_chunks - s)
        return jnp.where(s == 0, ctx_blk0 + b, lat_blk0 + b * lat_chunks + t)

    q_spec = pl.BlockSpec((c, RET_DK), lambda d, b, h, s: (row_blk(d, b, s), h))
    k_spec = pl.BlockSpec((c, RET_DK), lambda d, b, h, s: (row_blk(d, b, s), heads + h))
    v_spec = pl.BlockSpec((c, RET_DV), lambda d, b, h, s: (row_blk(d, b, s), heads + h))
    g_spec = pl.BlockSpec((c, RET_DV), lambda d, b, h, s: (row_blk(d, b, s), 2 * heads + heads * d + h))
    gn_spec = pl.BlockSpec((None, 1, RET_DV), lambda d, b, h, s: (d, 0, h))
    return pl.pallas_call(
        _ret_kernel,
        grid=(2, n_batch, heads, steps),
        in_specs=[pl.BlockSpec(memory_space=pltpu.SMEM), q_spec, k_spec, v_spec, g_spec, gn_spec],
        out_specs=pl.BlockSpec((None, c, RET_DV), lambda d, b, h, s: (d, row_blk(d, b, s), h)),
        out_shape=jax.ShapeDtypeStruct((2, m, heads * RET_DV), BF16),
        scratch_shapes=[pltpu.VMEM((RET_DK, RET_DV), F32)],
        compiler_params=_cparams(("arbitrary",) * 4),
        name="retention",
    )(lg, p, p, p, p, gn)


def _mla_up_kernel(dn_ref, gq_ref, gkv_ref, wq_ref, wk_ref, wv_ref, cos_ref, sin_ref,
                   q_ref, k_ref, v_ref, *, heads, scale):
    dn = dn_ref[...]
    lq = gq_ref.shape[1]
    lkv = gkv_ref.shape[1]

    def rms(a, g):
        return (a * lax.rsqrt(jnp.mean(a * a, axis=-1, keepdims=True) + EPS) * g).astype(BF16)

    cq = rms(dn[:, :lq], gq_ref[...])
    ckv = rms(dn[:, lq:lq + lkv], gkv_ref[...])
    cos = cos_ref[...]
    sin = sin_ref[...]
    half = MLA_DR // 4
    q = jnp.dot(cq, wq_ref[...], preferred_element_type=F32)
    kn = jnp.dot(ckv, wk_ref[...], preferred_element_type=F32)
    v_ref[...] = jnp.dot(ckv, wv_ref[...], preferred_element_type=F32).astype(v_ref.dtype)
    kr = _rope(dn[:, lq + lkv:], cos, sin, half).astype(k_ref.dtype)
    for h in range(heads):
        o = h * 2 * MLA_DN
        q_ref[:, o:o + MLA_DN] = (q[:, o:o + MLA_DN] * scale).astype(q_ref.dtype)
        qr = _rope(q[:, o + MLA_DN:o + 2 * MLA_DN], cos, sin, half)
        q_ref[:, o + MLA_DN:o + 2 * MLA_DN] = (qr * scale).astype(q_ref.dtype)
        k_ref[:, o:o + MLA_DN] = kn[:, h * MLA_DN:(h + 1) * MLA_DN].astype(k_ref.dtype)
        k_ref[:, o + MLA_DN:o + 2 * MLA_DN] = kr


def _attn_lat_kernel(q_ref, kl_ref, vl_ref, kc_ref, vc_ref, o_ref):
    q = q_ref[...]
    tq = q.shape[0]
    n_chunks = kl_ref.shape[0] // ATTN_TK

    def block(k, v, m, l, acc):
        s = lax.dot_general(q, k, (((1,), (1,)), ((), ())), preferred_element_type=F32)
        m_new = jnp.maximum(m, jnp.max(s, axis=-1, keepdims=True))
        alpha = jnp.exp(m - m_new)
        p = jnp.exp(s - m_new)
        l = alpha * l + jnp.sum(p, axis=-1, keepdims=True)
        acc = alpha * acc + jnp.dot(p.astype(BF16), v, preferred_element_type=F32)
        return m_new, l, acc

    m0 = jnp.full((tq, 1), -jnp.inf, F32)
    l0 = jnp.zeros((tq, 1), F32)
    acc0 = jnp.zeros((tq, vl_ref.shape[1]), F32)
    carry = block(kc_ref[...], vc_ref[...], m0, l0, acc0)

    def body(c, carry):
        r = pl.ds(pl.multiple_of(c * ATTN_TK, ATTN_TK), ATTN_TK)
        return block(kl_ref[r, :], vl_ref[r, :], *carry)

    m, l, acc = lax.fori_loop(0, n_chunks, body, carry)
    o_ref[...] = (acc / l).astype(o_ref.dtype)


def _attn_ctx_kernel(prev_ref, q_ref, kc_ref, vc_ref, o_ref):
    del prev_ref
    s = lax.dot_general(q_ref[...], kc_ref[...], (((1,), (1,)), ((), ())), preferred_element_type=F32)
    p = jnp.exp(s - jnp.max(s, axis=-1, keepdims=True))
    l = jnp.sum(p, axis=-1, keepdims=True)
    o_ref[...] = (jnp.dot(p.astype(BF16), vc_ref[...], preferred_element_type=F32) / l).astype(o_ref.dtype)


def _mla_attention(q, k, v, n_batch, n_lat, n_ctx, heads):
    m = q.shape[0]
    hd = 2 * MLA_DN
    nq = n_lat // ATTN_TQ
    ctx_blk0 = (n_batch * n_lat) // n_ctx
    kl = pl.BlockSpec((n_lat, hd), lambda b, h, t: (b, h))
    vl = pl.BlockSpec((n_lat, MLA_DV), lambda b, h, t: (b, h))
    kc = pl.BlockSpec((n_ctx, hd), lambda b, h, t: (ctx_blk0 + b, h))
    vc = pl.BlockSpec((n_ctx, MLA_DV), lambda b, h, t: (ctx_blk0 + b, h))
    out = pl.pallas_call(
        _attn_lat_kernel,
        grid=(n_batch, heads, nq),
        in_specs=[pl.BlockSpec((ATTN_TQ, hd), lambda b, h, t: (b * nq + t, h)), kl, vl, kc, vc],
        out_specs=pl.BlockSpec((ATTN_TQ, MLA_DV), lambda b, h, t: (b * nq + t, h)),
        out_shape=jax.ShapeDtypeStruct((m, heads * MLA_DV), BF16),
        compiler_params=_cparams(("arbitrary",) * 3),
        name="mla_attn_latent",
    )(q, k, v, k, v)
    return pl.pallas_call(
        _attn_ctx_kernel,
        grid=(n_batch, heads),
        in_specs=[
            pl.BlockSpec(memory_space=pl.ANY),
            pl.BlockSpec((n_ctx, hd), lambda b, h: (ctx_blk0 + b, h)),
            pl.BlockSpec((n_ctx, hd), lambda b, h: (ctx_blk0 + b, h)),
            pl.BlockSpec((n_ctx, MLA_DV), lambda b, h: (ctx_blk0 + b, h)),
        ],
        out_specs=pl.BlockSpec((n_ctx, MLA_DV), lambda b, h: (ctx_blk0 + b, h)),
        out_shape=jax.ShapeDtypeStruct((m, heads * MLA_DV), BF16),
        input_output_aliases={0: 0},
        compiler_params=_cparams(("arbitrary",) * 2),
        name="mla_attn_ctx",
    )(out, q, k, v)


def _sgu_kernel(u_ref, v_ref, gv_ref, ws_ref, bs_ref, wo_ref, x_ref, gate_ref, o_ref, z_scr):
    sd = v_ref.shape[1]
    gw = sd // SGU_GROUPS
    for ch in range(v_ref.shape[0] // SGU_CHUNK):
        rows = slice(ch * SGU_CHUNK, (ch + 1) * SGU_CHUNK)
        v = v_ref[rows, :].astype(F32)
        vn = (v * lax.rsqrt(jnp.mean(v * v, axis=-1, keepdims=True) + EPS) * gv_ref[...]).astype(BF16)
        for g in range(SGU_GROUPS):
            cols = slice(g * gw, (g + 1) * gw)
            z = jnp.dot(ws_ref[g], vn[:, cols], preferred_element_type=F32) + bs_ref[g]
            z_scr[rows, cols] = (u_ref[rows, cols].astype(F32) * z).astype(BF16)
    y = jnp.dot(z_scr[...], wo_ref[...], preferred_element_type=F32)
    o_ref[...] = x_ref[...] + gate_ref[...] * y


def _sgu(uv, g_v, w_s, b_s, w_out, x, mods, layer, geom):
    m, d = x.shape
    sd = w_out.shape[0]
    tpb, n_batch = geom
    tms = TM // 2
    scale = TM // tms
    gate_spec = pl.BlockSpec((None, None, 1, d),
                             lambda i: (layer * 6 + 2, jnp.minimum(i // (tpb * scale), n_batch), 0, 0))
    return pl.pallas_call(
        _sgu_kernel,
        grid=(m // tms,),
        in_specs=[
            pl.BlockSpec((tms, sd), lambda i: (i, 0)),
            pl.BlockSpec((tms, sd), lambda i: (i, 1)),
            pl.BlockSpec((1, sd), lambda i: (0, 0)),
            pl.BlockSpec((SGU_GROUPS, SGU_CHUNK, SGU_CHUNK), lambda i: (0, 0, 0)),
            pl.BlockSpec((SGU_GROUPS, SGU_CHUNK, 1), lambda i: (0, 0, 0)),
            pl.BlockSpec((sd, d), lambda i: (0, 0)),
            pl.BlockSpec((tms, d), lambda i: (i, 0)),
            gate_spec,
        ],
        out_specs=pl.BlockSpec((tms, d), lambda i: (i, 0)),
        out_shape=jax.ShapeDtypeStruct(x.shape, F32),
        scratch_shapes=[pltpu.VMEM((tms, sd), BF16)],
        input_output_aliases={6: 0},
        compiler_params=_cparams(("arbitrary",)),
        name="sgu",
    )(uv, uv, g_v.reshape(1, sd), w_s, b_s.reshape(SGU_GROUPS, SGU_CHUNK, 1), w_out, x, mods)


def _router_kernel(x_ref, g_ref, sh_ref, sc_ref, wr_ref, h_ref, idx_ref, wgt_ref):
    h = _modnorm(x_ref, g_ref, sh_ref, sc_ref)
    h_ref[...] = h.astype(h_ref.dtype)
    logits = jnp.dot(h, wr_ref[...], precision=HIGHEST, preferred_element_type=F32)
    lane = lax.broadcasted_iota(jnp.int32, logits.shape, 1).astype(F32)
    neg = jnp.float32(-jnp.inf)
    big = jnp.float32(logits.shape[1])
    logits = jnp.where(lane < N_EXPERTS, logits, neg)
    m1 = jnp.max(logits, axis=-1, keepdims=True)
    i1 = jnp.min(jnp.where(logits == m1, lane, big), axis=-1, keepdims=True)
    rest = jnp.where(lane == i1, neg, logits)
    m2 = jnp.max(rest, axis=-1, keepdims=True)
    i2 = jnp.min(jnp.where(rest == m2, lane, big), axis=-1, keepdims=True)
    e2 = jnp.exp(m2 - m1)
    w1 = 1.0 / (1.0 + e2)
    w2 = e2 / (1.0 + e2)
    idx_ref[...] = jnp.where(lane == 0, i1, jnp.where(lane == 1, i2, 0.0)).astype(jnp.int32)
    wgt_ref[...] = jnp.where(lane == 0, w1, jnp.where(lane == 1, w2, 0.0))


def _router(x, g, mods, layer, w_router_pad, *, n_rows, geom):
    d = x.shape[1]
    tpb, n_batch = geom
    sh_spec, sc_spec = _mod_specs(layer, 3, tpb, n_batch)
    one = lambda f: (lambda i: f(i, 0))
    lanes = w_router_pad.shape[1]
    return pl.pallas_call(
        _router_kernel,
        grid=(n_rows // TM,),
        in_specs=[
            pl.BlockSpec((TM, d), lambda i: (i, 0)),
            pl.BlockSpec((1, d), lambda i: (0, 0)),
            pl.BlockSpec(sh_spec.block_shape, one(sh_spec.index_map)),
            pl.BlockSpec(sc_spec.block_shape, one(sc_spec.index_map)),
            pl.BlockSpec((d, lanes), lambda i: (0, 0)),
        ],
        out_specs=[
            pl.BlockSpec((TM, d), lambda i: (i, 0)),
            pl.BlockSpec((TM, lanes), lambda i: (i, 0)),
            pl.BlockSpec((TM, lanes), lambda i: (i, 0)),
        ],
        out_shape=[
            jax.ShapeDtypeStruct((n_rows, d), BF16),
            jax.ShapeDtypeStruct((n_rows, lanes), jnp.int32),
            jax.ShapeDtypeStruct((n_rows, lanes), F32),
        ],
        compiler_params=_cparams(("arbitrary",)),
        name="moe_router",
    )(x, g.reshape(1, d), mods, mods, w_router_pad)


def _moe_ffn_kernel(te_ref, tv_ref, x_ref, wg_ref, wu_ref, wd_ref, o_ref, acc_ref):
    t = pl.program_id(0)
    f = pl.program_id(1)
    nf = pl.num_programs(1)
    valid = tv_ref[t] == 1

    @pl.when(valid)
    def _():
        x = x_ref[...]
        g = jnp.dot(x, wg_ref[...], preferred_element_type=F32)
        u = jnp.dot(x, wu_ref[...], preferred_element_type=F32)
        a = (_silu(g) * u).astype(BF16)
        y = jnp.dot(a, wd_ref[...], preferred_element_type=F32)

        @pl.when(f == 0)
        def _():
            acc_ref[...] = y

        @pl.when(f > 0)
        def _():
            acc_ref[...] += y

    @pl.when((f == nf - 1) & valid)
    def _():
        o_ref[...] = acc_ref[...].astype(o_ref.dtype)

    @pl.when((f == nf - 1) & jnp.logical_not(valid))
    def _():
        o_ref[...] = jnp.zeros_like(o_ref)


def _moe_ffn(xs, tile_expert, tile_valid, w_gu, w_down):
    r, d = xs.shape
    n_e, _, two_f = w_gu.shape
    ff = two_f // 2
    nf = ff // MOE_BF
    n_tiles = r // MOE_TM

    def fidx(t, f, tv):
        return jnp.where(tv[t] == 1, f, nf - 1)

    grid_spec = pltpu.PrefetchScalarGridSpec(
        num_scalar_prefetch=2,
        grid=(n_tiles, nf),
        in_specs=[
            pl.BlockSpec((MOE_TM, d), lambda t, f, te, tv: (t, 0)),
            pl.BlockSpec((None, d, MOE_BF), lambda t, f, te, tv: (te[t], 0, fidx(t, f, tv))),
            pl.BlockSpec((None, d, MOE_BF), lambda t, f, te, tv: (te[t], 0, nf + fidx(t, f, tv))),
            pl.BlockSpec((None, MOE_BF, d), lambda t, f, te, tv: (te[t], fidx(t, f, tv), 0)),
        ],
        out_specs=pl.BlockSpec((MOE_TM, d), lambda t, f, te, tv: (t, 0)),
        scratch_shapes=[pltpu.VMEM((MOE_TM, d), F32)],
    )
    return pl.pallas_call(
        _moe_ffn_kernel,
        grid_spec=grid_spec,
        out_shape=jax.ShapeDtypeStruct((r, d), BF16),
        compiler_params=_cparams(("arbitrary", "arbitrary")),
        name="moe_ffn",
    )(tile_expert, tile_valid, xs, w_gu, w_gu, w_down)


def _combine_kernel(x_ref, y0_ref, y1_ref, wgt_ref, gate_ref, gf_ref, o_ref, *, final):
    w = wgt_ref[...]
    y = w[:, 0:1] * y0_ref[...].astype(F32) + w[:, 1:2] * y1_ref[...].astype(F32)
    xn = x_ref[...] + gate_ref[...] * y
    if final:
        xn = xn * lax.rsqrt(jnp.mean(xn * xn, axis=-1, keepdims=True) + EPS) * gf_ref[...]
    o_ref[...] = xn


def _combine(x, y0, y1, wgt, mods, layer, g_final, *, n_rows, geom, final):
    d = x.shape[1]
    tpb, n_batch = geom
    lanes = wgt.shape[1]
    gate_spec = pl.BlockSpec((None, None, 1, d),
                             lambda i: (layer * 6 + 5, jnp.minimum(i // tpb, n_batch), 0, 0))
    row = lambda w: pl.BlockSpec((TM, w), lambda i: (i, 0))
    out_rows = n_rows if final else x.shape[0]
    return pl.pallas_call(
        functools.partial(_combine_kernel, final=final),
        grid=(n_rows // TM,),
        in_specs=[row(d), row(d), row(d), row(lanes), gate_spec, pl.BlockSpec((1, d), lambda i: (0, 0))],
        out_specs=row(d),
        out_shape=jax.ShapeDtypeStruct((out_rows, d), F32),
        input_output_aliases={} if final else {0: 0},
        compiler_params=_cparams(("arbitrary",)),
        name="moe_combine",
    )(x, y0, y1, wgt, mods, g_final.reshape(1, d))


def _moe_layer(x, g, mods, layer, w_router, w_gu, w_down, g_final, *, n_rows, geom, final):
    d = x.shape[1]
    wr_pad = jnp.zeros((d, 128), F32).at[:, :N_EXPERTS].set(w_router)
    h, idx, wgt = _router(x, g, mods, layer, wr_pad, n_rows=n_rows, geom=geom)
    e_flat = idx[:, :2].reshape(-1)
    onehot = (e_flat[:, None] == jnp.arange(N_EXPERTS, dtype=jnp.int32)[None, :]).astype(jnp.int32)
    csum = jnp.cumsum(onehot, axis=0)
    rank = jnp.sum(onehot * csum, axis=1) - 1
    counts = csum[-1]
    padded = ((counts + MOE_TM - 1) // MOE_TM) * MOE_TM
    ends = jnp.cumsum(padded)
    starts = ends - padded
    dest = starts[e_flat] + rank
    n_assign = 2 * n_rows
    r_rows = n_assign + (N_EXPERTS - 1) * MOE_TM
    n_tiles = r_rows // MOE_TM
    src = jnp.zeros((r_rows,), jnp.int32).at[dest].set(jnp.arange(n_assign, dtype=jnp.int32) // 2)
    tile_start = jnp.arange(n_tiles, dtype=jnp.int32) * MOE_TM
    tile_valid = (tile_start < ends[-1]).astype(jnp.int32)
    last_tile = jnp.maximum(ends[-1] // MOE_TM - 1, 0)
    tile_expert = jnp.searchsorted(ends, jnp.minimum(tile_start, last_tile * MOE_TM), side="right")
    tile_expert = jnp.minimum(tile_expert, N_EXPERTS - 1).astype(jnp.int32)
    xs = jnp.take(h, src, axis=0)
    ys = _moe_ffn(xs, tile_expert, tile_valid, w_gu, w_down)
    d2 = dest.reshape(n_rows, 2)
    y0 = jnp.take(ys, d2[:, 0], axis=0)
    y1 = jnp.take(ys, d2[:, 1], axis=0)
    return _combine(x, y0, y1, wgt, mods, layer, g_final, n_rows=n_rows, geom=geom, final=final)


def kernel(x, c, ctx, c_ctx, ada_w, ada_b, norm_mix_g, norm_ffn_g, final_norm_g, ret_w_in, ret_w_o, ret_decay_f, ret_decay_b, ret_gn_f, ret_gn_b, mla_w_down, mla_g_q, mla_g_kv, mla_w_uq, mla_w_ukv, mla_w_o, sgu_w_in, sgu_g_v, sgu_w_s, sgu_b_s, sgu_w_out, ffn_w_gu, ffn_w_down, moe_w_router, moe_w_gu, moe_w_down):
    n_batch, n_lat, d = x.shape
    n_ctx = ctx.shape[1]
    depth = ada_w.shape[0]
    rows_lat = n_batch * n_lat
    m = rows_lat + n_batch * n_ctx
    geom = (n_lat // TM, n_batch)
    assert n_batch * n_ctx == TM and d == 1024 and n_batch + 1 <= 8

    xt = jnp.concatenate([x.reshape(rows_lat, d), ctx.reshape(n_batch * n_ctx, d)], axis=0)
    cond = jnp.concatenate([c, c_ctx[None, :], jnp.zeros((8 - n_batch - 1, d), F32)], axis=0)
    mods = _adaln(cond, ada_w, ada_b)
    mods = mods.reshape(depth, 8, 6, d).transpose(0, 2, 1, 3).reshape(depth * 6, 8, 1, d)

    grid_rows = n_lat // GRID_W
    ret_cos, ret_sin = _rope_tables(grid_rows, n_batch * n_ctx, RET_DK)
    mla_cos, mla_sin = _rope_tables(grid_rows, n_batch * n_ctx, MLA_DR)
    pad_r = MLA_DN - MLA_DR
    mla_cos = jnp.concatenate([mla_cos, jnp.ones((mla_cos.shape[0], pad_r), F32)], axis=1)
    mla_sin = jnp.concatenate([mla_sin, jnp.zeros((mla_sin.shape[0], pad_r), F32)], axis=1)

    for i in range(depth):
        last = i == depth - 1
        kind, jm = i % N_MIXERS, i // N_MIXERS
        rows_out = rows_lat if last else m
        if kind == 0:
            p = _proj(xt, norm_mix_g[i], mods, i, 0, ret_w_in[jm].astype(BF16), mode="ret", bn=512,
                      out_dtype=BF16, n_rows=m, geom=geom, rope=(ret_cos, ret_sin))
            lg = jnp.stack([jax.nn.log_sigmoid(ret_decay_f[jm].astype(F32)),
                            jax.nn.log_sigmoid(ret_decay_b[jm].astype(F32))])
            heads = 1024 // RET_DK
            gn = jnp.stack([ret_gn_f[jm], ret_gn_b[jm]]).reshape(2, 1, heads * RET_DV)
            yy = _retention(p, lg, gn, n_batch, n_lat, n_ctx)
            hv = heads * RET_DV
            a_specs = [pl.BlockSpec((None, TM, hv), lambda r: (0, r, 0)),
                       pl.BlockSpec((None, TM, hv), lambda r: (1, r, 0))]
            xt = _outproj([yy, yy], a_specs, ret_w_o[jm].astype(BF16), xt, mods, i, 2,
                          n_rows=rows_out, geom=geom)
        elif kind == 1:
            heads = mla_w_o.shape[1] // MLA_DV
            lq, lkv = mla_g_q.shape[1], mla_g_kv.shape[1]
            n_dn = lq + lkv + MLA_DN
            wd = jnp.zeros((d, n_dn), F32).at[:, :lq + lkv + MLA_DR].set(mla_w_down[jm]).astype(BF16)
            dn = _proj(xt, norm_mix_g[i], mods, i, 0, wd, mode="f32", bn=n_dn, out_dtype=F32,
                       n_rows=m, geom=geom)
            wq = mla_w_uq[jm].reshape(lq, heads, MLA_DN + MLA_DR)
            wq = jnp.concatenate([wq, jnp.zeros((lq, heads, pad_r), F32)], axis=2)
            wq = wq.reshape(lq, heads * 2 * MLA_DN).astype(BF16)
            wkv = mla_w_ukv[jm].reshape(lkv, heads, MLA_DN + MLA_DV)
            wk = wkv[:, :, :MLA_DN].reshape(lkv, heads * MLA_DN).astype(BF16)
            wv = wkv[:, :, MLA_DN:].reshape(lkv, heads * MLA_DV).astype(BF16)
            tpb = geom[0]
            n_lat_tiles = tpb * n_batch
            pos_idx = lambda r: (jnp.where(r < n_lat_tiles, r % tpb, tpb + (r - n_lat_tiles)), 0)
            full = lambda a: pl.BlockSpec(a.shape, lambda r: (0,) * a.ndim)
            gq = mla_g_q[jm].reshape(1, lq)
            gkv = mla_g_kv[jm].reshape(1, lkv)
            scale = float((MLA_DN + MLA_DR) ** -0.5)
            q, k, v = pl.pallas_call(
                functools.partial(_mla_up_kernel, heads=heads, scale=scale),
                grid=(m // TM,),
                in_specs=[pl.BlockSpec((TM, n_dn), lambda r: (r, 0)), full(gq), full(gkv), full(wq),
                          full(wk), full(wv), pl.BlockSpec((TM, MLA_DN), pos_idx),
                          pl.BlockSpec((TM, MLA_DN), pos_idx)],
                out_specs=[pl.BlockSpec((TM, heads * 2 * MLA_DN), lambda r: (r, 0)),
                           pl.BlockSpec((TM, heads * 2 * MLA_DN), lambda r: (r, 0)),
                           pl.BlockSpec((TM, heads * MLA_DV), lambda r: (r, 0))],
                out_shape=[jax.ShapeDtypeStruct((m, heads * 2 * MLA_DN), BF16),
                           jax.ShapeDtypeStruct((m, heads * 2 * MLA_DN), BF16),
                           jax.ShapeDtypeStruct((m, heads * MLA_DV), BF16)],
                compiler_params=_cparams(("arbitrary",)),
                name="mla_up",
            )(dn, gq, gkv, wq, wk, wv, mla_cos, mla_sin)
            ao = _mla_attention(q, k, v, n_batch, n_lat, n_ctx, heads)
            a_specs = [pl.BlockSpec((TM, heads * MLA_DV), lambda r: (r, 0))]
            xt = _outproj([ao], a_specs, mla_w_o[jm].astype(BF16), xt, mods, i, 2,
                          n_rows=rows_out, geom=geom)
        else:
            uv = _proj(xt, norm_mix_g[i], mods, i, 0, sgu_w_in[jm].astype(BF16), mode="gelu", bn=1024,
                       out_dtype=BF16, n_rows=m, geom=geom)
            xt = _sgu(uv, sgu_g_v[jm], sgu_w_s[jm].astype(BF16), sgu_b_s[jm], sgu_w_out[jm].astype(BF16),
                      xt, mods, i, geom)
        f = i // 2
        if i % 2 == 0:
            dff = ffn_w_down.shape[1]
            wgu = ffn_w_gu[f].astype(BF16)
            a = _proj(xt, norm_ffn_g[i], mods, i, 3, wgu[:, :dff], mode="swiglu", bn=dff // 2,
                      out_dtype=BF16, n_rows=rows_out, geom=geom, w2=wgu[:, dff:])
            a_specs = [pl.BlockSpec((TM, dff), lambda r: (r, 0))]
            xt = _outproj([a], a_specs, ffn_w_down[f].astype(BF16), xt, mods, i, 5,
                          n_rows=rows_out, geom=geom)
        else:
            xt = _moe_layer(xt, norm_ffn_g[i], mods, i, moe_w_router[f], moe_w_gu[f].astype(BF16),
                            moe_w_down[f].astype(BF16), final_norm_g, n_rows=rows_out, geom=geom,
                            final=last)
    if not (depth - 1) % 2:
        raise NotImplementedError("final norm is fused into the expert combine of the last layer")
    return xt.reshape(n_batch, n_lat, d)
```

```python
import functools

import jax
import jax.numpy as jnp
from jax import lax
from jax.experimental import pallas as pl
from jax.experimental.pallas import tpu as pltpu

F32 = jnp.float32
BF16 = jnp.bfloat16
HIGHEST = lax.Precision.HIGHEST
LOG2_E = 1.4426950408889634

GRID_W = 64
ROPE_BASE = 10000.0
EPS = 1e-6
RET_DK = 256
RET_DV = 512
MLA_DN = 128
MLA_DR = 64
MLA_DV = 128
SGU_GROUPS = 8
SGU_CHUNK = 128
N_EXPERTS = 8
N_MIXERS = 3

TM = 1024
RET_CHUNK = 256
ATTN_TQ = 512
ATTN_TK = 1024
ATTN_ONES_ROWS = 16
MOE_TM = 1024
MOE_SUB = 256
MOE_BF = 512
VMEM_LIMIT = 56 * 1024 * 1024


def _cparams(sem):
    return pltpu.CompilerParams(dimension_semantics=sem, vmem_limit_bytes=VMEM_LIMIT)


def _silu(x):
    return x * jax.nn.sigmoid(x)


def _rope(x, cos, sin_signed, half):
    w = x.shape[-1]
    lane = lax.broadcasted_iota(jnp.int32, x.shape, 1)
    first = (lane % (2 * half)) < half
    rot = jnp.where(first, pltpu.roll(x, w - half, 1), pltpu.roll(x, half, 1))
    return x * cos + rot * sin_signed


def _rope_tables(rows, n_ctx_rows, rot_dim):
    row = jnp.repeat(jnp.arange(rows, dtype=F32), GRID_W)
    col = jnp.tile(jnp.arange(GRID_W, dtype=F32), rows)
    axis_dim = rot_dim // 2
    inv_freq = ROPE_BASE ** (-jnp.arange(0, axis_dim, 2, dtype=F32) / axis_dim)
    ang_r = row[:, None] * inv_freq
    ang_c = col[:, None] * inv_freq
    ang = jnp.concatenate([ang_r, ang_r, ang_c, ang_c], axis=-1)
    quarter = rot_dim // 4
    sign = jnp.where((jnp.arange(rot_dim) % (2 * quarter)) < quarter, -1.0, 1.0).astype(F32)
    cos = jnp.concatenate([jnp.cos(ang), jnp.ones((n_ctx_rows, rot_dim), F32)], axis=0)
    sin = jnp.concatenate([jnp.sin(ang) * sign, jnp.zeros((n_ctx_rows, rot_dim), F32)], axis=0)
    return cos, sin


def _adaln_kernel(cond_ref, w_ref, b_ref, o_ref):
    s = _silu(cond_ref[...])
    o_ref[...] = jnp.dot(s, w_ref[...], precision=HIGHEST, preferred_element_type=F32) + b_ref[...]


def _adaln(cond, ada_w, ada_b):
    depth, d, n6 = ada_w.shape
    bn = n6 // 4
    return pl.pallas_call(
        _adaln_kernel,
        grid=(depth, n6 // bn),
        in_specs=[
            pl.BlockSpec((8, d), lambda l, j: (0, 0)),
            pl.BlockSpec((None, d, bn), lambda l, j: (l, 0, j)),
            pl.BlockSpec((None, 1, bn), lambda l, j: (l, 0, j)),
        ],
        out_specs=pl.BlockSpec((None, 8, bn), lambda l, j: (l, 0, j)),
        out_shape=jax.ShapeDtypeStruct((depth, 8, n6), F32),
        compiler_params=_cparams(("arbitrary", "arbitrary")),
        name="adaln",
    )(cond, ada_w, ada_b.reshape(depth, 1, n6))


def _modnorm(x_ref, g_ref, shift_ref, scale_ref):
    xf = x_ref[...]
    y = xf * lax.rsqrt(jnp.mean(xf * xf, axis=-1, keepdims=True) + EPS)
    return (y * g_ref[...]) * (1.0 + scale_ref[...]) + shift_ref[...]


def _ret_in_kernel(x_ref, g_ref, sh_ref, sc_ref, wqk_ref, wv_ref, wg_ref, cos_ref, sin_ref,
                   qk_ref, v_ref, gate_ref, h_scr, *, n_q):
    j = pl.program_id(1)

    @pl.when(j == 0)
    def _():
        h_scr[...] = _modnorm(x_ref, g_ref, sh_ref, sc_ref).astype(BF16)

    h = h_scr[...]
    kscale = jnp.where(j >= n_q, RET_DK ** -0.5, 1.0).astype(F32)
    qk = jnp.dot(h, wqk_ref[...], preferred_element_type=F32)
    qk_ref[...] = (_rope(qk, cos_ref[...], sin_ref[...], RET_DK // 4) * kscale).astype(qk_ref.dtype)
    v_ref[...] = jnp.dot(h, wv_ref[...], preferred_element_type=F32).astype(v_ref.dtype)
    gate_ref[...] = _silu(jnp.dot(h, wg_ref[...], preferred_element_type=F32)).astype(gate_ref.dtype)


def _ret_in(x, g, mods, layer, w_in, rope, *, geom):
    m, d = x.shape
    heads = d // RET_DK
    n_q = heads
    nj = 2 * n_q
    wv_cols = (heads * RET_DV) // nj
    wg_cols = (2 * heads * RET_DV) // nj
    assert wv_cols == RET_DK and wg_cols == 2 * RET_DK
    tpb, n_batch = geom
    sh_spec, sc_spec = _mod_specs(layer, 0, tpb, n_batch)
    n_lat_tiles = tpb * n_batch
    cos, sin = rope

    def pos_idx(i, j):
        return (jnp.where(i < n_lat_tiles, i % tpb, tpb + (i - n_lat_tiles)), 0)

    return pl.pallas_call(
        functools.partial(_ret_in_kernel, n_q=n_q),
        grid=(m // TM, nj),
        in_specs=[
            pl.BlockSpec((TM, d), lambda i, j: (i, 0)),
            pl.BlockSpec((1, d), lambda i, j: (0, 0)),
            sh_spec, sc_spec,
            pl.BlockSpec((d, RET_DK), lambda i, j: (0, j)),
            pl.BlockSpec((d, RET_DK), lambda i, j: (0, nj + j)),
            pl.BlockSpec((d, 2 * RET_DK), lambda i, j: (0, nj + j)),
            pl.BlockSpec((TM, RET_DK), pos_idx),
            pl.BlockSpec((TM, RET_DK), pos_idx),
        ],
        out_specs=[
            pl.BlockSpec((TM, RET_DK), lambda i, j: (i, j)),
            pl.BlockSpec((TM, RET_DK), lambda i, j: (i, j)),
            pl.BlockSpec((TM, 2 * RET_DK), lambda i, j: (i, j)),
        ],
        out_shape=[
            jax.ShapeDtypeStruct((m, 2 * heads * RET_DK), BF16),
            jax.ShapeDtypeStruct((m, heads * RET_DV), BF16),
            jax.ShapeDtypeStruct((m, 2 * heads * RET_DV), BF16),
        ],
        scratch_shapes=[pltpu.VMEM((TM, d), BF16)],
        compiler_params=_cparams(("arbitrary", "arbitrary")),
        name="ret_in",
    )(x, g.reshape(1, d), mods, mods, w_in, w_in, w_in, cos, sin)


def _proj_kernel(*refs, mode):
    if mode == "swiglu":
        x_ref, g_ref, sh_ref, sc_ref, w_ref, w2_ref, o_ref, h_scr = refs
    else:
        x_ref, g_ref, sh_ref, sc_ref, w_ref, o_ref, h_scr = refs
    j = pl.program_id(1)

    @pl.when(j == 0)
    def _():
        h_scr[...] = _modnorm(x_ref, g_ref, sh_ref, sc_ref).astype(BF16)

    y = jnp.dot(h_scr[...], w_ref[...], preferred_element_type=F32)
    if mode == "f32":
        o_ref[...] = y
    elif mode == "gelu":
        o_ref[...] = jax.nn.gelu(y).astype(o_ref.dtype)
    elif mode == "swiglu":
        u = jnp.dot(h_scr[...], w2_ref[...], preferred_element_type=F32)
        o_ref[...] = (_silu(y) * u).astype(o_ref.dtype)


def _mod_specs(layer, k_shift, tiles_per_batch, n_batch):
    def spec(k):
        return pl.BlockSpec((None, None, 1, 1024),
                            lambda i, j: (layer * 6 + k, jnp.minimum(i // tiles_per_batch, n_batch), 0, 0))
    return spec(k_shift), spec(k_shift + 1)


def _proj(x, g, mods, layer, k_shift, w, *, mode, bn, out_dtype, n_rows, geom, w2=None):
    d = x.shape[1]
    n = w.shape[1]
    nt = n_rows // TM
    tpb, n_batch = geom
    sh_spec, sc_spec = _mod_specs(layer, k_shift, tpb, n_batch)
    in_specs = [
        pl.BlockSpec((TM, d), lambda i, j: (i, 0)),
        pl.BlockSpec((1, d), lambda i, j: (0, 0)),
        sh_spec, sc_spec,
        pl.BlockSpec((d, bn), lambda i, j: (0, j)),
    ]
    args = [x, g.reshape(1, d), mods, mods, w]
    if mode == "swiglu":
        in_specs.append(pl.BlockSpec((d, bn), lambda i, j: (0, j)))
        args.append(w2)
    return pl.pallas_call(
        functools.partial(_proj_kernel, mode=mode),
        grid=(nt, n // bn),
        in_specs=in_specs,
        out_specs=pl.BlockSpec((TM, bn), lambda i, j: (i, j)),
        out_shape=jax.ShapeDtypeStruct((n_rows, n), out_dtype),
        scratch_shapes=[pltpu.VMEM((TM, d), BF16)],
        compiler_params=_cparams(("arbitrary", "arbitrary")),
        name="proj_" + mode,
    )(*args)


def _outproj_kernel(*refs, n_a):
    a_refs = refs[:n_a]
    w_ref, x_ref, gate_ref, o_ref = refs[n_a:]
    if n_a == 1:
        a = a_refs[0][...]
    else:
        a = (a_refs[0][...].astype(F32) + a_refs[1][...].astype(F32)).astype(BF16)
    y = jnp.dot(a, w_ref[...], preferred_element_type=F32)
    o_ref[...] = x_ref[...] + gate_ref[...] * y


def _outproj(a_list, a_specs, w, x, mods, layer, k_gate, *, n_rows, geom):
    k, d = w.shape
    nt = n_rows // TM
    tpb, n_batch = geom
    gate_spec = pl.BlockSpec((None, None, 1, d),
                             lambda i: (layer * 6 + k_gate, jnp.minimum(i // tpb, n_batch), 0, 0))
    return pl.pallas_call(
        functools.partial(_outproj_kernel, n_a=len(a_list)),
        grid=(nt,),
        in_specs=list(a_specs) + [
            pl.BlockSpec((k, d), lambda i: (0, 0)),
            pl.BlockSpec((TM, d), lambda i: (i, 0)),
            gate_spec,
        ],
        out_specs=pl.BlockSpec((TM, d), lambda i: (i, 0)),
        out_shape=jax.ShapeDtypeStruct(x.shape, F32),
        input_output_aliases={len(a_list) + 1: 0},
        compiler_params=_cparams(("arbitrary",)),
        name="outproj",
    )(*a_list, w, x, mods)


def _ret_kernel(lg_ref, qf_ref, kf_ref, vf_ref, gf_ref, qb_ref, kb_ref, vb_ref, gb_ref, gnf_ref, gnb_ref,
                of_ref, ob_ref, s_scr, intra_scr, dec_scr):
    h = pl.program_id(1)
    s = pl.program_id(2)
    c = RET_CHUNK

    @pl.when(s == 0)
    def _():
        s_scr[...] = jnp.zeros_like(s_scr)
        ii = lax.broadcasted_iota(jnp.int32, (c, c), 0)
        jj = lax.broadcasted_iota(jnp.int32, (c, c), 1)
        pos = lax.broadcasted_iota(jnp.int32, (c, 1), 0).astype(F32)
        for d in range(2):
            lg = lg_ref[d, h]
            diff = ((ii - jj) if d == 0 else (jj - ii)).astype(F32)
            intra_scr[d] = jnp.where(diff >= 0, jnp.exp(lg * jnp.maximum(diff, 0.0)), 0.0)
            fpos = pos if d == 0 else (c - 1.0) - pos
            dec_scr[2 * d] = jnp.exp(lg * (fpos + 1.0))
            dec_scr[2 * d + 1] = jnp.exp(lg * ((c - 1.0) - fpos))

    dirs = ((qf_ref, kf_ref, vf_ref, gf_ref, gnf_ref, of_ref), (qb_ref, kb_ref, vb_ref, gb_ref, gnb_ref, ob_ref))
    for d, (q_ref, k_ref, v_ref, gate_ref, gn_ref, o_ref) in enumerate(dirs):
        chunk_decay = jnp.exp(lg_ref[d, h] * float(c))
        q = q_ref[...]
        k = k_ref[...]
        v = v_ref[...]
        state = s_scr[d]
        scores = lax.dot_general(q, k, (((1,), (1,)), ((), ())), preferred_element_type=F32) * intra_scr[d]
        o = jnp.dot(scores.astype(BF16), v, preferred_element_type=F32)
        o = o + dec_scr[2 * d] * jnp.dot(q, state.astype(BF16), preferred_element_type=F32)
        kd = (k.astype(F32) * dec_scr[2 * d + 1]).astype(BF16)
        s_scr[d] = state * chunk_decay + lax.dot_general(
            kd, v, (((0,), (0,)), ((), ())), preferred_element_type=F32)
        mu = jnp.mean(o, axis=-1, keepdims=True)
        oc = o - mu
        var = jnp.mean(oc * oc, axis=-1, keepdims=True)
        y = oc * lax.rsqrt(var + EPS) * gn_ref[...]
        o_ref[...] = (gate_ref[...].astype(F32) * y).astype(o_ref.dtype)


def _retention(qk, v, gates, lg, gn, n_batch, n_lat, n_ctx):
    m = qk.shape[0]
    heads = v.shape[1] // RET_DV
    c = RET_CHUNK
    lat_chunks = n_lat // c
    assert n_ctx == c
    steps = 1 + lat_chunks
    ctx_blk0 = (n_batch * n_lat) // c

    def row_blk(d, b, s):
        t = (s - 1) if d == 0 else (lat_chunks - s)
        return jnp.where(s == 0, ctx_blk0 + b, b * lat_chunks + t)

    def specs(d):
        return [
            pl.BlockSpec((c, RET_DK), lambda b, h, s: (row_blk(d, b, s), h)),
            pl.BlockSpec((c, RET_DK), lambda b, h, s: (row_blk(d, b, s), heads + h)),
            pl.BlockSpec((c, RET_DV), lambda b, h, s: (row_blk(d, b, s), h)),
            pl.BlockSpec((c, RET_DV), lambda b, h, s: (row_blk(d, b, s), heads * d + h)),
        ]

    def gn_spec(d):
        return pl.BlockSpec((None, 1, RET_DV), lambda b, h, s: (d, 0, h))

    shape = jax.ShapeDtypeStruct((m, heads * RET_DV), BF16)
    yf, yb = pl.pallas_call(
        _ret_kernel,
        grid=(n_batch, heads, steps),
        in_specs=[pl.BlockSpec(memory_space=pltpu.SMEM)] + specs(0) + specs(1) + [gn_spec(0), gn_spec(1)],
        out_specs=[pl.BlockSpec((c, RET_DV), lambda b, h, s: (row_blk(0, b, s), h)),
                   pl.BlockSpec((c, RET_DV), lambda b, h, s: (row_blk(1, b, s), h))],
        out_shape=[shape, shape],
        scratch_shapes=[pltpu.VMEM((2, RET_DK, RET_DV), F32), pltpu.VMEM((2, c, c), F32),
                        pltpu.VMEM((4, c, 1), F32)],
        compiler_params=_cparams(("arbitrary",) * 3),
        name="retention",
    )(lg, qk, qk, v, gates, qk, qk, v, gates, gn, gn)
    return yf, yb


def _mla_up_kernel(dn_ref, gq_ref, gkv_ref, wq_ref, wk_ref, wv_ref, cos_ref, sin_ref,
                   q_ref, k_ref, v_ref, *, heads, scale):
    dn = dn_ref[...]
    lq = gq_ref.shape[1]
    lkv = gkv_ref.shape[1]

    def rms(a, g):
        return (a * lax.rsqrt(jnp.mean(a * a, axis=-1, keepdims=True) + EPS) * g).astype(BF16)

    cq = rms(dn[:, :lq], gq_ref[...])
    ckv = rms(dn[:, lq:lq + lkv], gkv_ref[...])
    cos = cos_ref[...]
    sin = sin_ref[...]
    half = MLA_DR // 4
    q = jnp.dot(cq, wq_ref[...], preferred_element_type=F32)
    kn = jnp.dot(ckv, wk_ref[...], preferred_element_type=F32)
    v_ref[...] = lax.dot_general(wv_ref[...], ckv, (((1,), (1,)), ((), ())),
                                 preferred_element_type=F32).astype(v_ref.dtype)
    kr = _rope(dn[:, lq + lkv:], cos, sin, half).astype(k_ref.dtype)
    for h in range(heads):
        o = h * 2 * MLA_DN
        q_ref[:, o:o + MLA_DN] = (q[:, o:o + MLA_DN] * scale).astype(q_ref.dtype)
        qr = _rope(q[:, o + MLA_DN:o + 2 * MLA_DN], cos, sin, half)
        q_ref[:, o + MLA_DN:o + 2 * MLA_DN] = (qr * scale).astype(q_ref.dtype)
        k_ref[:, o:o + MLA_DN] = kn[:, h * MLA_DN:(h + 1) * MLA_DN].astype(k_ref.dtype)
        k_ref[:, o + MLA_DN:o + 2 * MLA_DN] = kr


def _attn_lat_kernel(q_ref, kl_ref, vtl_ref, kc_ref, vtc_ref, o_ref):
    q = q_ref[...]
    tq = q.shape[0]
    dv = vtl_ref.shape[0]
    n_chunks = kl_ref.shape[0] // ATTN_TK

    def scores(k):
        return lax.dot_general(k, q, (((1,), (1,)), ((), ())), preferred_element_type=F32)

    def accumulate(s, vt, m, acc):
        m_new = jnp.maximum(m, jnp.max(s, axis=0, keepdims=True))
        alpha = jnp.exp2(m - m_new)
        p = jnp.exp2(s - m_new).astype(BF16)
        lhs = jnp.concatenate([vt, jnp.ones((ATTN_ONES_ROWS, vt.shape[1]), BF16)], axis=0)
        return m_new, alpha * acc + jnp.dot(lhs, p, preferred_element_type=F32)

    m = jnp.full((1, tq), -jnp.inf, F32)
    acc = jnp.zeros((dv + ATTN_ONES_ROWS, tq), F32)
    s_cur = scores(kc_ref[...])
    vt_cur = vtc_ref[...]
    for c in range(n_chunks):
        r = slice(c * ATTN_TK, (c + 1) * ATTN_TK)
        s_next = scores(kl_ref[r, :])
        m, acc = accumulate(s_cur, vt_cur, m, acc)
        s_cur, vt_cur = s_next, vtl_ref[:, r]
    m, acc = accumulate(s_cur, vt_cur, m, acc)
    o_t = acc[:dv] / acc[dv:dv + 1]
    o_ref[...] = o_t.T.astype(o_ref.dtype)


def _attn_ctx_kernel(prev_ref, q_ref, kc_ref, vtc_ref, o_ref):
    del prev_ref
    s = lax.dot_general(q_ref[...], kc_ref[...], (((1,), (1,)), ((), ())), preferred_element_type=F32)
    p = jnp.exp2(s - jnp.max(s, axis=-1, keepdims=True))
    l = jnp.sum(p, axis=-1, keepdims=True)
    o = lax.dot_general(p.astype(BF16), vtc_ref[...], (((1,), (1,)), ((), ())), preferred_element_type=F32)
    o_ref[...] = (o / l).astype(o_ref.dtype)


def _mla_attention(q, k, vt, n_batch, n_lat, n_ctx, heads):
    m = q.shape[0]
    hd = 2 * MLA_DN
    nq = n_lat // ATTN_TQ
    ctx_blk0 = (n_batch * n_lat) // n_ctx
    kl = pl.BlockSpec((n_lat, hd), lambda b, h, t: (b, h))
    vl = pl.BlockSpec((MLA_DV, n_lat), lambda b, h, t: (h, b))
    kc = pl.BlockSpec((n_ctx, hd), lambda b, h, t: (ctx_blk0 + b, h))
    vc = pl.BlockSpec((MLA_DV, n_ctx), lambda b, h, t: (h, ctx_blk0 + b))
    out = pl.pallas_call(
        _attn_lat_kernel,
        grid=(n_batch, heads, nq),
        in_specs=[pl.BlockSpec((ATTN_TQ, hd), lambda b, h, t: (b * nq + t, h)), kl, vl, kc, vc],
        out_specs=pl.BlockSpec((ATTN_TQ, MLA_DV), lambda b, h, t: (b * nq + t, h)),
        out_shape=jax.ShapeDtypeStruct((m, heads * MLA_DV), BF16),
        compiler_params=_cparams(("arbitrary",) * 3),
        name="mla_attn_latent",
    )(q, k, vt, k, vt)
    return pl.pallas_call(
        _attn_ctx_kernel,
        grid=(n_batch, heads),
        in_specs=[
            pl.BlockSpec(memory_space=pl.ANY),
            pl.BlockSpec((n_ctx, hd), lambda b, h: (ctx_blk0 + b, h)),
            pl.BlockSpec((n_ctx, hd), lambda b, h: (ctx_blk0 + b, h)),
            pl.BlockSpec((MLA_DV, n_ctx), lambda b, h: (h, ctx_blk0 + b)),
        ],
        out_specs=pl.BlockSpec((n_ctx, MLA_DV), lambda b, h: (ctx_blk0 + b, h)),
        out_shape=jax.ShapeDtypeStruct((m, heads * MLA_DV), BF16),
        input_output_aliases={0: 0},
        compiler_params=_cparams(("arbitrary",) * 2),
        name="mla_attn_ctx",
    )(out, q, k, vt)


def _sgu_kernel(u_ref, v_ref, gv_ref, ws_ref, bs_ref, wo_ref, x_ref, gate_ref, o_ref, z_scr):
    sd = v_ref.shape[1]
    gw = sd // SGU_GROUPS
    for ch in range(v_ref.shape[0] // SGU_CHUNK):
        rows = slice(ch * SGU_CHUNK, (ch + 1) * SGU_CHUNK)
        v = v_ref[rows, :].astype(F32)
        vn = (v * lax.rsqrt(jnp.mean(v * v, axis=-1, keepdims=True) + EPS) * gv_ref[...]).astype(BF16)
        for g in range(SGU_GROUPS):
            cols = slice(g * gw, (g + 1) * gw)
            z = jnp.dot(ws_ref[g], vn[:, cols], preferred_element_type=F32) + bs_ref[g]
            z_scr[rows, cols] = (u_ref[rows, cols].astype(F32) * z).astype(BF16)
    y = jnp.dot(z_scr[...], wo_ref[...], preferred_element_type=F32)
    o_ref[...] = x_ref[...] + gate_ref[...] * y


def _sgu(uv, g_v, w_s, b_s, w_out, x, mods, layer, geom):
    m, d = x.shape
    sd = w_out.shape[0]
    tpb, n_batch = geom
    tms = TM // 2
    scale = TM // tms
    gate_spec = pl.BlockSpec((None, None, 1, d),
                             lambda i: (layer * 6 + 2, jnp.minimum(i // (tpb * scale), n_batch), 0, 0))
    return pl.pallas_call(
        _sgu_kernel,
        grid=(m // tms,),
        in_specs=[
            pl.BlockSpec((tms, sd), lambda i: (i, 0)),
            pl.BlockSpec((tms, sd), lambda i: (i, 1)),
            pl.BlockSpec((1, sd), lambda i: (0, 0)),
            pl.BlockSpec((SGU_GROUPS, SGU_CHUNK, SGU_CHUNK), lambda i: (0, 0, 0)),
            pl.BlockSpec((SGU_GROUPS, SGU_CHUNK, 1), lambda i: (0, 0, 0)),
            pl.BlockSpec((sd, d), lambda i: (0, 0)),
            pl.BlockSpec((tms, d), lambda i: (i, 0)),
            gate_spec,
        ],
        out_specs=pl.BlockSpec((tms, d), lambda i: (i, 0)),
        out_shape=jax.ShapeDtypeStruct(x.shape, F32),
        scratch_shapes=[pltpu.VMEM((tms, sd), BF16)],
        input_output_aliases={6: 0},
        compiler_params=_cparams(("arbitrary",)),
        name="sgu",
    )(uv, uv, g_v.reshape(1, sd), w_s, b_s.reshape(SGU_GROUPS, SGU_CHUNK, 1), w_out, x, mods)


def _router_kernel(x_ref, g_ref, sh_ref, sc_ref, wr_ref, h_ref, idx_ref, wgt_ref):
    h = _modnorm(x_ref, g_ref, sh_ref, sc_ref)
    h_ref[...] = h.astype(h_ref.dtype)
    logits = jnp.dot(h, wr_ref[...], precision=HIGHEST, preferred_element_type=F32)
    lane = lax.broadcasted_iota(jnp.int32, logits.shape, 1).astype(F32)
    neg = jnp.float32(-jnp.inf)
    big = jnp.float32(logits.shape[1])
    logits = jnp.where(lane < N_EXPERTS, logits, neg)
    m1 = jnp.max(logits, axis=-1, keepdims=True)
    i1 = jnp.min(jnp.where(logits == m1, lane, big), axis=-1, keepdims=True)
    rest = jnp.where(lane == i1, neg, logits)
    m2 = jnp.max(rest, axis=-1, keepdims=True)
    i2 = jnp.min(jnp.where(rest == m2, lane, big), axis=-1, keepdims=True)
    e2 = jnp.exp(m2 - m1)
    w1 = 1.0 / (1.0 + e2)
    w2 = e2 / (1.0 + e2)
    idx_ref[...] = jnp.where(lane == 0, i1, jnp.where(lane == 1, i2, 0.0)).astype(jnp.int32)
    wgt_ref[...] = jnp.where(lane == 0, w1, jnp.where(lane == 1, w2, 0.0))


def _router(x, g, mods, layer, w_router_pad, *, n_rows, geom):
    d = x.shape[1]
    tpb, n_batch = geom
    sh_spec, sc_spec = _mod_specs(layer, 3, tpb, n_batch)
    one = lambda f: (lambda i: f(i, 0))
    lanes = w_router_pad.shape[1]
    return pl.pallas_call(
        _router_kernel,
        grid=(n_rows // TM,),
        in_specs=[
            pl.BlockSpec((TM, d), lambda i: (i, 0)),
            pl.BlockSpec((1, d), lambda i: (0, 0)),
            pl.BlockSpec(sh_spec.block_shape, one(sh_spec.index_map)),
            pl.BlockSpec(sc_spec.block_shape, one(sc_spec.index_map)),
            pl.BlockSpec((d, lanes), lambda i: (0, 0)),
        ],
        out_specs=[
            pl.BlockSpec((TM, d), lambda i: (i, 0)),
            pl.BlockSpec((TM, lanes), lambda i: (i, 0)),
            pl.BlockSpec((TM, lanes), lambda i: (i, 0)),
        ],
        out_shape=[
            jax.ShapeDtypeStruct((n_rows, d), BF16),
            jax.ShapeDtypeStruct((n_rows, lanes), jnp.int32),
            jax.ShapeDtypeStruct((n_rows, lanes), F32),
        ],
        compiler_params=_cparams(("arbitrary",)),
        name="moe_router",
    )(x, g.reshape(1, d), mods, mods, w_router_pad)


def _moe_ffn_kernel(te_ref, tr_ref, x_ref, wgu_ref, wd_ref, o_ref, acc_ref):
    del te_ref
    t = pl.program_id(0)
    f = pl.program_id(1)
    nf = pl.num_programs(1)
    n_rows = tr_ref[t]
    bf = wd_ref.shape[0]
    for sb in range(MOE_TM // MOE_SUB):
        r = slice(sb * MOE_SUB, (sb + 1) * MOE_SUB)
        live = sb * MOE_SUB < n_rows

        @pl.when(live)
        def _():
            gu = jnp.dot(x_ref[r, :], wgu_ref[...], preferred_element_type=F32)
            a = (_silu(gu[:, :bf]) * gu[:, bf:]).astype(BF16)
            y = jnp.dot(a, wd_ref[...], preferred_element_type=F32)

            @pl.when(f == 0)
            def _():
                acc_ref[r, :] = y

            @pl.when((f > 0) & (f < nf - 1))
            def _():
                acc_ref[r, :] += y

            @pl.when(f == nf - 1)
            def _():
                o_ref[r, :] = (acc_ref[r, :] + y).astype(o_ref.dtype)

        @pl.when(jnp.logical_not(live) & (f == nf - 1))
        def _():
            o_ref[r, :] = jnp.zeros((MOE_SUB, o_ref.shape[1]), o_ref.dtype)


def _moe_ffn(xs, tile_expert, tile_rows, w_gu, w_down):
    r, d = xs.shape
    ff = w_down.shape[1]
    nf = ff // MOE_BF
    assert nf > 1
    n_tiles = r // MOE_TM

    def fidx(t, f, tr):
        return jnp.where(tr[t] > 0, f, nf - 1)

    grid_spec = pltpu.PrefetchScalarGridSpec(
        num_scalar_prefetch=2,
        grid=(n_tiles, nf),
        in_specs=[
            pl.BlockSpec((MOE_TM, d), lambda t, f, te, tr: (t, 0)),
            pl.BlockSpec((None, d, 2 * MOE_BF), lambda t, f, te, tr: (te[t], 0, fidx(t, f, tr))),
            pl.BlockSpec((None, MOE_BF, d), lambda t, f, te, tr: (te[t], fidx(t, f, tr), 0)),
        ],
        out_specs=pl.BlockSpec((MOE_TM, d), lambda t, f, te, tr: (t, 0)),
        scratch_shapes=[pltpu.VMEM((MOE_TM, d), F32)],
    )
    return pl.pallas_call(
        _moe_ffn_kernel,
        grid_spec=grid_spec,
        out_shape=jax.ShapeDtypeStruct((r, d), BF16),
        compiler_params=_cparams(("arbitrary", "arbitrary")),
        name="moe_ffn",
    )(tile_expert, tile_rows, xs, w_gu, w_down)


def _combine_kernel(x_ref, y0_ref, y1_ref, wgt_ref, gate_ref, gf_ref, o_ref, *, final):
    w = wgt_ref[...]
    y = w[:, 0:1] * y0_ref[...].astype(F32) + w[:, 1:2] * y1_ref[...].astype(F32)
    xn = x_ref[...] + gate_ref[...] * y
    if final:
        xn = xn * lax.rsqrt(jnp.mean(xn * xn, axis=-1, keepdims=True) + EPS) * gf_ref[...]
    o_ref[...] = xn


def _combine(x, y0, y1, wgt, mods, layer, g_final, *, n_rows, geom, final):
    d = x.shape[1]
    tpb, n_batch = geom
    lanes = wgt.shape[1]
    gate_spec = pl.BlockSpec((None, None, 1, d),
                             lambda i: (layer * 6 + 5, jnp.minimum(i // tpb, n_batch), 0, 0))
    row = lambda w: pl.BlockSpec((TM, w), lambda i: (i, 0))
    out_rows = n_rows if final else x.shape[0]
    return pl.pallas_call(
        functools.partial(_combine_kernel, final=final),
        grid=(n_rows // TM,),
        in_specs=[row(d), row(d), row(d), row(lanes), gate_spec, pl.BlockSpec((1, d), lambda i: (0, 0))],
        out_specs=row(d),
        out_shape=jax.ShapeDtypeStruct((out_rows, d), F32),
        input_output_aliases={} if final else {0: 0},
        compiler_params=_cparams(("arbitrary",)),
        name="moe_combine",
    )(x, y0, y1, wgt, mods, g_final.reshape(1, d))


def _moe_layer(x, g, mods, layer, w_router, w_gu, w_down, g_final, *, n_rows, geom, final):
    d = x.shape[1]
    wr_pad = jnp.zeros((d, 128), F32).at[:, :N_EXPERTS].set(w_router)
    h, idx, wgt = _router(x, g, mods, layer, wr_pad, n_rows=n_rows, geom=geom)
    e_flat = idx[:, :2].reshape(-1)
    onehot = (e_flat[:, None] == jnp.arange(N_EXPERTS, dtype=jnp.int32)[None, :]).astype(jnp.int32)
    csum = jnp.cumsum(onehot, axis=0)
    rank = jnp.sum(onehot * csum, axis=1) - 1
    counts = csum[-1]
    padded = ((counts + MOE_TM - 1) // MOE_TM) * MOE_TM
    ends = jnp.cumsum(padded)
    starts = ends - padded
    dest = starts[e_flat] + rank
    n_assign = 2 * n_rows
    r_rows = n_assign + (N_EXPERTS - 1) * MOE_TM
    n_tiles = r_rows // MOE_TM
    src = jnp.zeros((r_rows,), jnp.int32).at[dest].set(
        jnp.arange(n_assign, dtype=jnp.int32) // 2, mode="promise_in_bounds", unique_indices=True)
    tile_start = jnp.arange(n_tiles, dtype=jnp.int32) * MOE_TM
    last_tile = jnp.maximum(ends[-1] // MOE_TM - 1, 0)
    ts_used = jnp.minimum(tile_start, last_tile * MOE_TM)
    tile_expert = jnp.sum((ends[None, :] <= ts_used[:, None]).astype(jnp.int32), axis=1)
    tile_expert = jnp.minimum(tile_expert, N_EXPERTS - 1)
    real_end = (starts + counts)[tile_expert]
    tile_rows = jnp.where(tile_start < ends[-1], jnp.clip(real_end - tile_start, 0, MOE_TM), 0)
    xs = h.at[src].get(mode="promise_in_bounds")
    ys = _moe_ffn(xs, tile_expert, tile_rows.astype(jnp.int32), w_gu, w_down)
    d2 = dest.reshape(n_rows, 2)
    y0 = ys.at[d2[:, 0]].get(mode="promise_in_bounds")
    y1 = ys.at[d2[:, 1]].get(mode="promise_in_bounds")
    return _combine(x, y0, y1, wgt, mods, layer, g_final, n_rows=n_rows, geom=geom, final=final)


def kernel(x, c, ctx, c_ctx, ada_w, ada_b, norm_mix_g, norm_ffn_g, final_norm_g, ret_w_in, ret_w_o, ret_decay_f, ret_decay_b, ret_gn_f, ret_gn_b, mla_w_down, mla_g_q, mla_g_kv, mla_w_uq, mla_w_ukv, mla_w_o, sgu_w_in, sgu_g_v, sgu_w_s, sgu_b_s, sgu_w_out, ffn_w_gu, ffn_w_down, moe_w_router, moe_w_gu, moe_w_down):
    n_batch, n_lat, d = x.shape
    n_ctx = ctx.shape[1]
    depth = ada_w.shape[0]
    rows_lat = n_batch * n_lat
    m = rows_lat + n_batch * n_ctx
    geom = (n_lat // TM, n_batch)
    assert n_batch * n_ctx == TM and d == 1024 and n_batch + 1 <= 8

    xt = jnp.concatenate([x.reshape(rows_lat, d), ctx.reshape(n_batch * n_ctx, d)], axis=0)
    cond = jnp.concatenate([c, c_ctx[None, :], jnp.zeros((8 - n_batch - 1, d), F32)], axis=0)
    mods = _adaln(cond, ada_w, ada_b)
    mods = mods.reshape(depth, 8, 6, d).transpose(0, 2, 1, 3).reshape(depth * 6, 8, 1, d)

    grid_rows = n_lat // GRID_W
    ret_cos, ret_sin = _rope_tables(grid_rows, n_batch * n_ctx, RET_DK)
    mla_cos, mla_sin = _rope_tables(grid_rows, n_batch * n_ctx, MLA_DR)
    pad_r = MLA_DN - MLA_DR
    mla_cos = jnp.concatenate([mla_cos, jnp.ones((mla_cos.shape[0], pad_r), F32)], axis=1)
    mla_sin = jnp.concatenate([mla_sin, jnp.zeros((mla_sin.shape[0], pad_r), F32)], axis=1)

    for i in range(depth):
        last = i == depth - 1
        kind, jm = i % N_MIXERS, i // N_MIXERS
        rows_out = rows_lat if last else m
        if kind == 0:
            qk, vv, gates = _ret_in(xt, norm_mix_g[i], mods, i, ret_w_in[jm].astype(BF16),
                                    (ret_cos, ret_sin), geom=geom)
            lg = jnp.stack([jax.nn.log_sigmoid(ret_decay_f[jm].astype(F32)),
                            jax.nn.log_sigmoid(ret_decay_b[jm].astype(F32))])
            hv = vv.shape[1]
            gn = jnp.stack([ret_gn_f[jm], ret_gn_b[jm]]).reshape(2, 1, hv)
            yf, yb = _retention(qk, vv, gates, lg, gn, n_batch, n_lat, n_ctx)
            a_specs = [pl.BlockSpec((TM, hv), lambda r: (r, 0)), pl.BlockSpec((TM, hv), lambda r: (r, 0))]
            xt = _outproj([yf, yb], a_specs, ret_w_o[jm].astype(BF16), xt, mods, i, 2,
                          n_rows=rows_out, geom=geom)
        elif kind == 1:
            heads = mla_w_o.shape[1] // MLA_DV
            lq, lkv = mla_g_q.shape[1], mla_g_kv.shape[1]
            n_dn = lq + lkv + MLA_DN
            wd = jnp.zeros((d, n_dn), F32).at[:, :lq + lkv + MLA_DR].set(mla_w_down[jm]).astype(BF16)
            dn = _proj(xt, norm_mix_g[i], mods, i, 0, wd, mode="f32", bn=n_dn, out_dtype=F32,
                       n_rows=m, geom=geom)
            wq = mla_w_uq[jm].reshape(lq, heads, MLA_DN + MLA_DR)
            wq = jnp.concatenate([wq, jnp.zeros((lq, heads, pad_r), F32)], axis=2)
            wq = wq.reshape(lq, heads * 2 * MLA_DN).astype(BF16)
            wkv = mla_w_ukv[jm].reshape(lkv, heads, MLA_DN + MLA_DV)
            wk = wkv[:, :, :MLA_DN].reshape(lkv, heads * MLA_DN).astype(BF16)
            wv = wkv[:, :, MLA_DN:].reshape(lkv, heads * MLA_DV).T.astype(BF16)
            tpb = geom[0]
            n_lat_tiles = tpb * n_batch
            pos_idx = lambda r: (jnp.where(r < n_lat_tiles, r % tpb, tpb + (r - n_lat_tiles)), 0)
            full = lambda a: pl.BlockSpec(a.shape, lambda r: (0,) * a.ndim)
            gq = mla_g_q[jm].reshape(1, lq)
            gkv = mla_g_kv[jm].reshape(1, lkv)
            scale = float((MLA_DN + MLA_DR) ** -0.5 * LOG2_E)
            q, k, vt = pl.pallas_call(
                functools.partial(_mla_up_kernel, heads=heads, scale=scale),
                grid=(m // TM,),
                in_specs=[pl.BlockSpec((TM, n_dn), lambda r: (r, 0)), full(gq), full(gkv), full(wq),
                          full(wk), full(wv), pl.BlockSpec((TM, MLA_DN), pos_idx),
                          pl.BlockSpec((TM, MLA_DN), pos_idx)],
                out_specs=[pl.BlockSpec((TM, heads * 2 * MLA_DN), lambda r: (r, 0)),
                           pl.BlockSpec((TM, heads * 2 * MLA_DN), lambda r: (r, 0)),
                           pl.BlockSpec((heads * MLA_DV, TM), lambda r: (0, r))],
                out_shape=[jax.ShapeDtypeStruct((m, heads * 2 * MLA_DN), BF16),
                           jax.ShapeDtypeStruct((m, heads * 2 * MLA_DN), BF16),
                           jax.ShapeDtypeStruct((heads * MLA_DV, m), BF16)],
                compiler_params=_cparams(("arbitrary",)),
                name="mla_up",
            )(dn, gq, gkv, wq, wk, wv, mla_cos, mla_sin)
            ao = _mla_attention(q, k, vt, n_batch, n_lat, n_ctx, heads)
            a_specs = [pl.BlockSpec((TM, heads * MLA_DV), lambda r: (r, 0))]
            xt = _outproj([ao], a_specs, mla_w_o[jm].astype(BF16), xt, mods, i, 2,
                          n_rows=rows_out, geom=geom)
        else:
            uv = _proj(xt, norm_mix_g[i], mods, i, 0, sgu_w_in[jm].astype(BF16), mode="gelu", bn=1024,
                       out_dtype=BF16, n_rows=m, geom=geom)
            xt = _sgu(uv, sgu_g_v[jm], sgu_w_s[jm].astype(BF16), sgu_b_s[jm], sgu_w_out[jm].astype(BF16),
                      xt, mods, i, geom)
        f = i // 2
        if i % 2 == 0:
            dff = ffn_w_down.shape[1]
            wgu = ffn_w_gu[f].astype(BF16)
            a = _proj(xt, norm_ffn_g[i], mods, i, 3, wgu[:, :dff], mode="swiglu", bn=dff // 2,
                      out_dtype=BF16, n_rows=rows_out, geom=geom, w2=wgu[:, dff:])
            a_specs = [pl.BlockSpec((TM, dff), lambda r: (r, 0))]
            xt = _outproj([a], a_specs, ffn_w_down[f].astype(BF16), xt, mods, i, 5,
                          n_rows=rows_out, geom=geom)
        else:
            n_e, _, two_ff = moe_w_gu.shape[1:]
            nf = two_ff // 2 // MOE_BF
            wgu = moe_w_gu[f].reshape(n_e, d, 2, nf, MOE_BF).transpose(0, 1, 3, 2, 4)
            wgu = wgu.reshape(n_e, d, two_ff).astype(BF16)
            xt = _moe_layer(xt, norm_ffn_g[i], mods, i, moe_w_router[f], wgu,
                            moe_w_down[f].astype(BF16), final_norm_g, n_rows=rows_out, geom=geom,
                            final=last)
    if not (depth - 1) % 2:
        raise NotImplementedError("final norm is fused into the expert combine of the last layer")
    return xt.reshape(n_batch, n_lat, d)
```

```python
import functools

import jax
import jax.numpy as jnp
from jax import lax
from jax.experimental import pallas as pl
from jax.experimental.pallas import tpu as pltpu

F32 = jnp.float32
BF16 = jnp.bfloat16
HIGHEST = lax.Precision.HIGHEST
LOG2_E = 1.4426950408889634

GRID_W = 64
ROPE_BASE = 10000.0
EPS = 1e-6
RET_DK = 256
RET_DV = 512
MLA_DN = 128
MLA_DR = 64
MLA_DV = 128
SGU_GROUPS = 8
SGU_CHUNK = 128
N_EXPERTS = 8
N_MIXERS = 3

TM = 1024
ROW_CHUNK = 256
RET_CHUNK = 256
ATTN_TQ = 512
ATTN_TK = 1024
ATTN_ONES_ROWS = 16
MOE_TM = 1024
MOE_BF = 512
VMEM_LIMIT = 56 * 1024 * 1024


def _cparams(sem):
    return pltpu.CompilerParams(dimension_semantics=sem, vmem_limit_bytes=VMEM_LIMIT)


def _silu(x):
    return x * jax.nn.sigmoid(x)


def _rope(x, cos, sin_signed, half):
    w = x.shape[-1]
    lane = lax.broadcasted_iota(jnp.int32, x.shape, 1)
    first = (lane % (2 * half)) < half
    rot = jnp.where(first, pltpu.roll(x, w - half, 1), pltpu.roll(x, half, 1))
    return x * cos + rot * sin_signed


def _rope_tables(rows, n_ctx_rows, rot_dim):
    row = jnp.repeat(jnp.arange(rows, dtype=F32), GRID_W)
    col = jnp.tile(jnp.arange(GRID_W, dtype=F32), rows)
    axis_dim = rot_dim // 2
    inv_freq = ROPE_BASE ** (-jnp.arange(0, axis_dim, 2, dtype=F32) / axis_dim)
    ang_r = row[:, None] * inv_freq
    ang_c = col[:, None] * inv_freq
    ang = jnp.concatenate([ang_r, ang_r, ang_c, ang_c], axis=-1)
    quarter = rot_dim // 4
    sign = jnp.where((jnp.arange(rot_dim) % (2 * quarter)) < quarter, -1.0, 1.0).astype(F32)
    cos = jnp.concatenate([jnp.cos(ang), jnp.ones((n_ctx_rows, rot_dim), F32)], axis=0)
    sin = jnp.concatenate([jnp.sin(ang) * sign, jnp.zeros((n_ctx_rows, rot_dim), F32)], axis=0)
    return cos, sin


def _adaln_kernel(cond_ref, w_ref, b_ref, o_ref):
    s = _silu(cond_ref[...])
    o_ref[...] = jnp.dot(s, w_ref[...], precision=HIGHEST, preferred_element_type=F32) + b_ref[...]


def _adaln(cond, ada_w, ada_b):
    depth, d, n6 = ada_w.shape
    bn = n6 // 4
    return pl.pallas_call(
        _adaln_kernel,
        grid=(depth, n6 // bn),
        in_specs=[
            pl.BlockSpec((8, d), lambda l, j: (0, 0)),
            pl.BlockSpec((None, d, bn), lambda l, j: (l, 0, j)),
            pl.BlockSpec((None, 1, bn), lambda l, j: (l, 0, j)),
        ],
        out_specs=pl.BlockSpec((None, 8, bn), lambda l, j: (l, 0, j)),
        out_shape=jax.ShapeDtypeStruct((depth, 8, n6), F32),
        compiler_params=_cparams(("arbitrary", "arbitrary")),
        name="adaln",
    )(cond, ada_w, ada_b.reshape(depth, 1, n6))


def _modnorm(x_ref, g_ref, shift_ref, scale_ref):
    xf = x_ref[...]
    y = xf * lax.rsqrt(jnp.mean(xf * xf, axis=-1, keepdims=True) + EPS)
    return (y * g_ref[...]) * (1.0 + scale_ref[...]) + shift_ref[...]


def _chunk(c):
    return slice(c * ROW_CHUNK, (c + 1) * ROW_CHUNK)


def _pipelined_chunks(n_chunks, matmul, epilogue):
    pending = matmul(0)
    for c in range(n_chunks):
        following = matmul(c + 1) if c + 1 < n_chunks else None
        epilogue(c, pending)
        pending = following


def _ret_in_kernel(x_ref, g_ref, sh_ref, sc_ref, wqk_ref, wv_ref, wg_ref, cos_ref, sin_ref,
                   qk_ref, v_ref, gate_ref, h_scr, *, n_q):
    j = pl.program_id(1)

    @pl.when(j == 0)
    def _():
        h_scr[...] = _modnorm(x_ref, g_ref, sh_ref, sc_ref).astype(BF16)

    kscale = jnp.where(j >= n_q, RET_DK ** -0.5, 1.0).astype(F32)

    def matmul(c):
        h = h_scr[_chunk(c), :]
        return tuple(jnp.dot(h, w[...], preferred_element_type=F32) for w in (wqk_ref, wv_ref, wg_ref))

    def epilogue(c, ys):
        r = _chunk(c)
        qk = _rope(ys[0], cos_ref[r, :], sin_ref[r, :], RET_DK // 4)
        qk_ref[r, :] = (qk * kscale).astype(qk_ref.dtype)
        v_ref[r, :] = ys[1].astype(v_ref.dtype)
        gate_ref[r, :] = _silu(ys[2]).astype(gate_ref.dtype)

    _pipelined_chunks(h_scr.shape[0] // ROW_CHUNK, matmul, epilogue)


def _ret_in(x, g, mods, layer, w_in, rope, *, geom):
    m, d = x.shape
    heads = d // RET_DK
    n_q = heads
    nj = 2 * n_q
    wv_cols = (heads * RET_DV) // nj
    wg_cols = (2 * heads * RET_DV) // nj
    assert wv_cols == RET_DK and wg_cols == 2 * RET_DK
    tpb, n_batch = geom
    sh_spec, sc_spec = _mod_specs(layer, 0, tpb, n_batch)
    n_lat_tiles = tpb * n_batch
    cos, sin = rope

    def pos_idx(i, j):
        return (jnp.where(i < n_lat_tiles, i % tpb, tpb + (i - n_lat_tiles)), 0)

    return pl.pallas_call(
        functools.partial(_ret_in_kernel, n_q=n_q),
        grid=(m // TM, nj),
        in_specs=[
            pl.BlockSpec((TM, d), lambda i, j: (i, 0)),
            pl.BlockSpec((1, d), lambda i, j: (0, 0)),
            sh_spec, sc_spec,
            pl.BlockSpec((d, RET_DK), lambda i, j: (0, j)),
            pl.BlockSpec((d, RET_DK), lambda i, j: (0, nj + j)),
            pl.BlockSpec((d, 2 * RET_DK), lambda i, j: (0, nj + j)),
            pl.BlockSpec((TM, RET_DK), pos_idx),
            pl.BlockSpec((TM, RET_DK), pos_idx),
        ],
        out_specs=[
            pl.BlockSpec((TM, RET_DK), lambda i, j: (i, j)),
            pl.BlockSpec((TM, RET_DK), lambda i, j: (i, j)),
            pl.BlockSpec((TM, 2 * RET_DK), lambda i, j: (i, j)),
        ],
        out_shape=[
            jax.ShapeDtypeStruct((m, 2 * heads * RET_DK), BF16),
            jax.ShapeDtypeStruct((m, heads * RET_DV), BF16),
            jax.ShapeDtypeStruct((m, 2 * heads * RET_DV), BF16),
        ],
        scratch_shapes=[pltpu.VMEM((TM, d), BF16)],
        compiler_params=_cparams(("arbitrary", "arbitrary")),
        name="ret_in",
    )(x, g.reshape(1, d), mods, mods, w_in, w_in, w_in, cos, sin)


def _proj_kernel(*refs, mode):
    if mode == "swiglu":
        x_ref, g_ref, sh_ref, sc_ref, w_ref, w2_ref, o_ref, h_scr = refs
    else:
        x_ref, g_ref, sh_ref, sc_ref, w_ref, o_ref, h_scr = refs
    j = pl.program_id(1)

    @pl.when(j == 0)
    def _():
        h_scr[...] = _modnorm(x_ref, g_ref, sh_ref, sc_ref).astype(BF16)

    def matmul(c):
        h = h_scr[_chunk(c), :]
        y = jnp.dot(h, w_ref[...], preferred_element_type=F32)
        if mode == "swiglu":
            return y, jnp.dot(h, w2_ref[...], preferred_element_type=F32)
        return (y,)

    def epilogue(c, ys):
        if mode == "f32":
            o_ref[_chunk(c), :] = ys[0]
        elif mode == "gelu":
            o_ref[_chunk(c), :] = jax.nn.gelu(ys[0]).astype(o_ref.dtype)
        elif mode == "swiglu":
            o_ref[_chunk(c), :] = (_silu(ys[0]) * ys[1]).astype(o_ref.dtype)

    _pipelined_chunks(h_scr.shape[0] // ROW_CHUNK, matmul, epilogue)


def _mod_specs(layer, k_shift, tiles_per_batch, n_batch):
    def spec(k):
        return pl.BlockSpec((None, None, 1, 1024),
                            lambda i, j: (layer * 6 + k, jnp.minimum(i // tiles_per_batch, n_batch), 0, 0))
    return spec(k_shift), spec(k_shift + 1)


def _proj(x, g, mods, layer, k_shift, w, *, mode, bn, out_dtype, n_rows, geom, w2=None):
    d = x.shape[1]
    n = w.shape[1]
    nt = n_rows // TM
    tpb, n_batch = geom
    sh_spec, sc_spec = _mod_specs(layer, k_shift, tpb, n_batch)
    in_specs = [
        pl.BlockSpec((TM, d), lambda i, j: (i, 0)),
        pl.BlockSpec((1, d), lambda i, j: (0, 0)),
        sh_spec, sc_spec,
        pl.BlockSpec((d, bn), lambda i, j: (0, j)),
    ]
    args = [x, g.reshape(1, d), mods, mods, w]
    if mode == "swiglu":
        in_specs.append(pl.BlockSpec((d, bn), lambda i, j: (0, j)))
        args.append(w2)
    return pl.pallas_call(
        functools.partial(_proj_kernel, mode=mode),
        grid=(nt, n // bn),
        in_specs=in_specs,
        out_specs=pl.BlockSpec((TM, bn), lambda i, j: (i, j)),
        out_shape=jax.ShapeDtypeStruct((n_rows, n), out_dtype),
        scratch_shapes=[pltpu.VMEM((TM, d), BF16)],
        compiler_params=_cparams(("arbitrary", "arbitrary")),
        name="proj_" + mode,
    )(*args)


def _outproj_kernel(*refs, n_a):
    a_refs = refs[:n_a]
    w_ref, x_ref, gate_ref, o_ref = refs[n_a:]
    if n_a == 1:
        a = a_refs[0][...]
    else:
        a = (a_refs[0][...].astype(F32) + a_refs[1][...].astype(F32)).astype(BF16)
    y = jnp.dot(a, w_ref[...], preferred_element_type=F32)
    o_ref[...] = x_ref[...] + gate_ref[...] * y


def _outproj(a_list, a_specs, w, x, mods, layer, k_gate, *, n_rows, geom):
    k, d = w.shape
    nt = n_rows // TM
    tpb, n_batch = geom
    gate_spec = pl.BlockSpec((None, None, 1, d),
                             lambda i: (layer * 6 + k_gate, jnp.minimum(i // tpb, n_batch), 0, 0))
    return pl.pallas_call(
        functools.partial(_outproj_kernel, n_a=len(a_list)),
        grid=(nt,),
        in_specs=list(a_specs) + [
            pl.BlockSpec((k, d), lambda i: (0, 0)),
            pl.BlockSpec((TM, d), lambda i: (i, 0)),
            gate_spec,
        ],
        out_specs=pl.BlockSpec((TM, d), lambda i: (i, 0)),
        out_shape=jax.ShapeDtypeStruct(x.shape, F32),
        input_output_aliases={len(a_list) + 1: 0},
        compiler_params=_cparams(("arbitrary",)),
        name="outproj",
    )(*a_list, w, x, mods)


def _ret_kernel(lg_ref, qf_ref, kf_ref, vf_ref, gf_ref, qb_ref, kb_ref, vb_ref, gb_ref, gnf_ref, gnb_ref,
                of_ref, ob_ref, s_scr, intra_scr, dec_scr):
    h = pl.program_id(1)
    s = pl.program_id(2)
    c = RET_CHUNK

    @pl.when(s == 0)
    def _():
        s_scr[...] = jnp.zeros_like(s_scr)
        ii = lax.broadcasted_iota(jnp.int32, (c, c), 0)
        jj = lax.broadcasted_iota(jnp.int32, (c, c), 1)
        pos = lax.broadcasted_iota(jnp.int32, (c, 1), 0).astype(F32)
        for d in range(2):
            lg = lg_ref[d, h]
            diff = ((ii - jj) if d == 0 else (jj - ii)).astype(F32)
            intra_scr[d] = jnp.where(diff >= 0, jnp.exp(lg * jnp.maximum(diff, 0.0)), 0.0)
            fpos = pos if d == 0 else (c - 1.0) - pos
            dec_scr[2 * d] = jnp.exp(lg * (fpos + 1.0))
            dec_scr[2 * d + 1] = jnp.exp(lg * ((c - 1.0) - fpos))

    dirs = ((qf_ref, kf_ref, vf_ref, gf_ref, gnf_ref, of_ref), (qb_ref, kb_ref, vb_ref, gb_ref, gnb_ref, ob_ref))
    for d, (q_ref, k_ref, v_ref, gate_ref, gn_ref, o_ref) in enumerate(dirs):
        chunk_decay = jnp.exp(lg_ref[d, h] * float(c))
        q = q_ref[...]
        k = k_ref[...]
        v = v_ref[...]
        state = s_scr[d]
        scores = lax.dot_general(q, k, (((1,), (1,)), ((), ())), preferred_element_type=F32) * intra_scr[d]
        o = jnp.dot(scores.astype(BF16), v, preferred_element_type=F32)
        o = o + dec_scr[2 * d] * jnp.dot(q, state.astype(BF16), preferred_element_type=F32)
        kd = (k.astype(F32) * dec_scr[2 * d + 1]).astype(BF16)
        s_scr[d] = state * chunk_decay + lax.dot_general(
            kd, v, (((0,), (0,)), ((), ())), preferred_element_type=F32)
        mu = jnp.mean(o, axis=-1, keepdims=True)
        oc = o - mu
        var = jnp.mean(oc * oc, axis=-1, keepdims=True)
        y = oc * lax.rsqrt(var + EPS) * gn_ref[...]
        o_ref[...] = (gate_ref[...].astype(F32) * y).astype(o_ref.dtype)


def _retention(qk, v, gates, lg, gn, n_batch, n_lat, n_ctx):
    m = qk.shape[0]
    heads = v.shape[1] // RET_DV
    c = RET_CHUNK
    lat_chunks = n_lat // c
    assert n_ctx == c
    steps = 1 + lat_chunks
    ctx_blk0 = (n_batch * n_lat) // c

    def row_blk(d, b, s):
        t = (s - 1) if d == 0 else (lat_chunks - s)
        return jnp.where(s == 0, ctx_blk0 + b, b * lat_chunks + t)

    def specs(d):
        return [
            pl.BlockSpec((c, RET_DK), lambda b, h, s: (row_blk(d, b, s), h)),
            pl.BlockSpec((c, RET_DK), lambda b, h, s: (row_blk(d, b, s), heads + h)),
            pl.BlockSpec((c, RET_DV), lambda b, h, s: (row_blk(d, b, s), h)),
            pl.BlockSpec((c, RET_DV), lambda b, h, s: (row_blk(d, b, s), heads * d + h)),
        ]

    def gn_spec(d):
        return pl.BlockSpec((None, 1, RET_DV), lambda b, h, s: (d, 0, h))

    shape = jax.ShapeDtypeStruct((m, heads * RET_DV), BF16)
    yf, yb = pl.pallas_call(
        _ret_kernel,
        grid=(n_batch, heads, steps),
        in_specs=[pl.BlockSpec(memory_space=pltpu.SMEM)] + specs(0) + specs(1) + [gn_spec(0), gn_spec(1)],
        out_specs=[pl.BlockSpec((c, RET_DV), lambda b, h, s: (row_blk(0, b, s), h)),
                   pl.BlockSpec((c, RET_DV), lambda b, h, s: (row_blk(1, b, s), h))],
        out_shape=[shape, shape],
        scratch_shapes=[pltpu.VMEM((2, RET_DK, RET_DV), F32), pltpu.VMEM((2, c, c), F32),
                        pltpu.VMEM((4, c, 1), F32)],
        compiler_params=_cparams(("arbitrary",) * 3),
        name="retention",
    )(lg, qk, qk, v, gates, qk, qk, v, gates, gn, gn)
    return yf, yb


def _mla_up_kernel(dn_ref, gq_ref, gkv_ref, wq_ref, wk_ref, wv_ref, cos_ref, sin_ref,
                   q_ref, k_ref, v_ref, *, heads, scale):
    dn = dn_ref[...]
    lq = gq_ref.shape[1]
    lkv = gkv_ref.shape[1]

    def rms(a, g):
        return (a * lax.rsqrt(jnp.mean(a * a, axis=-1, keepdims=True) + EPS) * g).astype(BF16)

    cq = rms(dn[:, :lq], gq_ref[...])
    ckv = rms(dn[:, lq:lq + lkv], gkv_ref[...])
    cos = cos_ref[...]
    sin = sin_ref[...]
    half = MLA_DR // 4
    q = jnp.dot(cq, wq_ref[...], preferred_element_type=F32)
    kn = jnp.dot(ckv, wk_ref[...], preferred_element_type=F32)
    v_ref[...] = lax.dot_general(wv_ref[...], ckv, (((1,), (1,)), ((), ())),
                                 preferred_element_type=F32).astype(v_ref.dtype)
    kr = _rope(dn[:, lq + lkv:], cos, sin, half).astype(k_ref.dtype)
    for h in range(heads):
        o = h * 2 * MLA_DN
        q_ref[:, o:o + MLA_DN] = (q[:, o:o + MLA_DN] * scale).astype(q_ref.dtype)
        qr = _rope(q[:, o + MLA_DN:o + 2 * MLA_DN], cos, sin, half)
        q_ref[:, o + MLA_DN:o + 2 * MLA_DN] = (qr * scale).astype(q_ref.dtype)
        k_ref[:, o:o + MLA_DN] = kn[:, h * MLA_DN:(h + 1) * MLA_DN].astype(k_ref.dtype)
        k_ref[:, o + MLA_DN:o + 2 * MLA_DN] = kr


def _attn_lat_kernel(q_ref, kl_ref, vtl_ref, kc_ref, vtc_ref, o_ref):
    q = q_ref[...]
    tq = q.shape[0]
    dv = vtl_ref.shape[0]
    n_chunks = kl_ref.shape[0] // ATTN_TK

    def scores(k):
        return lax.dot_general(k, q, (((1,), (1,)), ((), ())), preferred_element_type=F32)

    def accumulate(s, vt, m, acc):
        m_new = jnp.maximum(m, jnp.max(s, axis=0, keepdims=True))
        alpha = jnp.exp2(m - m_new)
        p = jnp.exp2(s - m_new).astype(BF16)
        lhs = jnp.concatenate([vt, jnp.ones((ATTN_ONES_ROWS, vt.shape[1]), BF16)], axis=0)
        return m_new, alpha * acc + jnp.dot(lhs, p, preferred_element_type=F32)

    m = jnp.full((1, tq), -jnp.inf, F32)
    acc = jnp.zeros((dv + ATTN_ONES_ROWS, tq), F32)
    s_cur = scores(kc_ref[...])
    vt_cur = vtc_ref[...]
    for c in range(n_chunks):
        r = slice(c * ATTN_TK, (c + 1) * ATTN_TK)
        s_next = scores(kl_ref[r, :])
        m, acc = accumulate(s_cur, vt_cur, m, acc)
        s_cur, vt_cur = s_next, vtl_ref[:, r]
    m, acc = accumulate(s_cur, vt_cur, m, acc)
    o_t = acc[:dv] / acc[dv:dv + 1]
    o_ref[...] = o_t.T.astype(o_ref.dtype)


def _attn_ctx_kernel(prev_ref, q_ref, kc_ref, vtc_ref, o_ref):
    del prev_ref
    s = lax.dot_general(q_ref[...], kc_ref[...], (((1,), (1,)), ((), ())), preferred_element_type=F32)
    p = jnp.exp2(s - jnp.max(s, axis=-1, keepdims=True))
    l = jnp.sum(p, axis=-1, keepdims=True)
    o = lax.dot_general(p.astype(BF16), vtc_ref[...], (((1,), (1,)), ((), ())), preferred_element_type=F32)
    o_ref[...] = (o / l).astype(o_ref.dtype)


def _mla_attention(q, k, vt, n_batch, n_lat, n_ctx, heads):
    m = q.shape[0]
    hd = 2 * MLA_DN
    nq = n_lat // ATTN_TQ
    ctx_blk0 = (n_batch * n_lat) // n_ctx
    kl = pl.BlockSpec((n_lat, hd), lambda b, h, t: (b, h))
    vl = pl.BlockSpec((MLA_DV, n_lat), lambda b, h, t: (h, b))
    kc = pl.BlockSpec((n_ctx, hd), lambda b, h, t: (ctx_blk0 + b, h))
    vc = pl.BlockSpec((MLA_DV, n_ctx), lambda b, h, t: (h, ctx_blk0 + b))
    out = pl.pallas_call(
        _attn_lat_kernel,
        grid=(n_batch, heads, nq),
        in_specs=[pl.BlockSpec((ATTN_TQ, hd), lambda b, h, t: (b * nq + t, h)), kl, vl, kc, vc],
        out_specs=pl.BlockSpec((ATTN_TQ, MLA_DV), lambda b, h, t: (b * nq + t, h)),
        out_shape=jax.ShapeDtypeStruct((m, heads * MLA_DV), BF16),
        compiler_params=_cparams(("arbitrary",) * 3),
        name="mla_attn_latent",
    )(q, k, vt, k, vt)
    return pl.pallas_call(
        _attn_ctx_kernel,
        grid=(n_batch, heads),
        in_specs=[
            pl.BlockSpec(memory_space=pl.ANY),
            pl.BlockSpec((n_ctx, hd), lambda b, h: (ctx_blk0 + b, h)),
            pl.BlockSpec((n_ctx, hd), lambda b, h: (ctx_blk0 + b, h)),
            pl.BlockSpec((MLA_DV, n_ctx), lambda b, h: (h, ctx_blk0 + b)),
        ],
        out_specs=pl.BlockSpec((n_ctx, MLA_DV), lambda b, h: (ctx_blk0 + b, h)),
        out_shape=jax.ShapeDtypeStruct((m, heads * MLA_DV), BF16),
        input_output_aliases={0: 0},
        compiler_params=_cparams(("arbitrary",) * 2),
        name="mla_attn_ctx",
    )(out, q, k, vt)


def _sgu_kernel(u_ref, v_ref, gv_ref, ws_ref, bs_ref, wo_ref, x_ref, gate_ref, o_ref, z_scr):
    sd = v_ref.shape[1]
    gw = sd // SGU_GROUPS
    for ch in range(v_ref.shape[0] // SGU_CHUNK):
        rows = slice(ch * SGU_CHUNK, (ch + 1) * SGU_CHUNK)
        v = v_ref[rows, :].astype(F32)
        vn = (v * lax.rsqrt(jnp.mean(v * v, axis=-1, keepdims=True) + EPS) * gv_ref[...]).astype(BF16)
        for g in range(SGU_GROUPS):
            cols = slice(g * gw, (g + 1) * gw)
            z = jnp.dot(ws_ref[g], vn[:, cols], preferred_element_type=F32) + bs_ref[g]
            z_scr[rows, cols] = (u_ref[rows, cols].astype(F32) * z).astype(BF16)
    y = jnp.dot(z_scr[...], wo_ref[...], preferred_element_type=F32)
    o_ref[...] = x_ref[...] + gate_ref[...] * y


def _sgu(uv, g_v, w_s, b_s, w_out, x, mods, layer, geom):
    m, d = x.shape
    sd = w_out.shape[0]
    tpb, n_batch = geom
    tms = TM // 2
    scale = TM // tms
    gate_spec = pl.BlockSpec((None, None, 1, d),
                             lambda i: (layer * 6 + 2, jnp.minimum(i // (tpb * scale), n_batch), 0, 0))
    return pl.pallas_call(
        _sgu_kernel,
        grid=(m // tms,),
        in_specs=[
            pl.BlockSpec((tms, sd), lambda i: (i, 0)),
            pl.BlockSpec((tms, sd), lambda i: (i, 1)),
            pl.BlockSpec((1, sd), lambda i: (0, 0)),
            pl.BlockSpec((SGU_GROUPS, SGU_CHUNK, SGU_CHUNK), lambda i: (0, 0, 0)),
            pl.BlockSpec((SGU_GROUPS, SGU_CHUNK, 1), lambda i: (0, 0, 0)),
            pl.BlockSpec((sd, d), lambda i: (0, 0)),
            pl.BlockSpec((tms, d), lambda i: (i, 0)),
            gate_spec,
        ],
        out_specs=pl.BlockSpec((tms, d), lambda i: (i, 0)),
        out_shape=jax.ShapeDtypeStruct(x.shape, F32),
        scratch_shapes=[pltpu.VMEM((tms, sd), BF16)],
        input_output_aliases={6: 0},
        compiler_params=_cparams(("arbitrary",)),
        name="sgu",
    )(uv, uv, g_v.reshape(1, sd), w_s, b_s.reshape(SGU_GROUPS, SGU_CHUNK, 1), w_out, x, mods)


def _router_kernel(x_ref, g_ref, sh_ref, sc_ref, wr_ref, h_ref, idx_ref, wgt_ref):
    h = _modnorm(x_ref, g_ref, sh_ref, sc_ref)
    h_ref[...] = h.astype(h_ref.dtype)
    logits = jnp.dot(h, wr_ref[...], precision=HIGHEST, preferred_element_type=F32)
    lane = lax.broadcasted_iota(jnp.int32, logits.shape, 1).astype(F32)
    neg = jnp.float32(-jnp.inf)
    big = jnp.float32(logits.shape[1])
    logits = jnp.where(lane < N_EXPERTS, logits, neg)
    m1 = jnp.max(logits, axis=-1, keepdims=True)
    i1 = jnp.min(jnp.where(logits == m1, lane, big), axis=-1, keepdims=True)
    rest = jnp.where(lane == i1, neg, logits)
    m2 = jnp.max(rest, axis=-1, keepdims=True)
    i2 = jnp.min(jnp.where(rest == m2, lane, big), axis=-1, keepdims=True)
    e2 = jnp.exp(m2 - m1)
    w1 = 1.0 / (1.0 + e2)
    w2 = e2 / (1.0 + e2)
    idx_ref[...] = jnp.where(lane == 0, i1, jnp.where(lane == 1, i2, 0.0)).astype(jnp.int32)
    wgt_ref[...] = jnp.where(lane == 0, w1, jnp.where(lane == 1, w2, 0.0))


def _router(x, g, mods, layer, w_router_pad, *, n_rows, geom):
    d = x.shape[1]
    tpb, n_batch = geom
    sh_spec, sc_spec = _mod_specs(layer, 3, tpb, n_batch)
    one = lambda f: (lambda i: f(i, 0))
    lanes = w_router_pad.shape[1]
    return pl.pallas_call(
        _router_kernel,
        grid=(n_rows // TM,),
        in_specs=[
            pl.BlockSpec((TM, d), lambda i: (i, 0)),
            pl.BlockSpec((1, d), lambda i: (0, 0)),
            pl.BlockSpec(sh_spec.block_shape, one(sh_spec.index_map)),
            pl.BlockSpec(sc_spec.block_shape, one(sc_spec.index_map)),
            pl.BlockSpec((d, lanes), lambda i: (0, 0)),
        ],
        out_specs=[
            pl.BlockSpec((TM, d), lambda i: (i, 0)),
            pl.BlockSpec((TM, lanes), lambda i: (i, 0)),
            pl.BlockSpec((TM, lanes), lambda i: (i, 0)),
        ],
        out_shape=[
            jax.ShapeDtypeStruct((n_rows, d), BF16),
            jax.ShapeDtypeStruct((n_rows, lanes), jnp.int32),
            jax.ShapeDtypeStruct((n_rows, lanes), F32),
        ],
        compiler_params=_cparams(("arbitrary",)),
        name="moe_router",
    )(x, g.reshape(1, d), mods, mods, w_router_pad)


def _moe_ffn_kernel(te_ref, tr_ref, x_ref, wg_ref, wu_ref, wd_ref, o_ref, acc_ref):
    del te_ref
    t = pl.program_id(0)
    f = pl.program_id(1)
    n_rows = tr_ref[t]
    half_rows = MOE_TM // 2

    @pl.when(f == 0)
    def _():
        acc_ref[...] = jnp.zeros_like(acc_ref)

    def matmul(c):
        x = x_ref[_chunk(c), :]
        return (jnp.dot(x, wg_ref[...], preferred_element_type=F32),
                jnp.dot(x, wu_ref[...], preferred_element_type=F32))

    def epilogue(c, gu):
        a = (_silu(gu[0]) * gu[1]).astype(BF16)
        acc_ref[_chunk(c), :] += jnp.dot(a, wd_ref[...], preferred_element_type=F32)

    @pl.when(n_rows > half_rows)
    def _():
        _pipelined_chunks(MOE_TM // ROW_CHUNK, matmul, epilogue)

    @pl.when((n_rows > 0) & (n_rows <= half_rows))
    def _():
        _pipelined_chunks(half_rows // ROW_CHUNK, matmul, epilogue)

    @pl.when(f == pl.num_programs(1) - 1)
    def _():
        o_ref[...] = acc_ref[...].astype(o_ref.dtype)


def _moe_ffn(xs, tile_expert, tile_rows, w_g, w_u, w_down):
    r, d = xs.shape
    ff = w_down.shape[1]
    nf = ff // MOE_BF
    n_tiles = r // MOE_TM

    def fidx(t, f, tr):
        return jnp.where(tr[t] > 0, f, nf - 1)

    grid_spec = pltpu.PrefetchScalarGridSpec(
        num_scalar_prefetch=2,
        grid=(n_tiles, nf),
        in_specs=[
            pl.BlockSpec((MOE_TM, d), lambda t, f, te, tr: (t, 0)),
            pl.BlockSpec((None, d, MOE_BF), lambda t, f, te, tr: (te[t], 0, fidx(t, f, tr))),
            pl.BlockSpec((None, d, MOE_BF), lambda t, f, te, tr: (te[t], 0, fidx(t, f, tr))),
            pl.BlockSpec((None, MOE_BF, d), lambda t, f, te, tr: (te[t], fidx(t, f, tr), 0)),
        ],
        out_specs=pl.BlockSpec((MOE_TM, d), lambda t, f, te, tr: (t, 0)),
        scratch_shapes=[pltpu.VMEM((MOE_TM, d), F32)],
    )
    return pl.pallas_call(
        _moe_ffn_kernel,
        grid_spec=grid_spec,
        out_shape=jax.ShapeDtypeStruct((r, d), BF16),
        compiler_params=_cparams(("arbitrary", "arbitrary")),
        name="moe_ffn",
    )(tile_expert, tile_rows, xs, w_g, w_u, w_down)


def _combine_kernel(x_ref, y0_ref, y1_ref, wgt_ref, gate_ref, gf_ref, o_ref, *, final):
    w = wgt_ref[...]
    y = w[:, 0:1] * y0_ref[...].astype(F32) + w[:, 1:2] * y1_ref[...].astype(F32)
    xn = x_ref[...] + gate_ref[...] * y
    if final:
        xn = xn * lax.rsqrt(jnp.mean(xn * xn, axis=-1, keepdims=True) + EPS) * gf_ref[...]
    o_ref[...] = xn


def _combine(x, y0, y1, wgt, mods, layer, g_final, *, n_rows, geom, final):
    d = x.shape[1]
    tpb, n_batch = geom
    lanes = wgt.shape[1]
    gate_spec = pl.BlockSpec((None, None, 1, d),
                             lambda i: (layer * 6 + 5, jnp.minimum(i // tpb, n_batch), 0, 0))
    row = lambda w: pl.BlockSpec((TM, w), lambda i: (i, 0))
    out_rows = n_rows if final else x.shape[0]
    return pl.pallas_call(
        functools.partial(_combine_kernel, final=final),
        grid=(n_rows // TM,),
        in_specs=[row(d), row(d), row(d), row(lanes), gate_spec, pl.BlockSpec((1, d), lambda i: (0, 0))],
        out_specs=row(d),
        out_shape=jax.ShapeDtypeStruct((out_rows, d), F32),
        input_output_aliases={} if final else {0: 0},
        compiler_params=_cparams(("arbitrary",)),
        name="moe_combine",
    )(x, y0, y1, wgt, mods, g_final.reshape(1, d))


def _moe_layer(x, g, mods, layer, w_router, w_g, w_u, w_down, g_final, *, n_rows, geom, final):
    d = x.shape[1]
    wr_pad = jnp.zeros((d, 128), F32).at[:, :N_EXPERTS].set(w_router)
    h, idx, wgt = _router(x, g, mods, layer, wr_pad, n_rows=n_rows, geom=geom)
    e_flat = idx[:, :2].reshape(-1)
    onehot = (e_flat[:, None] == jnp.arange(N_EXPERTS, dtype=jnp.int32)[None, :]).astype(jnp.int32)
    csum = jnp.cumsum(onehot, axis=0)
    rank = jnp.sum(onehot * csum, axis=1) - 1
    counts = csum[-1]
    padded = ((counts + MOE_TM - 1) // MOE_TM) * MOE_TM
    ends = jnp.cumsum(padded)
    starts = ends - padded
    dest = starts[e_flat] + rank
    n_assign = 2 * n_rows
    r_rows = n_assign + (N_EXPERTS - 1) * MOE_TM
    n_tiles = r_rows // MOE_TM
    src = jnp.zeros((r_rows,), jnp.int32).at[dest].set(jnp.arange(n_assign, dtype=jnp.int32) // 2)
    tile_start = jnp.arange(n_tiles, dtype=jnp.int32) * MOE_TM
    last_tile = jnp.maximum(ends[-1] // MOE_TM - 1, 0)
    ts_used = jnp.minimum(tile_start, last_tile * MOE_TM)
    tile_expert = jnp.sum((ends[None, :] <= ts_used[:, None]).astype(jnp.int32), axis=1)
    tile_expert = jnp.minimum(tile_expert, N_EXPERTS - 1)
    real_end = (starts + counts)[tile_expert]
    tile_rows = jnp.where(tile_start < ends[-1], jnp.clip(real_end - tile_start, 0, MOE_TM), 0)
    xs = jnp.take(h, src, axis=0)
    ys = _moe_ffn(xs, tile_expert, tile_rows.astype(jnp.int32), w_g, w_u, w_down)
    d2 = dest.reshape(n_rows, 2)
    y0 = jnp.take(ys, d2[:, 0], axis=0)
    y1 = jnp.take(ys, d2[:, 1], axis=0)
    return _combine(x, y0, y1, wgt, mods, layer, g_final, n_rows=n_rows, geom=geom, final=final)


def kernel(x, c, ctx, c_ctx, ada_w, ada_b, norm_mix_g, norm_ffn_g, final_norm_g, ret_w_in, ret_w_o, ret_decay_f, ret_decay_b, ret_gn_f, ret_gn_b, mla_w_down, mla_g_q, mla_g_kv, mla_w_uq, mla_w_ukv, mla_w_o, sgu_w_in, sgu_g_v, sgu_w_s, sgu_b_s, sgu_w_out, ffn_w_gu, ffn_w_down, moe_w_router, moe_w_gu, moe_w_down):
    n_batch, n_lat, d = x.shape
    n_ctx = ctx.shape[1]
    depth = ada_w.shape[0]
    rows_lat = n_batch * n_lat
    m = rows_lat + n_batch * n_ctx
    geom = (n_lat // TM, n_batch)
    assert n_batch * n_ctx == TM and d == 1024 and n_batch + 1 <= 8

    xt = jnp.concatenate([x.reshape(rows_lat, d), ctx.reshape(n_batch * n_ctx, d)], axis=0)
    cond = jnp.concatenate([c, c_ctx[None, :], jnp.zeros((8 - n_batch - 1, d), F32)], axis=0)
    mods = _adaln(cond, ada_w, ada_b)
    mods = mods.reshape(depth, 8, 6, d).transpose(0, 2, 1, 3).reshape(depth * 6, 8, 1, d)

    grid_rows = n_lat // GRID_W
    ret_cos, ret_sin = _rope_tables(grid_rows, n_batch * n_ctx, RET_DK)
    mla_cos, mla_sin = _rope_tables(grid_rows, n_batch * n_ctx, MLA_DR)
    pad_r = MLA_DN - MLA_DR
    mla_cos = jnp.concatenate([mla_cos, jnp.ones((mla_cos.shape[0], pad_r), F32)], axis=1)
    mla_sin = jnp.concatenate([mla_sin, jnp.zeros((mla_sin.shape[0], pad_r), F32)], axis=1)

    for i in range(depth):
        last = i == depth - 1
        kind, jm = i % N_MIXERS, i // N_MIXERS
        rows_out = rows_lat if last else m
        if kind == 0:
            qk, vv, gates = _ret_in(xt, norm_mix_g[i], mods, i, ret_w_in[jm].astype(BF16),
                                    (ret_cos, ret_sin), geom=geom)
            lg = jnp.stack([jax.nn.log_sigmoid(ret_decay_f[jm].astype(F32)),
                            jax.nn.log_sigmoid(ret_decay_b[jm].astype(F32))])
            hv = vv.shape[1]
            gn = jnp.stack([ret_gn_f[jm], ret_gn_b[jm]]).reshape(2, 1, hv)
            yf, yb = _retention(qk, vv, gates, lg, gn, n_batch, n_lat, n_ctx)
            a_specs = [pl.BlockSpec((TM, hv), lambda r: (r, 0)), pl.BlockSpec((TM, hv), lambda r: (r, 0))]
            xt = _outproj([yf, yb], a_specs, ret_w_o[jm].astype(BF16), xt, mods, i, 2,
                          n_rows=rows_out, geom=geom)
        elif kind == 1:
            heads = mla_w_o.shape[1] // MLA_DV
            lq, lkv = mla_g_q.shape[1], mla_g_kv.shape[1]
            n_dn = lq + lkv + MLA_DN
            wd = jnp.zeros((d, n_dn), F32).at[:, :lq + lkv + MLA_DR].set(mla_w_down[jm]).astype(BF16)
            dn = _proj(xt, norm_mix_g[i], mods, i, 0, wd, mode="f32", bn=n_dn, out_dtype=F32,
                       n_rows=m, geom=geom)
            wq = mla_w_uq[jm].reshape(lq, heads, MLA_DN + MLA_DR)
            wq = jnp.concatenate([wq, jnp.zeros((lq, heads, pad_r), F32)], axis=2)
            wq = wq.reshape(lq, heads * 2 * MLA_DN).astype(BF16)
            wkv = mla_w_ukv[jm].reshape(lkv, heads, MLA_DN + MLA_DV)
            wk = wkv[:, :, :MLA_DN].reshape(lkv, heads * MLA_DN).astype(BF16)
            wv = wkv[:, :, MLA_DN:].reshape(lkv, heads * MLA_DV).T.astype(BF16)
            tpb = geom[0]
            n_lat_tiles = tpb * n_batch
            pos_idx = lambda r: (jnp.where(r < n_lat_tiles, r % tpb, tpb + (r - n_lat_tiles)), 0)
            full = lambda a: pl.BlockSpec(a.shape, lambda r: (0,) * a.ndim)
            gq = mla_g_q[jm].reshape(1, lq)
            gkv = mla_g_kv[jm].reshape(1, lkv)
            scale = float((MLA_DN + MLA_DR) ** -0.5 * LOG2_E)
            q, k, vt = pl.pallas_call(
                functools.partial(_mla_up_kernel, heads=heads, scale=scale),
                grid=(m // TM,),
                in_specs=[pl.BlockSpec((TM, n_dn), lambda r: (r, 0)), full(gq), full(gkv), full(wq),
                          full(wk), full(wv), pl.BlockSpec((TM, MLA_DN), pos_idx),
                          pl.BlockSpec((TM, MLA_DN), pos_idx)],
                out_specs=[pl.BlockSpec((TM, heads * 2 * MLA_DN), lambda r: (r, 0)),
                           pl.BlockSpec((TM, heads * 2 * MLA_DN), lambda r: (r, 0)),
                           pl.BlockSpec((heads * MLA_DV, TM), lambda r: (0, r))],
                out_shape=[jax.ShapeDtypeStruct((m, heads * 2 * MLA_DN), BF16),
                           jax.ShapeDtypeStruct((m, heads * 2 * MLA_DN), BF16),
                           jax.ShapeDtypeStruct((heads * MLA_DV, m), BF16)],
                compiler_params=_cparams(("arbitrary",)),
                name="mla_up",
            )(dn, gq, gkv, wq, wk, wv, mla_cos, mla_sin)
            ao = _mla_attention(q, k, vt, n_batch, n_lat, n_ctx, heads)
            a_specs = [pl.BlockSpec((TM, heads * MLA_DV), lambda r: (r, 0))]
            xt = _outproj([ao], a_specs, mla_w_o[jm].astype(BF16), xt, mods, i, 2,
                          n_rows=rows_out, geom=geom)
        else:
            uv = _proj(xt, norm_mix_g[i], mods, i, 0, sgu_w_in[jm].astype(BF16), mode="gelu", bn=1024,
                       out_dtype=BF16, n_rows=m, geom=geom)
            xt = _sgu(uv, sgu_g_v[jm], sgu_w_s[jm].astype(BF16), sgu_b_s[jm], sgu_w_out[jm].astype(BF16),
                      xt, mods, i, geom)
        f = i // 2
        if i % 2 == 0:
            dff = ffn_w_down.shape[1]
            wgu = ffn_w_gu[f].astype(BF16)
            a = _proj(xt, norm_ffn_g[i], mods, i, 3, wgu[:, :dff], mode="swiglu", bn=dff // 2,
                      out_dtype=BF16, n_rows=rows_out, geom=geom, w2=wgu[:, dff:])
            a_specs = [pl.BlockSpec((TM, dff), lambda r: (r, 0))]
            xt = _outproj([a], a_specs, ffn_w_down[f].astype(BF16), xt, mods, i, 5,
                          n_rows=rows_out, geom=geom)
        else:
            eff = moe_w_down.shape[2]
            xt = _moe_layer(xt, norm_ffn_g[i], mods, i, moe_w_router[f],
                            moe_w_gu[f, :, :, :eff].astype(BF16), moe_w_gu[f, :, :, eff:].astype(BF16),
                            moe_w_down[f].astype(BF16), final_norm_g, n_rows=rows_out, geom=geom,
                            final=last)
    if not (depth - 1) % 2:
        raise NotImplementedError("final norm is fused into the expert combine of the last layer")
    return xt.reshape(n_batch, n_lat, d)
```

```python
import functools

import jax
import jax.numpy as jnp
from jax import lax
from jax.experimental import pallas as pl
from jax.experimental.pallas import tpu as pltpu

F32 = jnp.float32
BF16 = jnp.bfloat16
HIGHEST = lax.Precision.HIGHEST
LOG2_E = 1.4426950408889634

GRID_W = 64
ROPE_BASE = 10000.0
EPS = 1e-6
RET_DK = 256
RET_DV = 512
MLA_DN = 128
MLA_DR = 64
MLA_DV = 128
SGU_GROUPS = 8
SGU_CHUNK = 128
N_EXPERTS = 8
N_MIXERS = 3

TM = 1024
ROW_CHUNK = 256
RET_CHUNK = 256
ATTN_TQ = 512
ATTN_TK = 1024
ATTN_ONES_ROWS = 16
ATTN_MAX_LAG_LOG2 = 60.0
MOE_TM = 1024
MOE_BF = 512
VMEM_LIMIT = 56 * 1024 * 1024


def _cparams(sem):
    return pltpu.CompilerParams(dimension_semantics=sem, vmem_limit_bytes=VMEM_LIMIT)


def _silu(x):
    return x * jax.nn.sigmoid(x)


def _rope(x, cos, sin_signed, half):
    w = x.shape[-1]
    lane = lax.broadcasted_iota(jnp.int32, x.shape, 1)
    first = (lane % (2 * half)) < half
    rot = jnp.where(first, pltpu.roll(x, w - half, 1), pltpu.roll(x, half, 1))
    return x * cos + rot * sin_signed


def _rope_tables(rows, n_ctx_rows, rot_dim):
    row = jnp.repeat(jnp.arange(rows, dtype=F32), GRID_W)
    col = jnp.tile(jnp.arange(GRID_W, dtype=F32), rows)
    axis_dim = rot_dim // 2
    inv_freq = ROPE_BASE ** (-jnp.arange(0, axis_dim, 2, dtype=F32) / axis_dim)
    ang_r = row[:, None] * inv_freq
    ang_c = col[:, None] * inv_freq
    ang = jnp.concatenate([ang_r, ang_r, ang_c, ang_c], axis=-1)
    quarter = rot_dim // 4
    sign = jnp.where((jnp.arange(rot_dim) % (2 * quarter)) < quarter, -1.0, 1.0).astype(F32)
    cos = jnp.concatenate([jnp.cos(ang), jnp.ones((n_ctx_rows, rot_dim), F32)], axis=0)
    sin = jnp.concatenate([jnp.sin(ang) * sign, jnp.zeros((n_ctx_rows, rot_dim), F32)], axis=0)
    return cos, sin


def _adaln_kernel(cond_ref, w_ref, b_ref, o_ref):
    s = _silu(cond_ref[...])
    o_ref[...] = jnp.dot(s, w_ref[...], precision=HIGHEST, preferred_element_type=F32) + b_ref[...]


def _adaln(cond, ada_w, ada_b):
    depth, d, n6 = ada_w.shape
    bn = n6 // 4
    return pl.pallas_call(
        _adaln_kernel,
        grid=(depth, n6 // bn),
        in_specs=[
            pl.BlockSpec((8, d), lambda l, j: (0, 0)),
            pl.BlockSpec((None, d, bn), lambda l, j: (l, 0, j)),
            pl.BlockSpec((None, 1, bn), lambda l, j: (l, 0, j)),
        ],
        out_specs=pl.BlockSpec((None, 8, bn), lambda l, j: (l, 0, j)),
        out_shape=jax.ShapeDtypeStruct((depth, 8, n6), F32),
        compiler_params=_cparams(("arbitrary", "arbitrary")),
        name="adaln",
    )(cond, ada_w, ada_b.reshape(depth, 1, n6))


def _modnorm(x_ref, g_ref, shift_ref, scale_ref):
    xf = x_ref[...]
    y = xf * lax.rsqrt(jnp.mean(xf * xf, axis=-1, keepdims=True) + EPS)
    return (y * g_ref[...]) * (1.0 + scale_ref[...]) + shift_ref[...]


def _chunk(c):
    return slice(c * ROW_CHUNK, (c + 1) * ROW_CHUNK)


def _pipelined_chunks(n_chunks, matmul, epilogue):
    pending = matmul(0)
    for c in range(n_chunks):
        following = matmul(c + 1) if c + 1 < n_chunks else None
        epilogue(c, pending)
        pending = following


def _ret_in_kernel(x_ref, g_ref, sh_ref, sc_ref, wqk_ref, wv_ref, wg_ref, cos_ref, sin_ref,
                   qk_ref, v_ref, gate_ref, h_scr, *, n_q):
    j = pl.program_id(1)

    @pl.when(j == 0)
    def _():
        h_scr[...] = _modnorm(x_ref, g_ref, sh_ref, sc_ref).astype(BF16)

    kscale = jnp.where(j >= n_q, RET_DK ** -0.5, 1.0).astype(F32)

    def matmul(c):
        h = h_scr[_chunk(c), :]
        return tuple(jnp.dot(h, w[...], preferred_element_type=F32) for w in (wqk_ref, wv_ref, wg_ref))

    def epilogue(c, ys):
        r = _chunk(c)
        qk = _rope(ys[0], cos_ref[r, :], sin_ref[r, :], RET_DK // 4)
        qk_ref[r, :] = (qk * kscale).astype(qk_ref.dtype)
        v_ref[r, :] = ys[1].astype(v_ref.dtype)
        gate_ref[r, :] = _silu(ys[2]).astype(gate_ref.dtype)

    _pipelined_chunks(h_scr.shape[0] // ROW_CHUNK, matmul, epilogue)


def _ret_in(x, g, mods, layer, w_in, rope, *, geom):
    m, d = x.shape
    heads = d // RET_DK
    n_q = heads
    nj = 2 * n_q
    wv_cols = (heads * RET_DV) // nj
    wg_cols = (2 * heads * RET_DV) // nj
    assert wv_cols == RET_DK and wg_cols == 2 * RET_DK
    tpb, n_batch = geom
    sh_spec, sc_spec = _mod_specs(layer, 0, tpb, n_batch)
    n_lat_tiles = tpb * n_batch
    cos, sin = rope

    def pos_idx(i, j):
        return (jnp.where(i < n_lat_tiles, i % tpb, tpb + (i - n_lat_tiles)), 0)

    return pl.pallas_call(
        functools.partial(_ret_in_kernel, n_q=n_q),
        grid=(m // TM, nj),
        in_specs=[
            pl.BlockSpec((TM, d), lambda i, j: (i, 0)),
            pl.BlockSpec((1, d), lambda i, j: (0, 0)),
            sh_spec, sc_spec,
            pl.BlockSpec((d, RET_DK), lambda i, j: (0, j)),
            pl.BlockSpec((d, RET_DK), lambda i, j: (0, nj + j)),
            pl.BlockSpec((d, 2 * RET_DK), lambda i, j: (0, nj + j)),
            pl.BlockSpec((TM, RET_DK), pos_idx),
            pl.BlockSpec((TM, RET_DK), pos_idx),
        ],
        out_specs=[
            pl.BlockSpec((TM, RET_DK), lambda i, j: (i, j)),
            pl.BlockSpec((TM, RET_DK), lambda i, j: (i, j)),
            pl.BlockSpec((TM, 2 * RET_DK), lambda i, j: (i, j)),
        ],
        out_shape=[
            jax.ShapeDtypeStruct((m, 2 * heads * RET_DK), BF16),
            jax.ShapeDtypeStruct((m, heads * RET_DV), BF16),
            jax.ShapeDtypeStruct((m, 2 * heads * RET_DV), BF16),
        ],
        scratch_shapes=[pltpu.VMEM((TM, d), BF16)],
        compiler_params=_cparams(("arbitrary", "arbitrary")),
        name="ret_in",
    )(x, g.reshape(1, d), mods, mods, w_in, w_in, w_in, cos, sin)


def _proj_kernel(*refs, mode):
    if mode == "swiglu":
        x_ref, g_ref, sh_ref, sc_ref, w_ref, w2_ref, o_ref, h_scr = refs
    else:
        x_ref, g_ref, sh_ref, sc_ref, w_ref, o_ref, h_scr = refs
    j = pl.program_id(1)

    @pl.when(j == 0)
    def _():
        h_scr[...] = _modnorm(x_ref, g_ref, sh_ref, sc_ref).astype(BF16)

    def matmul(c):
        h = h_scr[_chunk(c), :]
        y = jnp.dot(h, w_ref[...], preferred_element_type=F32)
        if mode == "swiglu":
            return y, jnp.dot(h, w2_ref[...], preferred_element_type=F32)
        return (y,)

    def epilogue(c, ys):
        if mode == "f32":
            o_ref[_chunk(c), :] = ys[0]
        elif mode == "gelu":
            o_ref[_chunk(c), :] = jax.nn.gelu(ys[0]).astype(o_ref.dtype)
        elif mode == "swiglu":
            o_ref[_chunk(c), :] = (_silu(ys[0]) * ys[1]).astype(o_ref.dtype)

    _pipelined_chunks(h_scr.shape[0] // ROW_CHUNK, matmul, epilogue)


def _mod_specs(layer, k_shift, tiles_per_batch, n_batch):
    def spec(k):
        return pl.BlockSpec((None, None, 1, 1024),
                            lambda i, j: (layer * 6 + k, jnp.minimum(i // tiles_per_batch, n_batch), 0, 0))
    return spec(k_shift), spec(k_shift + 1)


def _proj(x, g, mods, layer, k_shift, w, *, mode, bn, out_dtype, n_rows, geom, w2=None):
    d = x.shape[1]
    n = w.shape[1]
    nt = n_rows // TM
    tpb, n_batch = geom
    sh_spec, sc_spec = _mod_specs(layer, k_shift, tpb, n_batch)
    in_specs = [
        pl.BlockSpec((TM, d), lambda i, j: (i, 0)),
        pl.BlockSpec((1, d), lambda i, j: (0, 0)),
        sh_spec, sc_spec,
        pl.BlockSpec((d, bn), lambda i, j: (0, j)),
    ]
    args = [x, g.reshape(1, d), mods, mods, w]
    if mode == "swiglu":
        in_specs.append(pl.BlockSpec((d, bn), lambda i, j: (0, j)))
        args.append(w2)
    return pl.pallas_call(
        functools.partial(_proj_kernel, mode=mode),
        grid=(nt, n // bn),
        in_specs=in_specs,
        out_specs=pl.BlockSpec((TM, bn), lambda i, j: (i, j)),
        out_shape=jax.ShapeDtypeStruct((n_rows, n), out_dtype),
        scratch_shapes=[pltpu.VMEM((TM, d), BF16)],
        compiler_params=_cparams(("arbitrary", "arbitrary")),
        name="proj_" + mode,
    )(*args)


def _outproj_kernel(*refs, n_a):
    a_refs = refs[:n_a]
    w_ref, x_ref, gate_ref, o_ref = refs[n_a:]
    if n_a == 1:
        a = a_refs[0][...]
    else:
        a = (a_refs[0][...].astype(F32) + a_refs[1][...].astype(F32)).astype(BF16)
    y = jnp.dot(a, w_ref[...], preferred_element_type=F32)
    o_ref[...] = x_ref[...] + gate_ref[...] * y


def _outproj(a_list, a_specs, w, x, mods, layer, k_gate, *, n_rows, geom):
    k, d = w.shape
    nt = n_rows // TM
    tpb, n_batch = geom
    gate_spec = pl.BlockSpec((None, None, 1, d),
                             lambda i: (layer * 6 + k_gate, jnp.minimum(i // tpb, n_batch), 0, 0))
    return pl.pallas_call(
        functools.partial(_outproj_kernel, n_a=len(a_list)),
        grid=(nt,),
        in_specs=list(a_specs) + [
            pl.BlockSpec((k, d), lambda i: (0, 0)),
            pl.BlockSpec((TM, d), lambda i: (i, 0)),
            gate_spec,
        ],
        out_specs=pl.BlockSpec((TM, d), lambda i: (i, 0)),
        out_shape=jax.ShapeDtypeStruct(x.shape, F32),
        input_output_aliases={len(a_list) + 1: 0},
        compiler_params=_cparams(("arbitrary",)),
        name="outproj",
    )(*a_list, w, x, mods)


def _ret_kernel(lg_ref, qf_ref, kf_ref, vf_ref, gf_ref, qb_ref, kb_ref, vb_ref, gb_ref, gnf_ref, gnb_ref,
                of_ref, ob_ref, s_scr, intra_scr, dec_scr):
    h = pl.program_id(1)
    s = pl.program_id(2)
    c = RET_CHUNK

    @pl.when(s == 0)
    def _():
        s_scr[...] = jnp.zeros_like(s_scr)
        ii = lax.broadcasted_iota(jnp.int32, (c, c), 0)
        jj = lax.broadcasted_iota(jnp.int32, (c, c), 1)
        pos = lax.broadcasted_iota(jnp.int32, (c, 1), 0).astype(F32)
        for d in range(2):
            lg = lg_ref[d, h]
            diff = ((ii - jj) if d == 0 else (jj - ii)).astype(F32)
            intra_scr[d] = jnp.where(diff >= 0, jnp.exp(lg * jnp.maximum(diff, 0.0)), 0.0)
            fpos = pos if d == 0 else (c - 1.0) - pos
            dec_scr[2 * d] = jnp.exp(lg * (fpos + 1.0))
            dec_scr[2 * d + 1] = jnp.exp(lg * ((c - 1.0) - fpos))

    dirs = ((qf_ref, kf_ref, vf_ref, gf_ref, gnf_ref, of_ref), (qb_ref, kb_ref, vb_ref, gb_ref, gnb_ref, ob_ref))
    for d, (q_ref, k_ref, v_ref, gate_ref, gn_ref, o_ref) in enumerate(dirs):
        chunk_decay = jnp.exp(lg_ref[d, h] * float(c))
        q = q_ref[...]
        k = k_ref[...]
        v = v_ref[...]
        state = s_scr[d]
        scores = lax.dot_general(q, k, (((1,), (1,)), ((), ())), preferred_element_type=F32) * intra_scr[d]
        o = jnp.dot(scores.astype(BF16), v, preferred_element_type=F32)
        o = o + dec_scr[2 * d] * jnp.dot(q, state.astype(BF16), preferred_element_type=F32)
        kd = (k.astype(F32) * dec_scr[2 * d + 1]).astype(BF16)
        s_scr[d] = state * chunk_decay + lax.dot_general(
            kd, v, (((0,), (0,)), ((), ())), preferred_element_type=F32)
        mu = jnp.mean(o, axis=-1, keepdims=True)
        oc = o - mu
        var = jnp.mean(oc * oc, axis=-1, keepdims=True)
        y = oc * lax.rsqrt(var + EPS) * gn_ref[...]
        o_ref[...] = (gate_ref[...].astype(F32) * y).astype(o_ref.dtype)


def _retention(qk, v, gates, lg, gn, n_batch, n_lat, n_ctx):
    m = qk.shape[0]
    heads = v.shape[1] // RET_DV
    c = RET_CHUNK
    lat_chunks = n_lat // c
    assert n_ctx == c
    steps = 1 + lat_chunks
    ctx_blk0 = (n_batch * n_lat) // c

    def row_blk(d, b, s):
        t = (s - 1) if d == 0 else (lat_chunks - s)
        return jnp.where(s == 0, ctx_blk0 + b, b * lat_chunks + t)

    def specs(d):
        return [
            pl.BlockSpec((c, RET_DK), lambda b, h, s: (row_blk(d, b, s), h)),
            pl.BlockSpec((c, RET_DK), lambda b, h, s: (row_blk(d, b, s), heads + h)),
            pl.BlockSpec((c, RET_DV), lambda b, h, s: (row_blk(d, b, s), h)),
            pl.BlockSpec((c, RET_DV), lambda b, h, s: (row_blk(d, b, s), heads * d + h)),
        ]

    def gn_spec(d):
        return pl.BlockSpec((None, 1, RET_DV), lambda b, h, s: (d, 0, h))

    shape = jax.ShapeDtypeStruct((m, heads * RET_DV), BF16)
    yf, yb = pl.pallas_call(
        _ret_kernel,
        grid=(n_batch, heads, steps),
        in_specs=[pl.BlockSpec(memory_space=pltpu.SMEM)] + specs(0) + specs(1) + [gn_spec(0), gn_spec(1)],
        out_specs=[pl.BlockSpec((c, RET_DV), lambda b, h, s: (row_blk(0, b, s), h)),
                   pl.BlockSpec((c, RET_DV), lambda b, h, s: (row_blk(1, b, s), h))],
        out_shape=[shape, shape],
        scratch_shapes=[pltpu.VMEM((2, RET_DK, RET_DV), F32), pltpu.VMEM((2, c, c), F32),
                        pltpu.VMEM((4, c, 1), F32)],
        compiler_params=_cparams(("arbitrary",) * 3),
        name="retention",
    )(lg, qk, qk, v, gates, qk, qk, v, gates, gn, gn)
    return yf, yb


def _mla_up_kernel(dn_ref, gq_ref, gkv_ref, wq_ref, wk_ref, wv_ref, cos_ref, sin_ref,
                   q_ref, k_ref, v_ref, *, heads, scale):
    dn = dn_ref[...]
    lq = gq_ref.shape[1]
    lkv = gkv_ref.shape[1]

    def rms(a, g):
        return (a * lax.rsqrt(jnp.mean(a * a, axis=-1, keepdims=True) + EPS) * g).astype(BF16)

    cq = rms(dn[:, :lq], gq_ref[...])
    ckv = rms(dn[:, lq:lq + lkv], gkv_ref[...])
    cos = cos_ref[...]
    sin = sin_ref[...]
    half = MLA_DR // 4
    q = jnp.dot(cq, wq_ref[...], preferred_element_type=F32)
    kn = jnp.dot(ckv, wk_ref[...], preferred_element_type=F32)
    v_ref[...] = lax.dot_general(wv_ref[...], ckv, (((1,), (1,)), ((), ())),
                                 preferred_element_type=F32).astype(v_ref.dtype)
    kr = _rope(dn[:, lq + lkv:], cos, sin, half).astype(k_ref.dtype)
    for h in range(heads):
        o = h * 2 * MLA_DN
        q_ref[:, o:o + MLA_DN] = (q[:, o:o + MLA_DN] * scale).astype(q_ref.dtype)
        qr = _rope(q[:, o + MLA_DN:o + 2 * MLA_DN], cos, sin, half)
        q_ref[:, o + MLA_DN:o + 2 * MLA_DN] = (qr * scale).astype(q_ref.dtype)
        k_ref[:, o:o + MLA_DN] = kn[:, h * MLA_DN:(h + 1) * MLA_DN].astype(k_ref.dtype)
        k_ref[:, o + MLA_DN:o + 2 * MLA_DN] = kr


def _attn_lat_kernel(q_ref, kl_ref, vtl_ref, kc_ref, vtc_ref, o_ref):
    q = q_ref[...]
    tq = q.shape[0]
    dv = vtl_ref.shape[0]
    n_chunks = kl_ref.shape[0] // ATTN_TK

    def scores(k):
        return lax.dot_general(k, q, (((1,), (1,)), ((), ())), preferred_element_type=F32)

    def weighted(vt, p):
        lhs = jnp.concatenate([vt, jnp.ones((ATTN_ONES_ROWS, vt.shape[1]), BF16)], axis=0)
        return jnp.dot(lhs, p, preferred_element_type=F32)

    def exact_max_block(s, vt, m, acc, risk):
        m_new = jnp.maximum(m, jnp.max(s, axis=0, keepdims=True))
        p = jnp.exp2(s - m_new).astype(BF16)
        return m_new, jnp.exp2(m - m_new) * acc + weighted(vt, p), risk

    def lagged_max_block(s, vt, m, acc, risk):
        p = jnp.exp2(s - m).astype(BF16)
        bmax = jnp.max(s, axis=0, keepdims=True)
        m_new = jnp.maximum(m, bmax)
        return m_new, (acc + weighted(vt, p)) * jnp.exp2(m - m_new), jnp.maximum(risk, bmax - m)

    def run(later_block):
        state = (jnp.full((1, tq), -jnp.inf, F32), jnp.zeros((dv + ATTN_ONES_ROWS, tq), F32),
                 jnp.zeros((1, tq), F32))
        s_cur = scores(kc_ref[...])
        vt_cur = vtc_ref[...]
        for c in range(n_chunks):
            r = slice(c * ATTN_TK, (c + 1) * ATTN_TK)
            s_next = scores(kl_ref[r, :])
            state = (exact_max_block if c == 0 else later_block)(s_cur, vt_cur, *state)
            s_cur, vt_cur = s_next, vtl_ref[:, r]
        _, acc, risk = later_block(s_cur, vt_cur, *state)
        o_t = acc[:dv] / acc[dv:dv + 1]
        o_ref[...] = o_t.T.astype(o_ref.dtype)
        return risk

    risk = run(lagged_max_block)

    @pl.when(jnp.max(risk) > ATTN_MAX_LAG_LOG2)
    def _():
        run(exact_max_block)


def _attn_ctx_kernel(q_ref, kc_ref, vtc_ref, o_ref):
    s = lax.dot_general(q_ref[...], kc_ref[...], (((1,), (1,)), ((), ())), preferred_element_type=F32)
    p = jnp.exp2(s - jnp.max(s, axis=-1, keepdims=True))
    l = jnp.sum(p, axis=-1, keepdims=True)
    o = lax.dot_general(p.astype(BF16), vtc_ref[...], (((1,), (1,)), ((), ())), preferred_element_type=F32)
    o_ref[...] = (o / l).astype(o_ref.dtype)


def _mla_attention(q, k, vt, n_batch, n_lat, n_ctx, heads):
    m = q.shape[0]
    hd = 2 * MLA_DN
    nq = n_lat // ATTN_TQ
    ctx_blk0 = (n_batch * n_lat) // n_ctx
    kl = pl.BlockSpec((n_lat, hd), lambda b, h, t: (b, h))
    vl = pl.BlockSpec((MLA_DV, n_lat), lambda b, h, t: (h, b))
    kc = pl.BlockSpec((n_ctx, hd), lambda b, h, t: (ctx_blk0 + b, h))
    vc = pl.BlockSpec((MLA_DV, n_ctx), lambda b, h, t: (h, ctx_blk0 + b))
    out = pl.pallas_call(
        _attn_lat_kernel,
        grid=(n_batch, heads, nq),
        in_specs=[pl.BlockSpec((ATTN_TQ, hd), lambda b, h, t: (b * nq + t, h)), kl, vl, kc, vc],
        out_specs=pl.BlockSpec((ATTN_TQ, MLA_DV), lambda b, h, t: (b * nq + t, h)),
        out_shape=jax.ShapeDtypeStruct((n_batch * n_lat, heads * MLA_DV), BF16),
        compiler_params=_cparams(("arbitrary",) * 3),
        name="mla_attn_latent",
    )(q, k, vt, k, vt)
    out_ctx = pl.pallas_call(
        _attn_ctx_kernel,
        grid=(n_batch, heads),
        in_specs=[
            pl.BlockSpec((n_ctx, hd), lambda b, h: (ctx_blk0 + b, h)),
            pl.BlockSpec((n_ctx, hd), lambda b, h: (ctx_blk0 + b, h)),
            pl.BlockSpec((MLA_DV, n_ctx), lambda b, h: (h, ctx_blk0 + b)),
        ],
        out_specs=pl.BlockSpec((n_ctx, MLA_DV), lambda b, h: (b, h)),
        out_shape=jax.ShapeDtypeStruct((n_batch * n_ctx, heads * MLA_DV), BF16),
        compiler_params=_cparams(("arbitrary",) * 2),
        name="mla_attn_ctx",
    )(q, k, vt)
    return jnp.concatenate([out, out_ctx], axis=0)


def _sgu_kernel(u_ref, v_ref, gv_ref, ws_ref, bs_ref, wo_ref, x_ref, gate_ref, o_ref, z_scr):
    sd = v_ref.shape[1]
    gw = sd // SGU_GROUPS
    for ch in range(v_ref.shape[0] // SGU_CHUNK):
        rows = slice(ch * SGU_CHUNK, (ch + 1) * SGU_CHUNK)
        v = v_ref[rows, :].astype(F32)
        vn = (v * lax.rsqrt(jnp.mean(v * v, axis=-1, keepdims=True) + EPS) * gv_ref[...]).astype(BF16)
        for g in range(SGU_GROUPS):
            cols = slice(g * gw, (g + 1) * gw)
            z = jnp.dot(ws_ref[g], vn[:, cols], preferred_element_type=F32) + bs_ref[g]
            z_scr[rows, cols] = (u_ref[rows, cols].astype(F32) * z).astype(BF16)
    y = jnp.dot(z_scr[...], wo_ref[...], preferred_element_type=F32)
    o_ref[...] = x_ref[...] + gate_ref[...] * y


def _sgu(uv, g_v, w_s, b_s, w_out, x, mods, layer, geom):
    m, d = x.shape
    sd = w_out.shape[0]
    tpb, n_batch = geom
    tms = TM // 2
    scale = TM // tms
    gate_spec = pl.BlockSpec((None, None, 1, d),
                             lambda i: (layer * 6 + 2, jnp.minimum(i // (tpb * scale), n_batch), 0, 0))
    return pl.pallas_call(
        _sgu_kernel,
        grid=(m // tms,),
        in_specs=[
            pl.BlockSpec((tms, sd), lambda i: (i, 0)),
            pl.BlockSpec((tms, sd), lambda i: (i, 1)),
            pl.BlockSpec((1, sd), lambda i: (0, 0)),
            pl.BlockSpec((SGU_GROUPS, SGU_CHUNK, SGU_CHUNK), lambda i: (0, 0, 0)),
            pl.BlockSpec((SGU_GROUPS, SGU_CHUNK, 1), lambda i: (0, 0, 0)),
            pl.BlockSpec((sd, d), lambda i: (0, 0)),
            pl.BlockSpec((tms, d), lambda i: (i, 0)),
            gate_spec,
        ],
        out_specs=pl.BlockSpec((tms, d), lambda i: (i, 0)),
        out_shape=jax.ShapeDtypeStruct(x.shape, F32),
        scratch_shapes=[pltpu.VMEM((tms, sd), BF16)],
        input_output_aliases={6: 0},
        compiler_params=_cparams(("arbitrary",)),
        name="sgu",
    )(uv, uv, g_v.reshape(1, sd), w_s, b_s.reshape(SGU_GROUPS, SGU_CHUNK, 1), w_out, x, mods)


def _router_kernel(x_ref, g_ref, sh_ref, sc_ref, wr_ref, h_ref, idx_ref, wgt_ref):
    h = _modnorm(x_ref, g_ref, sh_ref, sc_ref)
    h_ref[...] = h.astype(h_ref.dtype)
    logits = jnp.dot(h, wr_ref[...], precision=HIGHEST, preferred_element_type=F32)
    lane = lax.broadcasted_iota(jnp.int32, logits.shape, 1).astype(F32)
    neg = jnp.float32(-jnp.inf)
    big = jnp.float32(logits.shape[1])
    logits = jnp.where(lane < N_EXPERTS, logits, neg)
    m1 = jnp.max(logits, axis=-1, keepdims=True)
    i1 = jnp.min(jnp.where(logits == m1, lane, big), axis=-1, keepdims=True)
    rest = jnp.where(lane == i1, neg, logits)
    m2 = jnp.max(rest, axis=-1, keepdims=True)
    i2 = jnp.min(jnp.where(rest == m2, lane, big), axis=-1, keepdims=True)
    e2 = jnp.exp(m2 - m1)
    w1 = 1.0 / (1.0 + e2)
    w2 = e2 / (1.0 + e2)
    idx_ref[...] = jnp.where(lane == 0, i1, jnp.where(lane == 1, i2, 0.0)).astype(jnp.int32)
    wgt_ref[...] = jnp.where(lane == 0, w1, jnp.where(lane == 1, w2, 0.0))


def _router(x, g, mods, layer, w_router_pad, *, n_rows, geom):
    d = x.shape[1]
    tpb, n_batch = geom
    sh_spec, sc_spec = _mod_specs(layer, 3, tpb, n_batch)
    one = lambda f: (lambda i: f(i, 0))
    lanes = w_router_pad.shape[1]
    return pl.pallas_call(
        _router_kernel,
        grid=(n_rows // TM,),
        in_specs=[
            pl.BlockSpec((TM, d), lambda i: (i, 0)),
            pl.BlockSpec((1, d), lambda i: (0, 0)),
            pl.BlockSpec(sh_spec.block_shape, one(sh_spec.index_map)),
            pl.BlockSpec(sc_spec.block_shape, one(sc_spec.index_map)),
            pl.BlockSpec((d, lanes), lambda i: (0, 0)),
        ],
        out_specs=[
            pl.BlockSpec((TM, d), lambda i: (i, 0)),
            pl.BlockSpec((TM, lanes), lambda i: (i, 0)),
            pl.BlockSpec((TM, lanes), lambda i: (i, 0)),
        ],
        out_shape=[
            jax.ShapeDtypeStruct((n_rows, d), BF16),
            jax.ShapeDtypeStruct((n_rows, lanes), jnp.int32),
            jax.ShapeDtypeStruct((n_rows, lanes), F32),
        ],
        compiler_params=_cparams(("arbitrary",)),
        name="moe_router",
    )(x, g.reshape(1, d), mods, mods, w_router_pad)


def _moe_ffn_kernel(te_ref, tr_ref, x_ref, wg32_ref, wu32_ref, wd32_ref, o_ref, acc_ref,
                    wg_ref, wu_ref, wd_ref):
    del te_ref
    t = pl.program_id(0)
    f = pl.program_id(1)
    n_rows = tr_ref[t]
    half_rows = MOE_TM // 2

    @pl.when(f == 0)
    def _():
        acc_ref[...] = jnp.zeros_like(acc_ref)

    @pl.when(n_rows > 0)
    def _():
        wg_ref[...] = wg32_ref[...].astype(BF16)
        wu_ref[...] = wu32_ref[...].astype(BF16)
        wd_ref[...] = wd32_ref[...].astype(BF16)

    def matmul(c):
        x = x_ref[_chunk(c), :]
        return (jnp.dot(x, wg_ref[...], preferred_element_type=F32),
                jnp.dot(x, wu_ref[...], preferred_element_type=F32))

    def epilogue(c, gu):
        a = (_silu(gu[0]) * gu[1]).astype(BF16)
        acc_ref[_chunk(c), :] += jnp.dot(a, wd_ref[...], preferred_element_type=F32)

    @pl.when(n_rows > half_rows)
    def _():
        _pipelined_chunks(MOE_TM // ROW_CHUNK, matmul, epilogue)

    @pl.when((n_rows > 0) & (n_rows <= half_rows))
    def _():
        _pipelined_chunks(half_rows // ROW_CHUNK, matmul, epilogue)

    @pl.when(f == pl.num_programs(1) - 1)
    def _():
        o_ref[...] = acc_ref[...].astype(o_ref.dtype)


def _moe_ffn(xs, tile_expert, tile_rows, w_gu, w_down, lf):
    r, d = xs.shape
    ff = w_down.shape[2]
    nf = ff // MOE_BF
    n_tiles = r // MOE_TM

    def fidx(t, f, tr):
        return jnp.where(tr[t] > 0, f, nf - 1)

    grid_spec = pltpu.PrefetchScalarGridSpec(
        num_scalar_prefetch=2,
        grid=(n_tiles, nf),
        in_specs=[
            pl.BlockSpec((MOE_TM, d), lambda t, f, te, tr: (t, 0)),
            pl.BlockSpec((None, None, d, MOE_BF), lambda t, f, te, tr: (lf, te[t], 0, fidx(t, f, tr))),
            pl.BlockSpec((None, None, d, MOE_BF), lambda t, f, te, tr: (lf, te[t], 0, nf + fidx(t, f, tr))),
            pl.BlockSpec((None, None, MOE_BF, d), lambda t, f, te, tr: (lf, te[t], fidx(t, f, tr), 0)),
        ],
        out_specs=pl.BlockSpec((MOE_TM, d), lambda t, f, te, tr: (t, 0)),
        scratch_shapes=[pltpu.VMEM((MOE_TM, d), F32), pltpu.VMEM((d, MOE_BF), BF16),
                        pltpu.VMEM((d, MOE_BF), BF16), pltpu.VMEM((MOE_BF, d), BF16)],
    )
    return pl.pallas_call(
        _moe_ffn_kernel,
        grid_spec=grid_spec,
        out_shape=jax.ShapeDtypeStruct((r, d), BF16),
        compiler_params=_cparams(("arbitrary", "arbitrary")),
        name="moe_ffn",
    )(tile_expert, tile_rows, xs, w_gu, w_gu, w_down)


def _combine_kernel(x_ref, y0_ref, y1_ref, wgt_ref, gate_ref, gf_ref, o_ref, *, final):
    w = wgt_ref[...]
    y = w[:, 0:1] * y0_ref[...].astype(F32) + w[:, 1:2] * y1_ref[...].astype(F32)
    xn = x_ref[...] + gate_ref[...] * y
    if final:
        xn = xn * lax.rsqrt(jnp.mean(xn * xn, axis=-1, keepdims=True) + EPS) * gf_ref[...]
    o_ref[...] = xn


def _combine(x, y0, y1, wgt, mods, layer, g_final, *, n_rows, geom, final):
    d = x.shape[1]
    tpb, n_batch = geom
    lanes = wgt.shape[1]
    gate_spec = pl.BlockSpec((None, None, 1, d),
                             lambda i: (layer * 6 + 5, jnp.minimum(i // tpb, n_batch), 0, 0))
    row = lambda w: pl.BlockSpec((TM, w), lambda i: (i, 0))
    out_rows = n_rows if final else x.shape[0]
    return pl.pallas_call(
        functools.partial(_combine_kernel, final=final),
        grid=(n_rows // TM,),
        in_specs=[row(d), row(d), row(d), row(lanes), gate_spec, pl.BlockSpec((1, d), lambda i: (0, 0))],
        out_specs=row(d),
        out_shape=jax.ShapeDtypeStruct((out_rows, d), F32),
        input_output_aliases={} if final else {0: 0},
        compiler_params=_cparams(("arbitrary",)),
        name="moe_combine",
    )(x, y0, y1, wgt, mods, g_final.reshape(1, d))


def _moe_layer(x, g, mods, layer, w_router, w_gu, w_down, lf, g_final, *, n_rows, geom, final):
    d = x.shape[1]
    wr_pad = jnp.zeros((d, 128), F32).at[:, :N_EXPERTS].set(w_router)
    h, idx, wgt = _router(x, g, mods, layer, wr_pad, n_rows=n_rows, geom=geom)
    e_flat = idx[:, :2].reshape(-1)
    onehot = (e_flat[:, None] == jnp.arange(N_EXPERTS, dtype=jnp.int32)[None, :]).astype(jnp.int32)
    csum = jnp.cumsum(onehot, axis=0)
    rank = jnp.sum(onehot * csum, axis=1) - 1
    counts = csum[-1]
    padded = ((counts + MOE_TM - 1) // MOE_TM) * MOE_TM
    ends = jnp.cumsum(padded)
    starts = ends - padded
    dest = starts[e_flat] + rank
    n_assign = 2 * n_rows
    r_rows = n_assign + (N_EXPERTS - 1) * MOE_TM
    n_tiles = r_rows // MOE_TM
    src = (jnp.arange(r_rows, dtype=jnp.int32) % n_rows).at[dest].set(
        jnp.arange(n_assign, dtype=jnp.int32) // 2)
    tile_start = jnp.arange(n_tiles, dtype=jnp.int32) * MOE_TM
    last_tile = jnp.maximum(ends[-1] // MOE_TM - 1, 0)
    ts_used = jnp.minimum(tile_start, last_tile * MOE_TM)
    tile_expert = jnp.sum((ends[None, :] <= ts_used[:, None]).astype(jnp.int32), axis=1)
    tile_expert = jnp.minimum(tile_expert, N_EXPERTS - 1)
    real_end = (starts + counts)[tile_expert]
    tile_rows = jnp.where(tile_start < ends[-1], jnp.clip(real_end - tile_start, 0, MOE_TM), 0)
    xs = jnp.take(h, src, axis=0, mode="clip")
    ys = _moe_ffn(xs, tile_expert, tile_rows.astype(jnp.int32), w_gu, w_down, lf)
    d2 = dest.reshape(n_rows, 2)
    y0 = jnp.take(ys, d2[:, 0], axis=0, mode="clip")
    y1 = jnp.take(ys, d2[:, 1], axis=0, mode="clip")
    return _combine(x, y0, y1, wgt, mods, layer, g_final, n_rows=n_rows, geom=geom, final=final)


def kernel(x, c, ctx, c_ctx, ada_w, ada_b, norm_mix_g, norm_ffn_g, final_norm_g, ret_w_in, ret_w_o, ret_decay_f, ret_decay_b, ret_gn_f, ret_gn_b, mla_w_down, mla_g_q, mla_g_kv, mla_w_uq, mla_w_ukv, mla_w_o, sgu_w_in, sgu_g_v, sgu_w_s, sgu_b_s, sgu_w_out, ffn_w_gu, ffn_w_down, moe_w_router, moe_w_gu, moe_w_down):
    n_batch, n_lat, d = x.shape
    n_ctx = ctx.shape[1]
    depth = ada_w.shape[0]
    rows_lat = n_batch * n_lat
    m = rows_lat + n_batch * n_ctx
    geom = (n_lat // TM, n_batch)
    assert n_batch * n_ctx == TM and d == 1024 and n_batch + 1 <= 8

    xt = jnp.concatenate([x.reshape(rows_lat, d), ctx.reshape(n_batch * n_ctx, d)], axis=0)
    cond = jnp.concatenate([c, c_ctx[None, :], jnp.zeros((8 - n_batch - 1, d), F32)], axis=0)
    mods = _adaln(cond, ada_w, ada_b)
    mods = mods.reshape(depth, 8, 6, d).transpose(0, 2, 1, 3).reshape(depth * 6, 8, 1, d)

    grid_rows = n_lat // GRID_W
    ret_cos, ret_sin = _rope_tables(grid_rows, n_batch * n_ctx, RET_DK)
    mla_cos, mla_sin = _rope_tables(grid_rows, n_batch * n_ctx, MLA_DR)
    pad_r = MLA_DN - MLA_DR
    mla_cos = jnp.concatenate([mla_cos, jnp.ones((mla_cos.shape[0], pad_r), F32)], axis=1)
    mla_sin = jnp.concatenate([mla_sin, jnp.zeros((mla_sin.shape[0], pad_r), F32)], axis=1)

    for i in range(depth):
        last = i == depth - 1
        kind, jm = i % N_MIXERS, i // N_MIXERS
        rows_out = rows_lat if last else m
        if kind == 0:
            qk, vv, gates = _ret_in(xt, norm_mix_g[i], mods, i, ret_w_in[jm].astype(BF16),
                                    (ret_cos, ret_sin), geom=geom)
            lg = jnp.stack([jax.nn.log_sigmoid(ret_decay_f[jm].astype(F32)),
                            jax.nn.log_sigmoid(ret_decay_b[jm].astype(F32))])
            hv = vv.shape[1]
            gn = jnp.stack([ret_gn_f[jm], ret_gn_b[jm]]).reshape(2, 1, hv)
            yf, yb = _retention(qk, vv, gates, lg, gn, n_batch, n_lat, n_ctx)
            a_specs = [pl.BlockSpec((TM, hv), lambda r: (r, 0)), pl.BlockSpec((TM, hv), lambda r: (r, 0))]
            xt = _outproj([yf, yb], a_specs, ret_w_o[jm].astype(BF16), xt, mods, i, 2,
                          n_rows=rows_out, geom=geom)
        elif kind == 1:
            heads = mla_w_o.shape[1] // MLA_DV
            lq, lkv = mla_g_q.shape[1], mla_g_kv.shape[1]
            n_dn = lq + lkv + MLA_DN
            wd = jnp.zeros((d, n_dn), F32).at[:, :lq + lkv + MLA_DR].set(mla_w_down[jm]).astype(BF16)
            dn = _proj(xt, norm_mix_g[i], mods, i, 0, wd, mode="f32", bn=n_dn, out_dtype=F32,
                       n_rows=m, geom=geom)
            wq = mla_w_uq[jm].reshape(lq, heads, MLA_DN + MLA_DR)
            wq = jnp.concatenate([wq, jnp.zeros((lq, heads, pad_r), F32)], axis=2)
            wq = wq.reshape(lq, heads * 2 * MLA_DN).astype(BF16)
            wkv = mla_w_ukv[jm].reshape(lkv, heads, MLA_DN + MLA_DV)
            wk = wkv[:, :, :MLA_DN].reshape(lkv, heads * MLA_DN).astype(BF16)
            wv = wkv[:, :, MLA_DN:].reshape(lkv, heads * MLA_DV).T.astype(BF16)
            tpb = geom[0]
            n_lat_tiles = tpb * n_batch
            pos_idx = lambda r: (jnp.where(r < n_lat_tiles, r % tpb, tpb + (r - n_lat_tiles)), 0)
            full = lambda a: pl.BlockSpec(a.shape, lambda r: (0,) * a.ndim)
            gq = mla_g_q[jm].reshape(1, lq)
            gkv = mla_g_kv[jm].reshape(1, lkv)
            scale = float((MLA_DN + MLA_DR) ** -0.5 * LOG2_E)
            q, k, vt = pl.pallas_call(
                functools.partial(_mla_up_kernel, heads=heads, scale=scale),
                grid=(m // TM,),
                in_specs=[pl.BlockSpec((TM, n_dn), lambda r: (r, 0)), full(gq), full(gkv), full(wq),
                          full(wk), full(wv), pl.BlockSpec((TM, MLA_DN), pos_idx),
                          pl.BlockSpec((TM, MLA_DN), pos_idx)],
                out_specs=[pl.BlockSpec((TM, heads * 2 * MLA_DN), lambda r: (r, 0)),
                           pl.BlockSpec((TM, heads * 2 * MLA_DN), lambda r: (r, 0)),
                           pl.BlockSpec((heads * MLA_DV, TM), lambda r: (0, r))],
                out_shape=[jax.ShapeDtypeStruct((m, heads * 2 * MLA_DN), BF16),
                           jax.ShapeDtypeStruct((m, heads * 2 * MLA_DN), BF16),
                           jax.ShapeDtypeStruct((heads * MLA_DV, m), BF16)],
                compiler_params=_cparams(("arbitrary",)),
                name="mla_up",
            )(dn, gq, gkv, wq, wk, wv, mla_cos, mla_sin)
            ao = _mla_attention(q, k, vt, n_batch, n_lat, n_ctx, heads)
            a_specs = [pl.BlockSpec((TM, heads * MLA_DV), lambda r: (r, 0))]
            xt = _outproj([ao], a_specs, mla_w_o[jm].astype(BF16), xt, mods, i, 2,
                          n_rows=rows_out, geom=geom)
        else:
            uv = _proj(xt, norm_mix_g[i], mods, i, 0, sgu_w_in[jm].astype(BF16), mode="gelu", bn=1024,
                       out_dtype=BF16, n_rows=m, geom=geom)
            xt = _sgu(uv, sgu_g_v[jm], sgu_w_s[jm].astype(BF16), sgu_b_s[jm], sgu_w_out[jm].astype(BF16),
                      xt, mods, i, geom)
        f = i // 2
        if i % 2 == 0:
            dff = ffn_w_down.shape[1]
            wgu = ffn_w_gu[f].astype(BF16)
            a = _proj(xt, norm_ffn_g[i], mods, i, 3, wgu[:, :dff], mode="swiglu", bn=dff // 2,
                      out_dtype=BF16, n_rows=rows_out, geom=geom, w2=wgu[:, dff:])
            a_specs = [pl.BlockSpec((TM, dff), lambda r: (r, 0))]
            xt = _outproj([a], a_specs, ffn_w_down[f].astype(BF16), xt, mods, i, 5,
                          n_rows=rows_out, geom=geom)
        else:
            xt = _moe_layer(xt, norm_ffn_g[i], mods, i, moe_w_router[f], moe_w_gu, moe_w_down, f,
                            final_norm_g, n_rows=rows_out, geom=geom, final=last)
    if not (depth - 1) % 2:
        raise NotImplementedError("final norm is fused into the expert combine of the last layer")
    return xt.reshape(n_batch, n_lat, d)
```

```python
import functools

import jax
import jax.numpy as jnp
from jax import lax
from jax.experimental import pallas as pl
from jax.experimental.pallas import tpu as pltpu

F32 = jnp.float32
BF16 = jnp.bfloat16
HIGHEST = lax.Precision.HIGHEST
LOG2_E = 1.4426950408889634

GRID_W = 64
ROPE_BASE = 10000.0
EPS = 1e-6
RET_DK = 256
RET_DV = 512
MLA_DN = 128
MLA_DR = 64
MLA_DV = 128
SGU_GROUPS = 8
SGU_CHUNK = 128
N_EXPERTS = 8
N_MIXERS = 3

TM = 1024
ROW_CHUNK = 256
RET_CHUNK = 256
RET_HEADS_PER_STEP = 2
RET_IN_STEPS = 4
ATTN_TQ = 512
ATTN_TK = 1024
ATTN_ONES_ROWS = 16
ATTN_MAX_LAG_LOG2 = 60.0
MOE_TM = 1024
MOE_BF = 512
VMEM_LIMIT = 56 * 1024 * 1024


def _cparams(sem):
    return pltpu.CompilerParams(dimension_semantics=sem, vmem_limit_bytes=VMEM_LIMIT)


def _silu(x):
    return x * jax.nn.sigmoid(x)


def _rope(x, cos, sin_signed, half):
    w = x.shape[-1]
    lane = lax.broadcasted_iota(jnp.int32, x.shape, 1)
    first = (lane % (2 * half)) < half
    rot = jnp.where(first, pltpu.roll(x, w - half, 1), pltpu.roll(x, half, 1))
    return x * cos + rot * sin_signed


def _rope_tables(rows, n_ctx_rows, rot_dim):
    row = jnp.repeat(jnp.arange(rows, dtype=F32), GRID_W)
    col = jnp.tile(jnp.arange(GRID_W, dtype=F32), rows)
    axis_dim = rot_dim // 2
    inv_freq = ROPE_BASE ** (-jnp.arange(0, axis_dim, 2, dtype=F32) / axis_dim)
    ang_r = row[:, None] * inv_freq
    ang_c = col[:, None] * inv_freq
    ang = jnp.concatenate([ang_r, ang_r, ang_c, ang_c], axis=-1)
    quarter = rot_dim // 4
    sign = jnp.where((jnp.arange(rot_dim) % (2 * quarter)) < quarter, -1.0, 1.0).astype(F32)
    cos = jnp.concatenate([jnp.cos(ang), jnp.ones((n_ctx_rows, rot_dim), F32)], axis=0)
    sin = jnp.concatenate([jnp.sin(ang) * sign, jnp.zeros((n_ctx_rows, rot_dim), F32)], axis=0)
    return cos, sin


def _adaln_kernel(cond_ref, w_ref, b_ref, o_ref):
    s = _silu(cond_ref[...])
    o_ref[...] = jnp.dot(s, w_ref[...], precision=HIGHEST, preferred_element_type=F32) + b_ref[...]


def _adaln(cond, ada_w, ada_b):
    depth, d, n6 = ada_w.shape
    bn = n6 // 4
    return pl.pallas_call(
        _adaln_kernel,
        grid=(depth, n6 // bn),
        in_specs=[
            pl.BlockSpec((8, d), lambda l, j: (0, 0)),
            pl.BlockSpec((None, d, bn), lambda l, j: (l, 0, j)),
            pl.BlockSpec((None, 1, bn), lambda l, j: (l, 0, j)),
        ],
        out_specs=pl.BlockSpec((None, 8, bn), lambda l, j: (l, 0, j)),
        out_shape=jax.ShapeDtypeStruct((depth, 8, n6), F32),
        compiler_params=_cparams(("arbitrary", "arbitrary")),
        name="adaln",
    )(cond, ada_w, ada_b.reshape(depth, 1, n6))


def _modnorm(x_ref, g_ref, shift_ref, scale_ref):
    xf = x_ref[...]
    y = xf * lax.rsqrt(jnp.mean(xf * xf, axis=-1, keepdims=True) + EPS)
    return (y * g_ref[...]) * (1.0 + scale_ref[...]) + shift_ref[...]


def _chunk(c):
    return slice(c * ROW_CHUNK, (c + 1) * ROW_CHUNK)


def _pipelined_chunks(n_chunks, matmul, epilogue):
    pending = matmul(0)
    for c in range(n_chunks):
        following = matmul(c + 1) if c + 1 < n_chunks else None
        epilogue(c, pending)
        pending = following


def _ret_in_kernel(x_ref, g_ref, sh_ref, sc_ref, wqk_ref, wv_ref, wg_ref, cos_ref, sin_ref,
                   qk_ref, v_ref, gate_ref, h_scr, *, n_q):
    j = pl.program_id(1)

    @pl.when(j == 0)
    def _():
        h_scr[...] = _modnorm(x_ref, g_ref, sh_ref, sc_ref).astype(BF16)

    kscale = jnp.where(j >= n_q, RET_DK ** -0.5, 1.0).astype(F32)

    def matmul(c):
        h = h_scr[_chunk(c), :]
        return tuple(jnp.dot(h, w[...], preferred_element_type=F32) for w in (wqk_ref, wv_ref, wg_ref))

    def epilogue(c, ys):
        r = _chunk(c)
        for hs in range(ys[0].shape[1] // RET_DK):
            cols = slice(hs * RET_DK, (hs + 1) * RET_DK)
            qk = _rope(ys[0][:, cols], cos_ref[r, :], sin_ref[r, :], RET_DK // 4)
            qk_ref[r, cols] = (qk * kscale).astype(qk_ref.dtype)
        v_ref[r, :] = ys[1].astype(v_ref.dtype)
        gate_ref[r, :] = _silu(ys[2]).astype(gate_ref.dtype)

    _pipelined_chunks(h_scr.shape[0] // ROW_CHUNK, matmul, epilogue)


def _ret_in(x, g, mods, layer, w_in, rope, *, geom):
    m, d = x.shape
    heads = d // RET_DK
    nj = RET_IN_STEPS
    n_q = nj // 2
    qk_cols = (2 * heads * RET_DK) // nj
    wv_cols = (heads * RET_DV) // nj
    wg_cols = (2 * heads * RET_DV) // nj
    assert wv_cols == qk_cols and wg_cols == 2 * qk_cols and qk_cols % RET_DK == 0
    tpb, n_batch = geom
    sh_spec, sc_spec = _mod_specs(layer, 0, tpb, n_batch)
    n_lat_tiles = tpb * n_batch
    cos, sin = rope

    def pos_idx(i, j):
        return (jnp.where(i < n_lat_tiles, i % tpb, tpb + (i - n_lat_tiles)), 0)

    return pl.pallas_call(
        functools.partial(_ret_in_kernel, n_q=n_q),
        grid=(m // TM, nj),
        in_specs=[
            pl.BlockSpec((TM, d), lambda i, j: (i, 0)),
            pl.BlockSpec((1, d), lambda i, j: (0, 0)),
            sh_spec, sc_spec,
            pl.BlockSpec((d, qk_cols), lambda i, j: (0, j)),
            pl.BlockSpec((d, qk_cols), lambda i, j: (0, nj + j)),
            pl.BlockSpec((d, 2 * qk_cols), lambda i, j: (0, nj + j)),
            pl.BlockSpec((TM, RET_DK), pos_idx),
            pl.BlockSpec((TM, RET_DK), pos_idx),
        ],
        out_specs=[
            pl.BlockSpec((TM, qk_cols), lambda i, j: (i, j)),
            pl.BlockSpec((TM, qk_cols), lambda i, j: (i, j)),
            pl.BlockSpec((TM, 2 * qk_cols), lambda i, j: (i, j)),
        ],
        out_shape=[
            jax.ShapeDtypeStruct((m, 2 * heads * RET_DK), BF16),
            jax.ShapeDtypeStruct((m, heads * RET_DV), BF16),
            jax.ShapeDtypeStruct((m, 2 * heads * RET_DV), BF16),
        ],
        scratch_shapes=[pltpu.VMEM((TM, d), BF16)],
        compiler_params=_cparams(("arbitrary", "arbitrary")),
        name="ret_in",
    )(x, g.reshape(1, d), mods, mods, w_in, w_in, w_in, cos, sin)


def _proj_kernel(*refs, mode):
    if mode == "swiglu":
        x_ref, g_ref, sh_ref, sc_ref, w_ref, w2_ref, o_ref, h_scr = refs
    else:
        x_ref, g_ref, sh_ref, sc_ref, w_ref, o_ref, h_scr = refs
    j = pl.program_id(1)

    @pl.when(j == 0)
    def _():
        h_scr[...] = _modnorm(x_ref, g_ref, sh_ref, sc_ref).astype(BF16)

    def matmul(c):
        h = h_scr[_chunk(c), :]
        y = jnp.dot(h, w_ref[...], preferred_element_type=F32)
        if mode == "swiglu":
            return y, jnp.dot(h, w2_ref[...], preferred_element_type=F32)
        return (y,)

    def epilogue(c, ys):
        if mode == "f32":
            o_ref[_chunk(c), :] = ys[0]
        elif mode == "gelu":
            o_ref[_chunk(c), :] = jax.nn.gelu(ys[0]).astype(o_ref.dtype)
        elif mode == "swiglu":
            o_ref[_chunk(c), :] = (_silu(ys[0]) * ys[1]).astype(o_ref.dtype)

    _pipelined_chunks(h_scr.shape[0] // ROW_CHUNK, matmul, epilogue)


def _mod_specs(layer, k_shift, tiles_per_batch, n_batch):
    def spec(k):
        return pl.BlockSpec((None, None, 1, 1024),
                            lambda i, j: (layer * 6 + k, jnp.minimum(i // tiles_per_batch, n_batch), 0, 0))
    return spec(k_shift), spec(k_shift + 1)


def _proj(x, g, mods, layer, k_shift, w, *, mode, bn, out_dtype, n_rows, geom, w2=None):
    d = x.shape[1]
    n = w.shape[1]
    nt = n_rows // TM
    tpb, n_batch = geom
    sh_spec, sc_spec = _mod_specs(layer, k_shift, tpb, n_batch)
    in_specs = [
        pl.BlockSpec((TM, d), lambda i, j: (i, 0)),
        pl.BlockSpec((1, d), lambda i, j: (0, 0)),
        sh_spec, sc_spec,
        pl.BlockSpec((d, bn), lambda i, j: (0, j)),
    ]
    args = [x, g.reshape(1, d), mods, mods, w]
    if mode == "swiglu":
        in_specs.append(pl.BlockSpec((d, bn), lambda i, j: (0, j)))
        args.append(w2)
    return pl.pallas_call(
        functools.partial(_proj_kernel, mode=mode),
        grid=(nt, n // bn),
        in_specs=in_specs,
        out_specs=pl.BlockSpec((TM, bn), lambda i, j: (i, j)),
        out_shape=jax.ShapeDtypeStruct((n_rows, n), out_dtype),
        scratch_shapes=[pltpu.VMEM((TM, d), BF16)],
        compiler_params=_cparams(("arbitrary", "arbitrary")),
        name="proj_" + mode,
    )(*args)


def _outproj_kernel(*refs, n_a):
    a_refs = refs[:n_a]
    w_ref, x_ref, gate_ref, o_ref = refs[n_a:]
    if n_a == 1:
        a = a_refs[0][...]
    else:
        a = (a_refs[0][...].astype(F32) + a_refs[1][...].astype(F32)).astype(BF16)
    y = jnp.dot(a, w_ref[...], preferred_element_type=F32)
    o_ref[...] = x_ref[...] + gate_ref[...] * y


def _outproj(a_list, a_specs, w, x, mods, layer, k_gate, *, n_rows, geom):
    k, d = w.shape
    nt = n_rows // TM
    tpb, n_batch = geom
    gate_spec = pl.BlockSpec((None, None, 1, d),
                             lambda i: (layer * 6 + k_gate, jnp.minimum(i // tpb, n_batch), 0, 0))
    return pl.pallas_call(
        functools.partial(_outproj_kernel, n_a=len(a_list)),
        grid=(nt,),
        in_specs=list(a_specs) + [
            pl.BlockSpec((k, d), lambda i: (0, 0)),
            pl.BlockSpec((TM, d), lambda i: (i, 0)),
            gate_spec,
        ],
        out_specs=pl.BlockSpec((TM, d), lambda i: (i, 0)),
        out_shape=jax.ShapeDtypeStruct(x.shape, F32),
        input_output_aliases={len(a_list) + 1: 0},
        compiler_params=_cparams(("arbitrary",)),
        name="outproj",
    )(*a_list, w, x, mods)


def _ret_kernel(lg_ref, qf_ref, kf_ref, vf_ref, gf_ref, qb_ref, kb_ref, vb_ref, gb_ref, gnf_ref, gnb_ref,
                of_ref, ob_ref, s_scr, intra_scr, dec_scr):
    hp = pl.program_id(1)
    s = pl.program_id(2)
    c = RET_CHUNK
    nh = RET_HEADS_PER_STEP
    chains = [(d, j) for d in range(2) for j in range(nh)]

    @pl.when(s == 0)
    def _():
        s_scr[...] = jnp.zeros_like(s_scr)
        ii = lax.broadcasted_iota(jnp.int32, (c, c), 0)
        jj = lax.broadcasted_iota(jnp.int32, (c, c), 1)
        pos = lax.broadcasted_iota(jnp.int32, (c, 1), 0).astype(F32)
        for n, (d, j) in enumerate(chains):
            lg = lg_ref[d, hp * nh + j]
            diff = ((ii - jj) if d == 0 else (jj - ii)).astype(F32)
            intra_scr[n] = jnp.where(diff >= 0, jnp.exp(lg * jnp.maximum(diff, 0.0)), 0.0)
            fpos = pos if d == 0 else (c - 1.0) - pos
            dec_scr[2 * n] = jnp.exp(lg * (fpos + 1.0))
            dec_scr[2 * n + 1] = jnp.exp(lg * ((c - 1.0) - fpos))

    refs = ((qf_ref, kf_ref, vf_ref, gf_ref, gnf_ref, of_ref), (qb_ref, kb_ref, vb_ref, gb_ref, gnb_ref, ob_ref))
    qcols = [slice(j * RET_DK, (j + 1) * RET_DK) for _, j in chains]
    vcols = [slice(j * RET_DV, (j + 1) * RET_DV) for _, j in chains]
    n_chains = range(len(chains))
    q = [refs[d][0][:, qcols[n]] for n, (d, _) in enumerate(chains)]
    k = [refs[d][1][:, qcols[n]] for n, (d, _) in enumerate(chains)]
    v = [refs[d][2][:, vcols[n]] for n, (d, _) in enumerate(chains)]
    contract_last = (((1,), (1,)), ((), ()))
    contract_rows = (((0,), (0,)), ((), ()))
    scores = [lax.dot_general(q[n], k[n], contract_last, preferred_element_type=F32) for n in n_chains]
    masked = [(scores[n] * intra_scr[n]).astype(BF16) for n in n_chains]
    state = [s_scr[n] for n in n_chains]
    o = [jnp.dot(masked[n], v[n], preferred_element_type=F32)
         + dec_scr[2 * n] * jnp.dot(q[n], state[n].astype(BF16), preferred_element_type=F32) for n in n_chains]
    kd = [(k[n].astype(F32) * dec_scr[2 * n + 1]).astype(BF16) for n in n_chains]
    for n, (d, j) in enumerate(chains):
        chunk_decay = jnp.exp(lg_ref[d, hp * nh + j] * float(c))
        s_scr[n] = state[n] * chunk_decay + lax.dot_general(kd[n], v[n], contract_rows,
                                                           preferred_element_type=F32)
    for n, (d, _) in enumerate(chains):
        gate_ref, gn_ref, o_ref = refs[d][3:]
        mu = jnp.mean(o[n], axis=-1, keepdims=True)
        oc = o[n] - mu
        var = jnp.mean(oc * oc, axis=-1, keepdims=True)
        y = oc * lax.rsqrt(var + EPS) * gn_ref[:, vcols[n]]
        o_ref[:, vcols[n]] = (gate_ref[:, vcols[n]].astype(F32) * y).astype(o_ref.dtype)


def _retention(qk, v, gates, lg, gn, n_batch, n_lat, n_ctx):
    m = qk.shape[0]
    heads = v.shape[1] // RET_DV
    c = RET_CHUNK
    lat_chunks = n_lat // c
    assert n_ctx == c
    steps = 1 + lat_chunks
    ctx_blk0 = (n_batch * n_lat) // c

    def row_blk(d, b, s):
        t = (s - 1) if d == 0 else (lat_chunks - s)
        return jnp.where(s == 0, ctx_blk0 + b, b * lat_chunks + t)

    nh = RET_HEADS_PER_STEP
    groups = heads // nh
    qw, vw = nh * RET_DK, nh * RET_DV

    def specs(d):
        return [
            pl.BlockSpec((c, qw), lambda b, h, s: (row_blk(d, b, s), h)),
            pl.BlockSpec((c, qw), lambda b, h, s: (row_blk(d, b, s), groups + h)),
            pl.BlockSpec((c, vw), lambda b, h, s: (row_blk(d, b, s), h)),
            pl.BlockSpec((c, vw), lambda b, h, s: (row_blk(d, b, s), groups * d + h)),
        ]

    def gn_spec(d):
        return pl.BlockSpec((None, 1, vw), lambda b, h, s: (d, 0, h))

    shape = jax.ShapeDtypeStruct((m, heads * RET_DV), BF16)
    yf, yb = pl.pallas_call(
        _ret_kernel,
        grid=(n_batch, groups, steps),
        in_specs=[pl.BlockSpec(memory_space=pltpu.SMEM)] + specs(0) + specs(1) + [gn_spec(0), gn_spec(1)],
        out_specs=[pl.BlockSpec((c, vw), lambda b, h, s: (row_blk(0, b, s), h)),
                   pl.BlockSpec((c, vw), lambda b, h, s: (row_blk(1, b, s), h))],
        out_shape=[shape, shape],
        scratch_shapes=[pltpu.VMEM((2 * nh, RET_DK, RET_DV), F32), pltpu.VMEM((2 * nh, c, c), F32),
                        pltpu.VMEM((4 * nh, c, 1), F32)],
        compiler_params=_cparams(("arbitrary",) * 3),
        name="retention",
    )(lg, qk, qk, v, gates, qk, qk, v, gates, gn, gn)
    return yf, yb


def _mla_up_kernel(dn_ref, gq_ref, gkv_ref, wq_ref, wk_ref, wv_ref, cos_ref, sin_ref,
                   q_ref, k_ref, v_ref, *, heads, scale):
    dn = dn_ref[...]
    lq = gq_ref.shape[1]
    lkv = gkv_ref.shape[1]

    def rms(a, g):
        return (a * lax.rsqrt(jnp.mean(a * a, axis=-1, keepdims=True) + EPS) * g).astype(BF16)

    cq = rms(dn[:, :lq], gq_ref[...])
    ckv = rms(dn[:, lq:lq + lkv], gkv_ref[...])
    cos = cos_ref[...]
    sin = sin_ref[...]
    half = MLA_DR // 4
    q = jnp.dot(cq, wq_ref[...], preferred_element_type=F32)
    kn = jnp.dot(ckv, wk_ref[...], preferred_element_type=F32)
    v_ref[...] = lax.dot_general(wv_ref[...], ckv, (((1,), (1,)), ((), ())),
                                 preferred_element_type=F32).astype(v_ref.dtype)
    kr = _rope(dn[:, lq + lkv:], cos, sin, half).astype(k_ref.dtype)
    for h in range(heads):
        o = h * 2 * MLA_DN
        q_ref[:, o:o + MLA_DN] = (q[:, o:o + MLA_DN] * scale).astype(q_ref.dtype)
        qr = _rope(q[:, o + MLA_DN:o + 2 * MLA_DN], cos, sin, half)
        q_ref[:, o + MLA_DN:o + 2 * MLA_DN] = (qr * scale).astype(q_ref.dtype)
        k_ref[:, o:o + MLA_DN] = kn[:, h * MLA_DN:(h + 1) * MLA_DN].astype(k_ref.dtype)
        k_ref[:, o + MLA_DN:o + 2 * MLA_DN] = kr


def _attn_lat_kernel(q_ref, kl_ref, vtl_ref, kc_ref, vtc_ref, o_ref):
    q = q_ref[...]
    tq = q.shape[0]
    dv = vtl_ref.shape[0]
    n_chunks = kl_ref.shape[0] // ATTN_TK

    def scores(k):
        return lax.dot_general(k, q, (((1,), (1,)), ((), ())), preferred_element_type=F32)

    def weighted(vt, p):
        lhs = jnp.concatenate([vt, jnp.ones((ATTN_ONES_ROWS, vt.shape[1]), BF16)], axis=0)
        return jnp.dot(lhs, p, preferred_element_type=F32)

    def exact_max_block(s, vt, m, acc, risk):
        m_new = jnp.maximum(m, jnp.max(s, axis=0, keepdims=True))
        p = jnp.exp2(s - m_new).astype(BF16)
        return m_new, jnp.exp2(m - m_new) * acc + weighted(vt, p), risk

    def lagged_max_block(s, vt, m, acc, risk):
        p = jnp.exp2(s - m).astype(BF16)
        bmax = jnp.max(s, axis=0, keepdims=True)
        m_new = jnp.maximum(m, bmax)
        return m_new, (acc + weighted(vt, p)) * jnp.exp2(m - m_new), jnp.maximum(risk, bmax - m)

    def run(later_block):
        state = (jnp.full((1, tq), -jnp.inf, F32), jnp.zeros((dv + ATTN_ONES_ROWS, tq), F32),
                 jnp.zeros((1, tq), F32))
        s_cur = scores(kc_ref[...])
        vt_cur = vtc_ref[...]
        for c in range(n_chunks):
            r = slice(c * ATTN_TK, (c + 1) * ATTN_TK)
            s_next = scores(kl_ref[r, :])
            state = (exact_max_block if c == 0 else later_block)(s_cur, vt_cur, *state)
            s_cur, vt_cur = s_next, vtl_ref[:, r]
        _, acc, risk = later_block(s_cur, vt_cur, *state)
        o_t = acc[:dv] / acc[dv:dv + 1]
        o_ref[...] = o_t.T.astype(o_ref.dtype)
        return risk

    risk = run(lagged_max_block)

    @pl.when(jnp.max(risk) > ATTN_MAX_LAG_LOG2)
    def _():
        run(exact_max_block)


def _attn_ctx_kernel(q_ref, kc_ref, vtc_ref, o_ref):
    s = lax.dot_general(q_ref[...], kc_ref[...], (((1,), (1,)), ((), ())), preferred_element_type=F32)
    p = jnp.exp2(s - jnp.max(s, axis=-1, keepdims=True))
    l = jnp.sum(p, axis=-1, keepdims=True)
    o = lax.dot_general(p.astype(BF16), vtc_ref[...], (((1,), (1,)), ((), ())), preferred_element_type=F32)
    o_ref[...] = (o / l).astype(o_ref.dtype)


def _mla_attention(q, k, vt, n_batch, n_lat, n_ctx, heads):
    m = q.shape[0]
    hd = 2 * MLA_DN
    nq = n_lat // ATTN_TQ
    ctx_blk0 = (n_batch * n_lat) // n_ctx
    kl = pl.BlockSpec((n_lat, hd), lambda b, h, t: (b, h))
    vl = pl.BlockSpec((MLA_DV, n_lat), lambda b, h, t: (h, b))
    kc = pl.BlockSpec((n_ctx, hd), lambda b, h, t: (ctx_blk0 + b, h))
    vc = pl.BlockSpec((MLA_DV, n_ctx), lambda b, h, t: (h, ctx_blk0 + b))
    out = pl.pallas_call(
        _attn_lat_kernel,
        grid=(n_batch, heads, nq),
        in_specs=[pl.BlockSpec((ATTN_TQ, hd), lambda b, h, t: (b * nq + t, h)), kl, vl, kc, vc],
        out_specs=pl.BlockSpec((ATTN_TQ, MLA_DV), lambda b, h, t: (b * nq + t, h)),
        out_shape=jax.ShapeDtypeStruct((n_batch * n_lat, heads * MLA_DV), BF16),
        compiler_params=_cparams(("arbitrary",) * 3),
        name="mla_attn_latent",
    )(q, k, vt, k, vt)
    out_ctx = pl.pallas_call(
        _attn_ctx_kernel,
        grid=(n_batch, heads),
        in_specs=[
            pl.BlockSpec((n_ctx, hd), lambda b, h: (ctx_blk0 + b, h)),
            pl.BlockSpec((n_ctx, hd), lambda b, h: (ctx_blk0 + b, h)),
            pl.BlockSpec((MLA_DV, n_ctx), lambda b, h: (h, ctx_blk0 + b)),
        ],
        out_specs=pl.BlockSpec((n_ctx, MLA_DV), lambda b, h: (b, h)),
        out_shape=jax.ShapeDtypeStruct((n_batch * n_ctx, heads * MLA_DV), BF16),
        compiler_params=_cparams(("arbitrary",) * 2),
        name="mla_attn_ctx",
    )(q, k, vt)
    return jnp.concatenate([out, out_ctx], axis=0)


def _sgu_kernel(u_ref, v_ref, gv_ref, ws_ref, bs_ref, wo_ref, x_ref, gate_ref, o_ref, z_scr):
    sd = v_ref.shape[1]
    gw = sd // SGU_GROUPS
    for ch in range(v_ref.shape[0] // SGU_CHUNK):
        rows = slice(ch * SGU_CHUNK, (ch + 1) * SGU_CHUNK)
        v = v_ref[rows, :].astype(F32)
        vn = (v * lax.rsqrt(jnp.mean(v * v, axis=-1, keepdims=True) + EPS) * gv_ref[...]).astype(BF16)
        for g in range(SGU_GROUPS):
            cols = slice(g * gw, (g + 1) * gw)
            z = jnp.dot(ws_ref[g], vn[:, cols], preferred_element_type=F32) + bs_ref[g]
            z_scr[rows, cols] = (u_ref[rows, cols].astype(F32) * z).astype(BF16)
    y = jnp.dot(z_scr[...], wo_ref[...], preferred_element_type=F32)
    o_ref[...] = x_ref[...] + gate_ref[...] * y


def _sgu(uv, g_v, w_s, b_s, w_out, x, mods, layer, geom):
    m, d = x.shape
    sd = w_out.shape[0]
    tpb, n_batch = geom
    tms = TM // 2
    scale = TM // tms
    gate_spec = pl.BlockSpec((None, None, 1, d),
                             lambda i: (layer * 6 + 2, jnp.minimum(i // (tpb * scale), n_batch), 0, 0))
    return pl.pallas_call(
        _sgu_kernel,
        grid=(m // tms,),
        in_specs=[
            pl.BlockSpec((tms, sd), lambda i: (i, 0)),
            pl.BlockSpec((tms, sd), lambda i: (i, 1)),
            pl.BlockSpec((1, sd), lambda i: (0, 0)),
            pl.BlockSpec((SGU_GROUPS, SGU_CHUNK, SGU_CHUNK), lambda i: (0, 0, 0)),
            pl.BlockSpec((SGU_GROUPS, SGU_CHUNK, 1), lambda i: (0, 0, 0)),
            pl.BlockSpec((sd, d), lambda i: (0, 0)),
            pl.BlockSpec((tms, d), lambda i: (i, 0)),
            gate_spec,
        ],
        out_specs=pl.BlockSpec((tms, d), lambda i: (i, 0)),
        out_shape=jax.ShapeDtypeStruct(x.shape, F32),
        scratch_shapes=[pltpu.VMEM((tms, sd), BF16)],
        input_output_aliases={6: 0},
        compiler_params=_cparams(("arbitrary",)),
        name="sgu",
    )(uv, uv, g_v.reshape(1, sd), w_s, b_s.reshape(SGU_GROUPS, SGU_CHUNK, 1), w_out, x, mods)


def _router_kernel(x_ref, g_ref, sh_ref, sc_ref, wr_ref, h_ref, idx_ref, wgt_ref, *, n_tiles):
    i = pl.program_id(0)

    @pl.when(i < n_tiles)
    def _():
        h = _modnorm(x_ref, g_ref, sh_ref, sc_ref)
        h_ref[...] = h.astype(h_ref.dtype)
        logits = jnp.dot(h, wr_ref[...], precision=HIGHEST, preferred_element_type=F32)
        lane = lax.broadcasted_iota(jnp.int32, logits.shape, 1).astype(F32)
        neg = jnp.float32(-jnp.inf)
        big = jnp.float32(logits.shape[1])
        logits = jnp.where(lane < N_EXPERTS, logits, neg)
        m1 = jnp.max(logits, axis=-1, keepdims=True)
        i1 = jnp.min(jnp.where(logits == m1, lane, big), axis=-1, keepdims=True)
        rest = jnp.where(lane == i1, neg, logits)
        m2 = jnp.max(rest, axis=-1, keepdims=True)
        i2 = jnp.min(jnp.where(rest == m2, lane, big), axis=-1, keepdims=True)
        e2 = jnp.exp(m2 - m1)
        w1 = 1.0 / (1.0 + e2)
        w2 = e2 / (1.0 + e2)
        idx_ref[...] = jnp.where(lane == 0, i1, jnp.where(lane == 1, i2, 0.0)).astype(jnp.int32)
        wgt_ref[...] = jnp.where(lane == 0, w1, jnp.where(lane == 1, w2, 0.0))

    @pl.when(i >= n_tiles)
    def _():
        h_ref[...] = jnp.zeros_like(h_ref)


def _router(x, g, mods, layer, w_router_pad, *, n_rows, h_rows, geom):
    d = x.shape[1]
    tpb, n_batch = geom
    sh_spec, sc_spec = _mod_specs(layer, 3, tpb, n_batch)
    nt = n_rows // TM
    real = lambda i: jnp.minimum(i, nt - 1)
    one = lambda f: (lambda i: f(real(i), 0))
    lanes = w_router_pad.shape[1]
    return pl.pallas_call(
        functools.partial(_router_kernel, n_tiles=nt),
        grid=(h_rows // TM,),
        in_specs=[
            pl.BlockSpec((TM, d), lambda i: (real(i), 0)),
            pl.BlockSpec((1, d), lambda i: (0, 0)),
            pl.BlockSpec(sh_spec.block_shape, one(sh_spec.index_map)),
            pl.BlockSpec(sc_spec.block_shape, one(sc_spec.index_map)),
            pl.BlockSpec((d, lanes), lambda i: (0, 0)),
        ],
        out_specs=[
            pl.BlockSpec((TM, d), lambda i: (i, 0)),
            pl.BlockSpec((TM, lanes), lambda i: (real(i), 0)),
            pl.BlockSpec((TM, lanes), lambda i: (real(i), 0)),
        ],
        out_shape=[
            jax.ShapeDtypeStruct((h_rows, d), BF16),
            jax.ShapeDtypeStruct((n_rows, lanes), jnp.int32),
            jax.ShapeDtypeStruct((n_rows, lanes), F32),
        ],
        compiler_params=_cparams(("arbitrary",)),
        name="moe_router",
    )(x, g.reshape(1, d), mods, mods, w_router_pad)


def _moe_ffn_kernel(te_ref, tr_ref, x_ref, wg32_ref, wu32_ref, wd32_ref, o_ref, acc_ref,
                    wg_ref, wu_ref, wd_ref):
    del te_ref
    t = pl.program_id(0)
    f = pl.program_id(1)
    n_rows = tr_ref[t]
    half_rows = MOE_TM // 2

    @pl.when(f == 0)
    def _():
        acc_ref[...] = jnp.zeros_like(acc_ref)

    @pl.when(n_rows > 0)
    def _():
        wg_ref[...] = wg32_ref[...].astype(BF16)
        wu_ref[...] = wu32_ref[...].astype(BF16)
        wd_ref[...] = wd32_ref[...].astype(BF16)

    def matmul(c):
        x = x_ref[_chunk(c), :]
        return (jnp.dot(x, wg_ref[...], preferred_element_type=F32),
                jnp.dot(x, wu_ref[...], preferred_element_type=F32))

    def epilogue(c, gu):
        a = (_silu(gu[0]) * gu[1]).astype(BF16)
        acc_ref[_chunk(c), :] += jnp.dot(a, wd_ref[...], preferred_element_type=F32)

    @pl.when(n_rows > half_rows)
    def _():
        _pipelined_chunks(MOE_TM // ROW_CHUNK, matmul, epilogue)

    @pl.when((n_rows > 0) & (n_rows <= half_rows))
    def _():
        _pipelined_chunks(half_rows // ROW_CHUNK, matmul, epilogue)

    @pl.when(f == pl.num_programs(1) - 1)
    def _():
        o_ref[...] = acc_ref[...].astype(o_ref.dtype)


def _moe_ffn(xs, tile_expert, tile_rows, w_gu, w_down, lf):
    r, d = xs.shape
    ff = w_down.shape[2]
    nf = ff // MOE_BF
    n_tiles = r // MOE_TM

    def fidx(t, f, tr):
        return jnp.where(tr[t] > 0, f, nf - 1)

    grid_spec = pltpu.PrefetchScalarGridSpec(
        num_scalar_prefetch=2,
        grid=(n_tiles, nf),
        in_specs=[
            pl.BlockSpec((MOE_TM, d), lambda t, f, te, tr: (t, 0)),
            pl.BlockSpec((None, None, d, MOE_BF), lambda t, f, te, tr: (lf, te[t], 0, fidx(t, f, tr))),
            pl.BlockSpec((None, None, d, MOE_BF), lambda t, f, te, tr: (lf, te[t], 0, nf + fidx(t, f, tr))),
            pl.BlockSpec((None, None, MOE_BF, d), lambda t, f, te, tr: (lf, te[t], fidx(t, f, tr), 0)),
        ],
        out_specs=pl.BlockSpec((MOE_TM, d), lambda t, f, te, tr: (t, 0)),
        scratch_shapes=[pltpu.VMEM((MOE_TM, d), F32), pltpu.VMEM((d, MOE_BF), BF16),
                        pltpu.VMEM((d, MOE_BF), BF16), pltpu.VMEM((MOE_BF, d), BF16)],
    )
    return pl.pallas_call(
        _moe_ffn_kernel,
        grid_spec=grid_spec,
        out_shape=jax.ShapeDtypeStruct((r, d), BF16),
        compiler_params=_cparams(("arbitrary", "arbitrary")),
        name="moe_ffn",
    )(tile_expert, tile_rows, xs, w_gu, w_gu, w_down)


def _combine_kernel(x_ref, y0_ref, y1_ref, wgt_ref, gate_ref, gf_ref, o_ref, *, final):
    w = wgt_ref[...]
    y = w[:, 0:1] * y0_ref[...].astype(F32) + w[:, 1:2] * y1_ref[...].astype(F32)
    xn = x_ref[...] + gate_ref[...] * y
    if final:
        xn = xn * lax.rsqrt(jnp.mean(xn * xn, axis=-1, keepdims=True) + EPS) * gf_ref[...]
    o_ref[...] = xn


def _combine(x, y0, y1, wgt, mods, layer, g_final, *, n_rows, geom, final):
    d = x.shape[1]
    tpb, n_batch = geom
    lanes = wgt.shape[1]
    gate_spec = pl.BlockSpec((None, None, 1, d),
                             lambda i: (layer * 6 + 5, jnp.minimum(i // tpb, n_batch), 0, 0))
    row = lambda w: pl.BlockSpec((TM, w), lambda i: (i, 0))
    out_rows = n_rows if final else x.shape[0]
    return pl.pallas_call(
        functools.partial(_combine_kernel, final=final),
        grid=(n_rows // TM,),
        in_specs=[row(d), row(d), row(d), row(lanes), gate_spec, pl.BlockSpec((1, d), lambda i: (0, 0))],
        out_specs=row(d),
        out_shape=jax.ShapeDtypeStruct((out_rows, d), F32),
        input_output_aliases={} if final else {0: 0},
        compiler_params=_cparams(("arbitrary",)),
        name="moe_combine",
    )(x, y0, y1, wgt, mods, g_final.reshape(1, d))


def _moe_layer(x, g, mods, layer, w_router, w_gu, w_down, lf, g_final, *, n_rows, geom, final):
    d = x.shape[1]
    wr_pad = jnp.zeros((d, 128), F32).at[:, :N_EXPERTS].set(w_router)
    n_assign = 2 * n_rows
    r_rows = n_assign + (N_EXPERTS - 1) * MOE_TM
    h, idx, wgt = _router(x, g, mods, layer, wr_pad, n_rows=n_rows, h_rows=r_rows, geom=geom)
    e_flat = idx[:, :2].reshape(-1)
    onehot = (e_flat[:, None] == jnp.arange(N_EXPERTS, dtype=jnp.int32)[None, :]).astype(jnp.int32)
    csum = jnp.cumsum(onehot, axis=0)
    rank = jnp.sum(onehot * csum, axis=1) - 1
    counts = csum[-1]
    padded = ((counts + MOE_TM - 1) // MOE_TM) * MOE_TM
    ends = jnp.cumsum(padded)
    starts = ends - padded
    dest = starts[e_flat] + rank
    n_tiles = r_rows // MOE_TM
    src = (jnp.arange(r_rows, dtype=jnp.int32) % n_rows).at[dest].set(
        jnp.arange(n_assign, dtype=jnp.int32) // 2)
    tile_start = jnp.arange(n_tiles, dtype=jnp.int32) * MOE_TM
    last_tile = jnp.maximum(ends[-1] // MOE_TM - 1, 0)
    ts_used = jnp.minimum(tile_start, last_tile * MOE_TM)
    tile_expert = jnp.sum((ends[None, :] <= ts_used[:, None]).astype(jnp.int32), axis=1)
    tile_expert = jnp.minimum(tile_expert, N_EXPERTS - 1)
    real_end = (starts + counts)[tile_expert]
    tile_rows = jnp.where(tile_start < ends[-1], jnp.clip(real_end - tile_start, 0, MOE_TM), 0)
    xs = jnp.take(h, src, axis=0, mode="clip")
    ys = _moe_ffn(xs, tile_expert, tile_rows.astype(jnp.int32), w_gu, w_down, lf)
    d2 = dest.reshape(n_rows, 2)
    y0 = jnp.take(ys, d2[:, 0], axis=0, mode="clip")
    y1 = jnp.take(ys, d2[:, 1], axis=0, mode="clip")
    return _combine(x, y0, y1, wgt, mods, layer, g_final, n_rows=n_rows, geom=geom, final=final)


def kernel(x, c, ctx, c_ctx, ada_w, ada_b, norm_mix_g, norm_ffn_g, final_norm_g, ret_w_in, ret_w_o, ret_decay_f, ret_decay_b, ret_gn_f, ret_gn_b, mla_w_down, mla_g_q, mla_g_kv, mla_w_uq, mla_w_ukv, mla_w_o, sgu_w_in, sgu_g_v, sgu_w_s, sgu_b_s, sgu_w_out, ffn_w_gu, ffn_w_down, moe_w_router, moe_w_gu, moe_w_down):
    n_batch, n_lat, d = x.shape
    n_ctx = ctx.shape[1]
    depth = ada_w.shape[0]
    rows_lat = n_batch * n_lat
    m = rows_lat + n_batch * n_ctx
    geom = (n_lat // TM, n_batch)
    assert n_batch * n_ctx == TM and d == 1024 and n_batch + 1 <= 8

    xt = jnp.concatenate([x.reshape(rows_lat, d), ctx.reshape(n_batch * n_ctx, d)], axis=0)
    cond = jnp.concatenate([c, c_ctx[None, :], jnp.zeros((8 - n_batch - 1, d), F32)], axis=0)
    mods = _adaln(cond, ada_w, ada_b)
    mods = mods.reshape(depth, 8, 6, d).transpose(0, 2, 1, 3).reshape(depth * 6, 8, 1, d)

    grid_rows = n_lat // GRID_W
    ret_cos, ret_sin = _rope_tables(grid_rows, n_batch * n_ctx, RET_DK)
    mla_cos, mla_sin = _rope_tables(grid_rows, n_batch * n_ctx, MLA_DR)
    pad_r = MLA_DN - MLA_DR
    mla_cos = jnp.concatenate([mla_cos, jnp.ones((mla_cos.shape[0], pad_r), F32)], axis=1)
    mla_sin = jnp.concatenate([mla_sin, jnp.zeros((mla_sin.shape[0], pad_r), F32)], axis=1)

    for i in range(depth):
        last = i == depth - 1
        kind, jm = i % N_MIXERS, i // N_MIXERS
        rows_out = rows_lat if last else m
        if kind == 0:
            qk, vv, gates = _ret_in(xt, norm_mix_g[i], mods, i, ret_w_in[jm].astype(BF16),
                                    (ret_cos, ret_sin), geom=geom)
            lg = jnp.stack([jax.nn.log_sigmoid(ret_decay_f[jm].astype(F32)),
                            jax.nn.log_sigmoid(ret_decay_b[jm].astype(F32))])
            hv = vv.shape[1]
            gn = jnp.stack([ret_gn_f[jm], ret_gn_b[jm]]).reshape(2, 1, hv)
            yf, yb = _retention(qk, vv, gates, lg, gn, n_batch, n_lat, n_ctx)
            a_specs = [pl.BlockSpec((TM, hv), lambda r: (r, 0)), pl.BlockSpec((TM, hv), lambda r: (r, 0))]
            xt = _outproj([yf, yb], a_specs, ret_w_o[jm].astype(BF16), xt, mods, i, 2,
                          n_rows=rows_out, geom=geom)
        elif kind == 1:
            heads = mla_w_o.shape[1] // MLA_DV
            lq, lkv = mla_g_q.shape[1], mla_g_kv.shape[1]
            n_dn = lq + lkv + MLA_DN
            wd = jnp.zeros((d, n_dn), F32).at[:, :lq + lkv + MLA_DR].set(mla_w_down[jm]).astype(BF16)
            dn = _proj(xt, norm_mix_g[i], mods, i, 0, wd, mode="f32", bn=n_dn, out_dtype=F32,
                       n_rows=m, geom=geom)
            wq = mla_w_uq[jm].reshape(lq, heads, MLA_DN + MLA_DR)
            wq = jnp.concatenate([wq, jnp.zeros((lq, heads, pad_r), F32)], axis=2)
            wq = wq.reshape(lq, heads * 2 * MLA_DN).astype(BF16)
            wkv = mla_w_ukv[jm].reshape(lkv, heads, MLA_DN + MLA_DV)
            wk = wkv[:, :, :MLA_DN].reshape(lkv, heads * MLA_DN).astype(BF16)
            wv = wkv[:, :, MLA_DN:].reshape(lkv, heads * MLA_DV).T.astype(BF16)
            tpb = geom[0]
            n_lat_tiles = tpb * n_batch
            pos_idx = lambda r: (jnp.where(r < n_lat_tiles, r % tpb, tpb + (r - n_lat_tiles)), 0)
            full = lambda a: pl.BlockSpec(a.shape, lambda r: (0,) * a.ndim)
            gq = mla_g_q[jm].reshape(1, lq)
            gkv = mla_g_kv[jm].reshape(1, lkv)
            scale = float((MLA_DN + MLA_DR) ** -0.5 * LOG2_E)
            q, k, vt = pl.pallas_call(
                functools.partial(_mla_up_kernel, heads=heads, scale=scale),
                grid=(m // TM,),
                in_specs=[pl.BlockSpec((TM, n_dn), lambda r: (r, 0)), full(gq), full(gkv), full(wq),
                          full(wk), full(wv), pl.BlockSpec((TM, MLA_DN), pos_idx),
                          pl.BlockSpec((TM, MLA_DN), pos_idx)],
                out_specs=[pl.BlockSpec((TM, heads * 2 * MLA_DN), lambda r: (r, 0)),
                           pl.BlockSpec((TM, heads * 2 * MLA_DN), lambda r: (r, 0)),
                           pl.BlockSpec((heads * MLA_DV, TM), lambda r: (0, r))],
                out_shape=[jax.ShapeDtypeStruct((m, heads * 2 * MLA_DN), BF16),
                           jax.ShapeDtypeStruct((m, heads * 2 * MLA_DN), BF16),
                           jax.ShapeDtypeStruct((heads * MLA_DV, m), BF16)],
                compiler_params=_cparams(("arbitrary",)),
                name="mla_up",
            )(dn, gq, gkv, wq, wk, wv, mla_cos, mla_sin)
            ao = _mla_attention(q, k, vt, n_batch, n_lat, n_ctx, heads)
            a_specs = [pl.BlockSpec((TM, heads * MLA_DV), lambda r: (r, 0))]
            xt = _outproj([ao], a_specs, mla_w_o[jm].astype(BF16), xt, mods, i, 2,
                          n_rows=rows_out, geom=geom)
        else:
            uv = _proj(xt, norm_mix_g[i], mods, i, 0, sgu_w_in[jm].astype(BF16), mode="gelu", bn=2048,
                       out_dtype=BF16, n_rows=m, geom=geom)
            xt = _sgu(uv, sgu_g_v[jm], sgu_w_s[jm].astype(BF16), sgu_b_s[jm], sgu_w_out[jm].astype(BF16),
                      xt, mods, i, geom)
        f = i // 2
        if i % 2 == 0:
            dff = ffn_w_down.shape[1]
            wgu = ffn_w_gu[f].astype(BF16)
            a = _proj(xt, norm_ffn_g[i], mods, i, 3, wgu[:, :dff], mode="swiglu", bn=dff // 2,
                      out_dtype=BF16, n_rows=rows_out, geom=geom, w2=wgu[:, dff:])
            a_specs = [pl.BlockSpec((TM, dff), lambda r: (r, 0))]
            xt = _outproj([a], a_specs, ffn_w_down[f].astype(BF16), xt, mods, i, 5,
                          n_rows=rows_out, geom=geom)
        else:
            xt = _moe_layer(xt, norm_ffn_g[i], mods, i, moe_w_router[f], moe_w_gu, moe_w_down, f,
                            final_norm_g, n_rows=rows_out, geom=geom, final=last)
    if not (depth - 1) % 2:
        raise NotImplementedError("final norm is fused into the expert combine of the last layer")
    return xt.reshape(n_batch, n_lat, d)
```

```python
import functools

import jax
import jax.numpy as jnp
from jax import lax
from jax.experimental import pallas as pl
from jax.experimental.pallas import tpu as pltpu

F32 = jnp.float32
BF16 = jnp.bfloat16
HIGHEST = lax.Precision.HIGHEST
LOG2_E = 1.4426950408889634

GRID_W = 64
ROPE_BASE = 10000.0
EPS = 1e-6
RET_DK = 256
RET_DV = 512
MLA_DN = 128
MLA_DR = 64
MLA_DV = 128
SGU_GROUPS = 8
SGU_CHUNK = 128
N_EXPERTS = 8
N_MIXERS = 3

TM = 1024
ROW_CHUNK = 256
RET_CHUNK = 256
RET_HEADS_PER_STEP = 2
RET_IN_STEPS = 4
ATTN_TQ = 512
ATTN_TK = 1024
ATTN_ONES_ROWS = 16
ATTN_MAX_LAG_LOG2 = 60.0
MOE_TM = 1024
MOE_BF = 512
CUMSUM_BLOCK = 256
VMEM_LIMIT = 56 * 1024 * 1024


def _cparams(sem):
    return pltpu.CompilerParams(dimension_semantics=sem, vmem_limit_bytes=VMEM_LIMIT)


def _silu(x):
    return x * jax.nn.sigmoid(x)


def _rope(x, cos, sin_signed, half):
    w = x.shape[-1]
    lane = lax.broadcasted_iota(jnp.int32, x.shape, 1)
    first = (lane % (2 * half)) < half
    rot = jnp.where(first, pltpu.roll(x, w - half, 1), pltpu.roll(x, half, 1))
    return x * cos + rot * sin_signed


def _rope_tables(rows, n_ctx_rows, rot_dim):
    row = jnp.repeat(jnp.arange(rows, dtype=F32), GRID_W)
    col = jnp.tile(jnp.arange(GRID_W, dtype=F32), rows)
    axis_dim = rot_dim // 2
    inv_freq = ROPE_BASE ** (-jnp.arange(0, axis_dim, 2, dtype=F32) / axis_dim)
    ang_r = row[:, None] * inv_freq
    ang_c = col[:, None] * inv_freq
    ang = jnp.concatenate([ang_r, ang_r, ang_c, ang_c], axis=-1)
    quarter = rot_dim // 4
    sign = jnp.where((jnp.arange(rot_dim) % (2 * quarter)) < quarter, -1.0, 1.0).astype(F32)
    cos = jnp.concatenate([jnp.cos(ang), jnp.ones((n_ctx_rows, rot_dim), F32)], axis=0)
    sin = jnp.concatenate([jnp.sin(ang) * sign, jnp.zeros((n_ctx_rows, rot_dim), F32)], axis=0)
    return cos, sin


def _adaln_kernel(cond_ref, w_ref, b_ref, o_ref):
    s = _silu(cond_ref[...])
    o_ref[...] = jnp.dot(s, w_ref[...], precision=HIGHEST, preferred_element_type=F32) + b_ref[...]


def _adaln(cond, ada_w, ada_b):
    depth, d, n6 = ada_w.shape
    bn = n6 // 4
    return pl.pallas_call(
        _adaln_kernel,
        grid=(depth, n6 // bn),
        in_specs=[
            pl.BlockSpec((8, d), lambda l, j: (0, 0)),
            pl.BlockSpec((None, d, bn), lambda l, j: (l, 0, j)),
            pl.BlockSpec((None, 1, bn), lambda l, j: (l, 0, j)),
        ],
        out_specs=pl.BlockSpec((None, 8, bn), lambda l, j: (l, 0, j)),
        out_shape=jax.ShapeDtypeStruct((depth, 8, n6), F32),
        compiler_params=_cparams(("arbitrary", "arbitrary")),
        name="adaln",
    )(cond, ada_w, ada_b.reshape(depth, 1, n6))


def _modnorm(x_ref, g_ref, shift_ref, scale_ref):
    xf = x_ref[...]
    y = xf * lax.rsqrt(jnp.mean(xf * xf, axis=-1, keepdims=True) + EPS)
    return (y * g_ref[...]) * (1.0 + scale_ref[...]) + shift_ref[...]


def _chunk(c):
    return slice(c * ROW_CHUNK, (c + 1) * ROW_CHUNK)


def _pipelined_chunks(n_chunks, matmul, epilogue):
    pending = matmul(0)
    for c in range(n_chunks):
        following = matmul(c + 1) if c + 1 < n_chunks else None
        epilogue(c, pending)
        pending = following


def _ret_in_kernel(x_ref, g_ref, sh_ref, sc_ref, wqk_ref, wv_ref, wg_ref, cos_ref, sin_ref,
                   qk_ref, v_ref, gate_ref, h_scr, *, n_q):
    j = pl.program_id(1)

    @pl.when(j == 0)
    def _():
        h_scr[...] = _modnorm(x_ref, g_ref, sh_ref, sc_ref).astype(BF16)

    kscale = jnp.where(j >= n_q, RET_DK ** -0.5, 1.0).astype(F32)

    def matmul(c):
        h = h_scr[_chunk(c), :]
        return tuple(jnp.dot(h, w[...], preferred_element_type=F32) for w in (wqk_ref, wv_ref, wg_ref))

    def epilogue(c, ys):
        r = _chunk(c)
        for hs in range(ys[0].shape[1] // RET_DK):
            cols = slice(hs * RET_DK, (hs + 1) * RET_DK)
            qk = _rope(ys[0][:, cols], cos_ref[r, :], sin_ref[r, :], RET_DK // 4)
            qk_ref[r, cols] = (qk * kscale).astype(qk_ref.dtype)
        v_ref[r, :] = ys[1].astype(v_ref.dtype)
        gate_ref[r, :] = _silu(ys[2]).astype(gate_ref.dtype)

    _pipelined_chunks(h_scr.shape[0] // ROW_CHUNK, matmul, epilogue)


def _ret_in(x, g, mods, layer, w_in, rope, *, geom):
    m, d = x.shape
    heads = d // RET_DK
    nj = RET_IN_STEPS
    n_q = nj // 2
    qk_cols = (2 * heads * RET_DK) // nj
    wv_cols = (heads * RET_DV) // nj
    wg_cols = (2 * heads * RET_DV) // nj
    assert wv_cols == qk_cols and wg_cols == 2 * qk_cols and qk_cols % RET_DK == 0
    tpb, n_batch = geom
    sh_spec, sc_spec = _mod_specs(layer, 0, tpb, n_batch)
    n_lat_tiles = tpb * n_batch
    cos, sin = rope

    def pos_idx(i, j):
        return (jnp.where(i < n_lat_tiles, i % tpb, tpb + (i - n_lat_tiles)), 0)

    return pl.pallas_call(
        functools.partial(_ret_in_kernel, n_q=n_q),
        grid=(m // TM, nj),
        in_specs=[
            pl.BlockSpec((TM, d), lambda i, j: (i, 0)),
            pl.BlockSpec((1, d), lambda i, j: (0, 0)),
            sh_spec, sc_spec,
            pl.BlockSpec((d, qk_cols), lambda i, j: (0, j)),
            pl.BlockSpec((d, qk_cols), lambda i, j: (0, nj + j)),
            pl.BlockSpec((d, 2 * qk_cols), lambda i, j: (0, nj + j)),
            pl.BlockSpec((TM, RET_DK), pos_idx),
            pl.BlockSpec((TM, RET_DK), pos_idx),
        ],
        out_specs=[
            pl.BlockSpec((TM, qk_cols), lambda i, j: (i, j)),
            pl.BlockSpec((TM, qk_cols), lambda i, j: (i, j)),
            pl.BlockSpec((TM, 2 * qk_cols), lambda i, j: (i, j)),
        ],
        out_shape=[
            jax.ShapeDtypeStruct((m, 2 * heads * RET_DK), BF16),
            jax.ShapeDtypeStruct((m, heads * RET_DV), BF16),
            jax.ShapeDtypeStruct((m, 2 * heads * RET_DV), BF16),
        ],
        scratch_shapes=[pltpu.VMEM((TM, d), BF16)],
        compiler_params=_cparams(("arbitrary", "arbitrary")),
        name="ret_in",
    )(x, g.reshape(1, d), mods, mods, w_in, w_in, w_in, cos, sin)


def _proj_kernel(x_ref, g_ref, sh_ref, sc_ref, w_ref, o_ref, h_scr, *, mode):
    j = pl.program_id(1)

    @pl.when(j == 0)
    def _():
        h_scr[...] = _modnorm(x_ref, g_ref, sh_ref, sc_ref).astype(BF16)

    def matmul(c):
        return jnp.dot(h_scr[_chunk(c), :], w_ref[...], preferred_element_type=F32)

    def epilogue(c, y):
        if mode == "f32":
            o_ref[_chunk(c), :] = y
        elif mode == "gelu":
            o_ref[_chunk(c), :] = jax.nn.gelu(y).astype(o_ref.dtype)

    _pipelined_chunks(h_scr.shape[0] // ROW_CHUNK, matmul, epilogue)


def _mod_specs(layer, k_shift, tiles_per_batch, n_batch):
    def spec(k):
        return pl.BlockSpec((None, None, 1, 1024),
                            lambda i, j: (layer * 6 + k, jnp.minimum(i // tiles_per_batch, n_batch), 0, 0))
    return spec(k_shift), spec(k_shift + 1)


def _proj(x, g, mods, layer, k_shift, w, *, mode, bn, out_dtype, n_rows, geom):
    d = x.shape[1]
    n = w.shape[1]
    nt = n_rows // TM
    tpb, n_batch = geom
    sh_spec, sc_spec = _mod_specs(layer, k_shift, tpb, n_batch)
    in_specs = [
        pl.BlockSpec((TM, d), lambda i, j: (i, 0)),
        pl.BlockSpec((1, d), lambda i, j: (0, 0)),
        sh_spec, sc_spec,
        pl.BlockSpec((d, bn), lambda i, j: (0, j)),
    ]
    args = [x, g.reshape(1, d), mods, mods, w]
    return pl.pallas_call(
        functools.partial(_proj_kernel, mode=mode),
        grid=(nt, n // bn),
        in_specs=in_specs,
        out_specs=pl.BlockSpec((TM, bn), lambda i, j: (i, j)),
        out_shape=jax.ShapeDtypeStruct((n_rows, n), out_dtype),
        scratch_shapes=[pltpu.VMEM((TM, d), BF16)],
        compiler_params=_cparams(("arbitrary", "arbitrary")),
        name="proj_" + mode,
    )(*args)


def _dense_ffn_kernel(x_ref, g_ref, sh_ref, sc_ref, gate_ref, wg_ref, wu_ref, wd_ref, o_ref, h_scr):
    h_scr[...] = _modnorm(x_ref, g_ref, sh_ref, sc_ref).astype(BF16)

    def matmul(c):
        h = h_scr[_chunk(c), :]
        return (jnp.dot(h, wg_ref[...], preferred_element_type=F32),
                jnp.dot(h, wu_ref[...], preferred_element_type=F32))

    def epilogue(c, gu):
        a = (_silu(gu[0]) * gu[1]).astype(BF16)
        y = jnp.dot(a, wd_ref[...], preferred_element_type=F32)
        o_ref[_chunk(c), :] = x_ref[_chunk(c), :] + gate_ref[...] * y

    _pipelined_chunks(h_scr.shape[0] // ROW_CHUNK, matmul, epilogue)


def _dense_ffn(x, g, mods, layer, w_g, w_u, w_d, *, n_rows, geom):
    d = x.shape[1]
    dff = w_d.shape[0]
    tpb, n_batch = geom

    def mod_spec(k):
        return pl.BlockSpec((None, None, 1, d),
                            lambda i: (layer * 6 + k, jnp.minimum(i // tpb, n_batch), 0, 0))

    def resident(shape):
        return pl.BlockSpec(shape, lambda i: (0, 0), pipeline_mode=pl.Buffered(1))

    return pl.pallas_call(
        _dense_ffn_kernel,
        grid=(n_rows // TM,),
        in_specs=[
            pl.BlockSpec((TM, d), lambda i: (i, 0)),
            pl.BlockSpec((1, d), lambda i: (0, 0)),
            mod_spec(3), mod_spec(4), mod_spec(5),
            resident((d, dff)), resident((d, dff)), resident((dff, d)),
        ],
        out_specs=pl.BlockSpec((TM, d), lambda i: (i, 0)),
        out_shape=jax.ShapeDtypeStruct(x.shape, F32),
        scratch_shapes=[pltpu.VMEM((TM, d), BF16)],
        input_output_aliases={0: 0},
        compiler_params=_cparams(("arbitrary",)),
        name="dense_ffn",
    )(x, g.reshape(1, d), mods, mods, mods, w_g, w_u, w_d)


def _outproj_kernel(*refs, n_a):
    a_refs = refs[:n_a]
    w_ref, x_ref, gate_ref, o_ref = refs[n_a:]
    if n_a == 1:
        a = a_refs[0][...]
    else:
        a = (a_refs[0][...].astype(F32) + a_refs[1][...].astype(F32)).astype(BF16)
    y = jnp.dot(a, w_ref[...], preferred_element_type=F32)
    o_ref[...] = x_ref[...] + gate_ref[...] * y


def _outproj(a_list, a_specs, w, x, mods, layer, k_gate, *, n_rows, geom):
    k, d = w.shape
    nt = n_rows // TM
    tpb, n_batch = geom
    gate_spec = pl.BlockSpec((None, None, 1, d),
                             lambda i: (layer * 6 + k_gate, jnp.minimum(i // tpb, n_batch), 0, 0))
    return pl.pallas_call(
        functools.partial(_outproj_kernel, n_a=len(a_list)),
        grid=(nt,),
        in_specs=list(a_specs) + [
            pl.BlockSpec((k, d), lambda i: (0, 0)),
            pl.BlockSpec((TM, d), lambda i: (i, 0)),
            gate_spec,
        ],
        out_specs=pl.BlockSpec((TM, d), lambda i: (i, 0)),
        out_shape=jax.ShapeDtypeStruct(x.shape, F32),
        input_output_aliases={len(a_list) + 1: 0},
        compiler_params=_cparams(("arbitrary",)),
        name="outproj",
    )(*a_list, w, x, mods)


def _ret_kernel(lg_ref, qf_ref, kf_ref, vf_ref, gf_ref, qb_ref, kb_ref, vb_ref, gb_ref, gnf_ref, gnb_ref,
                of_ref, ob_ref, s_scr, intra_scr, dec_scr):
    hp = pl.program_id(1)
    s = pl.program_id(2)
    c = RET_CHUNK
    nh = RET_HEADS_PER_STEP
    chains = [(d, j) for d in range(2) for j in range(nh)]

    @pl.when(s == 0)
    def _():
        s_scr[...] = jnp.zeros_like(s_scr)
        ii = lax.broadcasted_iota(jnp.int32, (c, c), 0)
        jj = lax.broadcasted_iota(jnp.int32, (c, c), 1)
        pos = lax.broadcasted_iota(jnp.int32, (c, 1), 0).astype(F32)
        for n, (d, j) in enumerate(chains):
            lg = lg_ref[d, hp * nh + j]
            diff = ((ii - jj) if d == 0 else (jj - ii)).astype(F32)
            intra_scr[n] = jnp.where(diff >= 0, jnp.exp(lg * jnp.maximum(diff, 0.0)), 0.0)
            fpos = pos if d == 0 else (c - 1.0) - pos
            dec_scr[2 * n] = jnp.exp(lg * (fpos + 1.0))
            dec_scr[2 * n + 1] = jnp.exp(lg * ((c - 1.0) - fpos))

    refs = ((qf_ref, kf_ref, vf_ref, gf_ref, gnf_ref, of_ref), (qb_ref, kb_ref, vb_ref, gb_ref, gnb_ref, ob_ref))
    qcols = [slice(j * RET_DK, (j + 1) * RET_DK) for _, j in chains]
    vcols = [slice(j * RET_DV, (j + 1) * RET_DV) for _, j in chains]
    n_chains = range(len(chains))
    q = [refs[d][0][:, qcols[n]] for n, (d, _) in enumerate(chains)]
    k = [refs[d][1][:, qcols[n]] for n, (d, _) in enumerate(chains)]
    v = [refs[d][2][:, vcols[n]] for n, (d, _) in enumerate(chains)]
    contract_last = (((1,), (1,)), ((), ()))
    contract_rows = (((0,), (0,)), ((), ()))
    scores = [lax.dot_general(q[n], k[n], contract_last, preferred_element_type=F32) for n in n_chains]
    masked = [(scores[n] * intra_scr[n]).astype(BF16) for n in n_chains]
    state = [s_scr[n] for n in n_chains]
    qd = [(q[n].astype(F32) * dec_scr[2 * n]).astype(BF16) for n in n_chains]
    o = [jnp.dot(masked[n], v[n], preferred_element_type=F32)
         + jnp.dot(qd[n], state[n].astype(BF16), preferred_element_type=F32) for n in n_chains]
    kd = [(k[n].astype(F32) * dec_scr[2 * n + 1]).astype(BF16) for n in n_chains]
    for n, (d, j) in enumerate(chains):
        chunk_decay = jnp.exp(lg_ref[d, hp * nh + j] * float(c))
        s_scr[n] = state[n] * chunk_decay + lax.dot_general(kd[n], v[n], contract_rows,
                                                           preferred_element_type=F32)
    for n, (d, _) in enumerate(chains):
        gate_ref, gn_ref, o_ref = refs[d][3:]
        mu = jnp.mean(o[n], axis=-1, keepdims=True)
        oc = o[n] - mu
        var = jnp.mean(oc * oc, axis=-1, keepdims=True)
        y = oc * lax.rsqrt(var + EPS) * gn_ref[:, vcols[n]]
        o_ref[:, vcols[n]] = (gate_ref[:, vcols[n]].astype(F32) * y).astype(o_ref.dtype)


def _retention(qk, v, gates, lg, gn, n_batch, n_lat, n_ctx):
    m = qk.shape[0]
    heads = v.shape[1] // RET_DV
    c = RET_CHUNK
    lat_chunks = n_lat // c
    assert n_ctx == c
    steps = 1 + lat_chunks
    ctx_blk0 = (n_batch * n_lat) // c

    def row_blk(d, b, s):
        t = (s - 1) if d == 0 else (lat_chunks - s)
        return jnp.where(s == 0, ctx_blk0 + b, b * lat_chunks + t)

    nh = RET_HEADS_PER_STEP
    groups = heads // nh
    qw, vw = nh * RET_DK, nh * RET_DV

    def specs(d):
        return [
            pl.BlockSpec((c, qw), lambda b, h, s: (row_blk(d, b, s), h)),
            pl.BlockSpec((c, qw), lambda b, h, s: (row_blk(d, b, s), groups + h)),
            pl.BlockSpec((c, vw), lambda b, h, s: (row_blk(d, b, s), h)),
            pl.BlockSpec((c, vw), lambda b, h, s: (row_blk(d, b, s), groups * d + h)),
        ]

    def gn_spec(d):
        return pl.BlockSpec((None, 1, vw), lambda b, h, s: (d, 0, h))

    shape = jax.ShapeDtypeStruct((m, heads * RET_DV), BF16)
    yf, yb = pl.pallas_call(
        _ret_kernel,
        grid=(n_batch, groups, steps),
        in_specs=[pl.BlockSpec(memory_space=pltpu.SMEM)] + specs(0) + specs(1) + [gn_spec(0), gn_spec(1)],
        out_specs=[pl.BlockSpec((c, vw), lambda b, h, s: (row_blk(0, b, s), h)),
                   pl.BlockSpec((c, vw), lambda b, h, s: (row_blk(1, b, s), h))],
        out_shape=[shape, shape],
        scratch_shapes=[pltpu.VMEM((2 * nh, RET_DK, RET_DV), F32), pltpu.VMEM((2 * nh, c, c), F32),
                        pltpu.VMEM((4 * nh, c, 1), F32)],
        compiler_params=_cparams(("arbitrary",) * 3),
        name="retention",
    )(lg, qk, qk, v, gates, qk, qk, v, gates, gn, gn)
    return yf, yb


def _mla_up_kernel(dn_ref, gq_ref, gkv_ref, wq_ref, wk_ref, wv_ref, cos_ref, sin_ref,
                   q_ref, k_ref, v_ref, *, heads, scale):
    dn = dn_ref[...]
    lq = gq_ref.shape[1]
    lkv = gkv_ref.shape[1]

    def rms(a, g):
        return (a * lax.rsqrt(jnp.mean(a * a, axis=-1, keepdims=True) + EPS) * g).astype(BF16)

    cq = rms(dn[:, :lq], gq_ref[...])
    ckv = rms(dn[:, lq:lq + lkv], gkv_ref[...])
    cos = cos_ref[...]
    sin = sin_ref[...]
    half = MLA_DR // 4
    q = jnp.dot(cq, wq_ref[...], preferred_element_type=F32)
    kn = jnp.dot(ckv, wk_ref[...], preferred_element_type=F32)
    v_ref[...] = lax.dot_general(wv_ref[...], ckv, (((1,), (1,)), ((), ())),
                                 preferred_element_type=F32).astype(v_ref.dtype)
    kr = _rope(dn[:, lq + lkv:], cos, sin, half).astype(k_ref.dtype)
    for h in range(heads):
        o = h * 2 * MLA_DN
        q_ref[:, o:o + MLA_DN] = (q[:, o:o + MLA_DN] * scale).astype(q_ref.dtype)
        qr = _rope(q[:, o + MLA_DN:o + 2 * MLA_DN], cos, sin, half)
        q_ref[:, o + MLA_DN:o + 2 * MLA_DN] = (qr * scale).astype(q_ref.dtype)
        k_ref[:, o:o + MLA_DN] = kn[:, h * MLA_DN:(h + 1) * MLA_DN].astype(k_ref.dtype)
        k_ref[:, o + MLA_DN:o + 2 * MLA_DN] = kr


def _attn_lat_kernel(q_ref, kl_ref, vtl_ref, kc_ref, vtc_ref, o_ref):
    q = q_ref[...]
    tq = q.shape[0]
    dv = vtl_ref.shape[0]
    n_chunks = kl_ref.shape[0] // ATTN_TK

    def scores(k):
        return lax.dot_general(k, q, (((1,), (1,)), ((), ())), preferred_element_type=F32)

    def weighted(vt, p):
        lhs = jnp.concatenate([vt, jnp.ones((ATTN_ONES_ROWS, vt.shape[1]), BF16)], axis=0)
        return jnp.dot(lhs, p, preferred_element_type=F32)

    def exact_max_block(s, vt, m, acc, risk):
        m_new = jnp.maximum(m, jnp.max(s, axis=0, keepdims=True))
        p = jnp.exp2(s - m_new).astype(BF16)
        return m_new, jnp.exp2(m - m_new) * acc + weighted(vt, p), risk

    def lagged_max_block(s, vt, m, acc, risk):
        p = jnp.exp2(s - m).astype(BF16)
        bmax = jnp.max(s, axis=0, keepdims=True)
        m_new = jnp.maximum(m, bmax)
        return m_new, (acc + weighted(vt, p)) * jnp.exp2(m - m_new), jnp.maximum(risk, bmax - m)

    def run(later_block):
        state = (jnp.full((1, tq), -jnp.inf, F32), jnp.zeros((dv + ATTN_ONES_ROWS, tq), F32),
                 jnp.zeros((1, tq), F32))
        s_cur = scores(kc_ref[...])
        vt_cur = vtc_ref[...]
        for c in range(n_chunks):
            r = slice(c * ATTN_TK, (c + 1) * ATTN_TK)
            s_next = scores(kl_ref[r, :])
            state = (exact_max_block if c == 0 else later_block)(s_cur, vt_cur, *state)
            s_cur, vt_cur = s_next, vtl_ref[:, r]
        _, acc, risk = later_block(s_cur, vt_cur, *state)
        o_t = acc[:dv] / acc[dv:dv + 1]
        o_ref[...] = o_t.T.astype(o_ref.dtype)
        return risk

    risk = run(lagged_max_block)

    @pl.when(jnp.max(risk) > ATTN_MAX_LAG_LOG2)
    def _():
        run(exact_max_block)


def _attn_ctx_kernel(q_ref, kc_ref, vtc_ref, o_ref):
    s = lax.dot_general(q_ref[...], kc_ref[...], (((1,), (1,)), ((), ())), preferred_element_type=F32)
    p = jnp.exp2(s - jnp.max(s, axis=-1, keepdims=True))
    l = jnp.sum(p, axis=-1, keepdims=True)
    o = lax.dot_general(p.astype(BF16), vtc_ref[...], (((1,), (1,)), ((), ())), preferred_element_type=F32)
    o_ref[...] = (o / l).astype(o_ref.dtype)


def _mla_attention(q, k, vt, n_batch, n_lat, n_ctx, heads):
    m = q.shape[0]
    hd = 2 * MLA_DN
    nq = n_lat // ATTN_TQ
    ctx_blk0 = (n_batch * n_lat) // n_ctx
    kl = pl.BlockSpec((n_lat, hd), lambda b, h, t: (b, h))
    vl = pl.BlockSpec((MLA_DV, n_lat), lambda b, h, t: (h, b))
    kc = pl.BlockSpec((n_ctx, hd), lambda b, h, t: (ctx_blk0 + b, h))
    vc = pl.BlockSpec((MLA_DV, n_ctx), lambda b, h, t: (h, ctx_blk0 + b))
    out = pl.pallas_call(
        _attn_lat_kernel,
        grid=(n_batch, heads, nq),
        in_specs=[pl.BlockSpec((ATTN_TQ, hd), lambda b, h, t: (b * nq + t, h)), kl, vl, kc, vc],
        out_specs=pl.BlockSpec((ATTN_TQ, MLA_DV), lambda b, h, t: (b * nq + t, h)),
        out_shape=jax.ShapeDtypeStruct((n_batch * n_lat, heads * MLA_DV), BF16),
        compiler_params=_cparams(("arbitrary",) * 3),
        name="mla_attn_latent",
    )(q, k, vt, k, vt)
    out_ctx = pl.pallas_call(
        _attn_ctx_kernel,
        grid=(n_batch, heads),
        in_specs=[
            pl.BlockSpec((n_ctx, hd), lambda b, h: (ctx_blk0 + b, h)),
            pl.BlockSpec((n_ctx, hd), lambda b, h: (ctx_blk0 + b, h)),
            pl.BlockSpec((MLA_DV, n_ctx), lambda b, h: (h, ctx_blk0 + b)),
        ],
        out_specs=pl.BlockSpec((n_ctx, MLA_DV), lambda b, h: (b, h)),
        out_shape=jax.ShapeDtypeStruct((n_batch * n_ctx, heads * MLA_DV), BF16),
        compiler_params=_cparams(("arbitrary",) * 2),
        name="mla_attn_ctx",
    )(q, k, vt)
    return jnp.concatenate([out, out_ctx], axis=0)


def _sgu_kernel(u_ref, v_ref, gv_ref, ws_ref, bs_ref, wo_ref, x_ref, gate_ref, o_ref, z_scr):
    sd = v_ref.shape[1]
    gw = sd // SGU_GROUPS
    for ch in range(v_ref.shape[0] // SGU_CHUNK):
        rows = slice(ch * SGU_CHUNK, (ch + 1) * SGU_CHUNK)
        v = v_ref[rows, :].astype(F32)
        vn = (v * lax.rsqrt(jnp.mean(v * v, axis=-1, keepdims=True) + EPS) * gv_ref[...]).astype(BF16)
        for g in range(SGU_GROUPS):
            cols = slice(g * gw, (g + 1) * gw)
            z = jnp.dot(ws_ref[g], vn[:, cols], preferred_element_type=F32) + bs_ref[g]
            z_scr[rows, cols] = (u_ref[rows, cols].astype(F32) * z).astype(BF16)
    y = jnp.dot(z_scr[...], wo_ref[...], preferred_element_type=F32)
    o_ref[...] = x_ref[...] + gate_ref[...] * y


def _sgu(uv, g_v, w_s, b_s, w_out, x, mods, layer, geom):
    m, d = x.shape
    sd = w_out.shape[0]
    tpb, n_batch = geom
    tms = TM // 2
    scale = TM // tms
    gate_spec = pl.BlockSpec((None, None, 1, d),
                             lambda i: (layer * 6 + 2, jnp.minimum(i // (tpb * scale), n_batch), 0, 0))
    return pl.pallas_call(
        _sgu_kernel,
        grid=(m // tms,),
        in_specs=[
            pl.BlockSpec((tms, sd), lambda i: (i, 0)),
            pl.BlockSpec((tms, sd), lambda i: (i, 1)),
            pl.BlockSpec((1, sd), lambda i: (0, 0)),
            pl.BlockSpec((SGU_GROUPS, SGU_CHUNK, SGU_CHUNK), lambda i: (0, 0, 0)),
            pl.BlockSpec((SGU_GROUPS, SGU_CHUNK, 1), lambda i: (0, 0, 0)),
            pl.BlockSpec((sd, d), lambda i: (0, 0)),
            pl.BlockSpec((tms, d), lambda i: (i, 0)),
            gate_spec,
        ],
        out_specs=pl.BlockSpec((tms, d), lambda i: (i, 0)),
        out_shape=jax.ShapeDtypeStruct(x.shape, F32),
        scratch_shapes=[pltpu.VMEM((tms, sd), BF16)],
        input_output_aliases={6: 0},
        compiler_params=_cparams(("arbitrary",)),
        name="sgu",
    )(uv, uv, g_v.reshape(1, sd), w_s, b_s.reshape(SGU_GROUPS, SGU_CHUNK, 1), w_out, x, mods)


def _router_kernel(x_ref, g_ref, sh_ref, sc_ref, wr_ref, h_ref, idx_ref, wgt_ref, *, n_tiles):
    i = pl.program_id(0)

    @pl.when(i < n_tiles)
    def _():
        h = _modnorm(x_ref, g_ref, sh_ref, sc_ref)
        h_ref[...] = h.astype(h_ref.dtype)
        logits = jnp.dot(h, wr_ref[...], precision=HIGHEST, preferred_element_type=F32)
        lane = lax.broadcasted_iota(jnp.int32, logits.shape, 1).astype(F32)
        neg = jnp.float32(-jnp.inf)
        big = jnp.float32(logits.shape[1])
        logits = jnp.where(lane < N_EXPERTS, logits, neg)
        m1 = jnp.max(logits, axis=-1, keepdims=True)
        i1 = jnp.min(jnp.where(logits == m1, lane, big), axis=-1, keepdims=True)
        rest = jnp.where(lane == i1, neg, logits)
        m2 = jnp.max(rest, axis=-1, keepdims=True)
        i2 = jnp.min(jnp.where(rest == m2, lane, big), axis=-1, keepdims=True)
        e2 = jnp.exp(m2 - m1)
        w1 = 1.0 / (1.0 + e2)
        w2 = e2 / (1.0 + e2)
        idx_ref[...] = jnp.where(lane == 0, i1, jnp.where(lane == 1, i2, 0.0)).astype(jnp.int32)
        wgt_ref[...] = jnp.where(lane == 0, w1, jnp.where(lane == 1, w2, 0.0))

    @pl.when(i >= n_tiles)
    def _():
        h_ref[...] = jnp.zeros_like(h_ref)


def _router(x, g, mods, layer, w_router_pad, *, n_rows, h_rows, geom):
    d = x.shape[1]
    tpb, n_batch = geom
    sh_spec, sc_spec = _mod_specs(layer, 3, tpb, n_batch)
    nt = n_rows // TM
    real = lambda i: jnp.minimum(i, nt - 1)
    one = lambda f: (lambda i: f(real(i), 0))
    lanes = w_router_pad.shape[1]
    return pl.pallas_call(
        functools.partial(_router_kernel, n_tiles=nt),
        grid=(h_rows // TM,),
        in_specs=[
            pl.BlockSpec((TM, d), lambda i: (real(i), 0)),
            pl.BlockSpec((1, d), lambda i: (0, 0)),
            pl.BlockSpec(sh_spec.block_shape, one(sh_spec.index_map)),
            pl.BlockSpec(sc_spec.block_shape, one(sc_spec.index_map)),
            pl.BlockSpec((d, lanes), lambda i: (0, 0)),
        ],
        out_specs=[
            pl.BlockSpec((TM, d), lambda i: (i, 0)),
            pl.BlockSpec((TM, lanes), lambda i: (real(i), 0)),
            pl.BlockSpec((TM, lanes), lambda i: (real(i), 0)),
        ],
        out_shape=[
            jax.ShapeDtypeStruct((h_rows, d), BF16),
            jax.ShapeDtypeStruct((n_rows, lanes), jnp.int32),
            jax.ShapeDtypeStruct((n_rows, lanes), F32),
        ],
        compiler_params=_cparams(("arbitrary",)),
        name="moe_router",
    )(x, g.reshape(1, d), mods, mods, w_router_pad)


def _moe_ffn_kernel(te_ref, tr_ref, tf_ref, x_ref, wg32_ref, wu32_ref, wd32_ref, o_ref, acc_ref,
                    wg_ref, wu_ref, wd_ref):
    del te_ref
    t = pl.program_id(0)
    f = pl.program_id(1)
    n_rows = tr_ref[t]
    half_rows = MOE_TM // 2

    @pl.when(f == 0)
    def _():
        acc_ref[...] = jnp.zeros_like(acc_ref)

    @pl.when(tf_ref[t] == 1)
    def _():
        wg_ref[f] = wg32_ref[...].astype(BF16)
        wu_ref[f] = wu32_ref[...].astype(BF16)
        wd_ref[f] = wd32_ref[...].astype(BF16)

    def matmul(c):
        x = x_ref[_chunk(c), :]
        return (jnp.dot(x, wg_ref[f], preferred_element_type=F32),
                jnp.dot(x, wu_ref[f], preferred_element_type=F32))

    def epilogue(c, gu):
        a = (_silu(gu[0]) * gu[1]).astype(BF16)
        acc_ref[_chunk(c), :] += jnp.dot(a, wd_ref[f], preferred_element_type=F32)

    @pl.when(n_rows > half_rows)
    def _():
        _pipelined_chunks(MOE_TM // ROW_CHUNK, matmul, epilogue)

    @pl.when((n_rows > 0) & (n_rows <= half_rows))
    def _():
        _pipelined_chunks(half_rows // ROW_CHUNK, matmul, epilogue)

    @pl.when(f == pl.num_programs(1) - 1)
    def _():
        o_ref[...] = acc_ref[...].astype(o_ref.dtype)


def _moe_ffn(xs, tile_expert, tile_rows, tile_first, w_gu, w_down, lf):
    r, d = xs.shape
    ff = w_down.shape[2]
    nf = ff // MOE_BF
    n_tiles = r // MOE_TM

    def fidx(t, f, tf):
        return jnp.where(tf[t] == 1, f, nf - 1)

    grid_spec = pltpu.PrefetchScalarGridSpec(
        num_scalar_prefetch=3,
        grid=(n_tiles, nf),
        in_specs=[
            pl.BlockSpec((MOE_TM, d), lambda t, f, te, tr, tf: (t, 0)),
            pl.BlockSpec((None, None, d, MOE_BF), lambda t, f, te, tr, tf: (lf, te[t], 0, fidx(t, f, tf))),
            pl.BlockSpec((None, None, d, MOE_BF),
                         lambda t, f, te, tr, tf: (lf, te[t], 0, nf + fidx(t, f, tf))),
            pl.BlockSpec((None, None, MOE_BF, d), lambda t, f, te, tr, tf: (lf, te[t], fidx(t, f, tf), 0)),
        ],
        out_specs=pl.BlockSpec((MOE_TM, d), lambda t, f, te, tr, tf: (t, 0)),
        scratch_shapes=[pltpu.VMEM((MOE_TM, d), F32), pltpu.VMEM((nf, d, MOE_BF), BF16),
                        pltpu.VMEM((nf, d, MOE_BF), BF16), pltpu.VMEM((nf, MOE_BF, d), BF16)],
    )
    return pl.pallas_call(
        _moe_ffn_kernel,
        grid_spec=grid_spec,
        out_shape=jax.ShapeDtypeStruct((r, d), BF16),
        compiler_params=_cparams(("arbitrary", "arbitrary")),
        name="moe_ffn",
    )(tile_expert, tile_rows, tile_first, xs, w_gu, w_gu, w_down)


def _combine_kernel(x_ref, y0_ref, y1_ref, wgt_ref, gate_ref, gf_ref, o_ref, *, final):
    w = wgt_ref[...]
    y = w[:, 0:1] * y0_ref[...].astype(F32) + w[:, 1:2] * y1_ref[...].astype(F32)
    xn = x_ref[...] + gate_ref[...] * y
    if final:
        xn = xn * lax.rsqrt(jnp.mean(xn * xn, axis=-1, keepdims=True) + EPS) * gf_ref[...]
    o_ref[...] = xn


def _combine(x, y0, y1, wgt, mods, layer, g_final, *, n_rows, geom, final):
    d = x.shape[1]
    tpb, n_batch = geom
    lanes = wgt.shape[1]
    gate_spec = pl.BlockSpec((None, None, 1, d),
                             lambda i: (layer * 6 + 5, jnp.minimum(i // tpb, n_batch), 0, 0))
    row = lambda w: pl.BlockSpec((TM, w), lambda i: (i, 0))
    out_rows = n_rows if final else x.shape[0]
    return pl.pallas_call(
        functools.partial(_combine_kernel, final=final),
        grid=(n_rows // TM,),
        in_specs=[row(d), row(d), row(d), row(lanes), gate_spec, pl.BlockSpec((1, d), lambda i: (0, 0))],
        out_specs=row(d),
        out_shape=jax.ShapeDtypeStruct((out_rows, d), F32),
        input_output_aliases={} if final else {0: 0},
        compiler_params=_cparams(("arbitrary",)),
        name="moe_combine",
    )(x, y0, y1, wgt, mods, g_final.reshape(1, d))


def _moe_layer(x, g, mods, layer, w_router, w_gu, w_down, lf, g_final, *, n_rows, geom, final):
    d = x.shape[1]
    wr_pad = jnp.zeros((d, 128), F32).at[:, :N_EXPERTS].set(w_router)
    n_assign = 2 * n_rows
    r_rows = n_assign + (N_EXPERTS - 1) * MOE_TM
    h, idx, wgt = _router(x, g, mods, layer, wr_pad, n_rows=n_rows, h_rows=r_rows, geom=geom)
    e_flat = idx[:, :2].reshape(-1)
    onehot = (e_flat[:, None] == jnp.arange(N_EXPERTS, dtype=jnp.int32)[None, :]).astype(F32)
    blk = onehot.reshape(n_assign // CUMSUM_BLOCK, CUMSUM_BLOCK, N_EXPERTS)
    tril = jnp.tril(jnp.ones((CUMSUM_BLOCK, CUMSUM_BLOCK), F32))
    within = jnp.einsum("ij,bjk->bik", tril, blk)
    blk_tot = within[:, -1, :]
    blk_off = jnp.cumsum(blk_tot, axis=0) - blk_tot
    csum = (within + blk_off[:, None, :]).reshape(n_assign, N_EXPERTS)
    rank = (jnp.sum(onehot * csum, axis=1) - 1.0).astype(jnp.int32)
    counts = jnp.sum(blk_tot, axis=0).astype(jnp.int32)
    padded = ((counts + MOE_TM - 1) // MOE_TM) * MOE_TM
    ends = jnp.cumsum(padded)
    starts = ends - padded
    dest = starts[e_flat] + rank
    n_tiles = r_rows // MOE_TM
    src = (jnp.arange(r_rows, dtype=jnp.int32) % n_rows).at[dest].set(
        jnp.arange(n_assign, dtype=jnp.int32) // 2)
    tile_start = jnp.arange(n_tiles, dtype=jnp.int32) * MOE_TM
    last_tile = jnp.maximum(ends[-1] // MOE_TM - 1, 0)
    ts_used = jnp.minimum(tile_start, last_tile * MOE_TM)
    tile_expert = jnp.sum((ends[None, :] <= ts_used[:, None]).astype(jnp.int32), axis=1)
    tile_expert = jnp.minimum(tile_expert, N_EXPERTS - 1)
    real_end = (starts + counts)[tile_expert]
    used = tile_start < ends[-1]
    tile_rows = jnp.where(used, jnp.clip(real_end - tile_start, 0, MOE_TM), 0).astype(jnp.int32)
    tile_first = (used & (tile_start == starts[tile_expert])).astype(jnp.int32)
    xs = jnp.take(h, src, axis=0, mode="clip")
    ys = _moe_ffn(xs, tile_expert, tile_rows, tile_first, w_gu, w_down, lf)
    d2 = dest.reshape(n_rows, 2)
    y0 = jnp.take(ys, d2[:, 0], axis=0, mode="clip")
    y1 = jnp.take(ys, d2[:, 1], axis=0, mode="clip")
    return _combine(x, y0, y1, wgt, mods, layer, g_final, n_rows=n_rows, geom=geom, final=final)


def kernel(x, c, ctx, c_ctx, ada_w, ada_b, norm_mix_g, norm_ffn_g, final_norm_g, ret_w_in, ret_w_o, ret_decay_f, ret_decay_b, ret_gn_f, ret_gn_b, mla_w_down, mla_g_q, mla_g_kv, mla_w_uq, mla_w_ukv, mla_w_o, sgu_w_in, sgu_g_v, sgu_w_s, sgu_b_s, sgu_w_out, ffn_w_gu, ffn_w_down, moe_w_router, moe_w_gu, moe_w_down):
    n_batch, n_lat, d = x.shape
    n_ctx = ctx.shape[1]
    depth = ada_w.shape[0]
    rows_lat = n_batch * n_lat
    m = rows_lat + n_batch * n_ctx
    geom = (n_lat // TM, n_batch)
    assert n_batch * n_ctx == TM and d == 1024 and n_batch + 1 <= 8

    xt = jnp.concatenate([x.reshape(rows_lat, d), ctx.reshape(n_batch * n_ctx, d)], axis=0)
    cond = jnp.concatenate([c, c_ctx[None, :], jnp.zeros((8 - n_batch - 1, d), F32)], axis=0)
    mods = _adaln(cond, ada_w, ada_b)
    mods = mods.reshape(depth, 8, 6, d).transpose(0, 2, 1, 3).reshape(depth * 6, 8, 1, d)

    grid_rows = n_lat // GRID_W
    ret_cos, ret_sin = _rope_tables(grid_rows, n_batch * n_ctx, RET_DK)
    mla_cos, mla_sin = _rope_tables(grid_rows, n_batch * n_ctx, MLA_DR)
    pad_r = MLA_DN - MLA_DR
    mla_cos = jnp.concatenate([mla_cos, jnp.ones((mla_cos.shape[0], pad_r), F32)], axis=1)
    mla_sin = jnp.concatenate([mla_sin, jnp.zeros((mla_sin.shape[0], pad_r), F32)], axis=1)

    for i in range(depth):
        last = i == depth - 1
        kind, jm = i % N_MIXERS, i // N_MIXERS
        rows_out = rows_lat if last else m
        if kind == 0:
            qk, vv, gates = _ret_in(xt, norm_mix_g[i], mods, i, ret_w_in[jm].astype(BF16),
                                    (ret_cos, ret_sin), geom=geom)
            lg = jnp.stack([jax.nn.log_sigmoid(ret_decay_f[jm].astype(F32)),
                            jax.nn.log_sigmoid(ret_decay_b[jm].astype(F32))])
            hv = vv.shape[1]
            gn = jnp.stack([ret_gn_f[jm], ret_gn_b[jm]]).reshape(2, 1, hv)
            yf, yb = _retention(qk, vv, gates, lg, gn, n_batch, n_lat, n_ctx)
            a_specs = [pl.BlockSpec((TM, hv), lambda r: (r, 0)), pl.BlockSpec((TM, hv), lambda r: (r, 0))]
            xt = _outproj([yf, yb], a_specs, ret_w_o[jm].astype(BF16), xt, mods, i, 2,
                          n_rows=rows_out, geom=geom)
        elif kind == 1:
            heads = mla_w_o.shape[1] // MLA_DV
            lq, lkv = mla_g_q.shape[1], mla_g_kv.shape[1]
            n_dn = lq + lkv + MLA_DN
            wd = jnp.zeros((d, n_dn), F32).at[:, :lq + lkv + MLA_DR].set(mla_w_down[jm]).astype(BF16)
            dn = _proj(xt, norm_mix_g[i], mods, i, 0, wd, mode="f32", bn=n_dn, out_dtype=F32,
                       n_rows=m, geom=geom)
            wq = mla_w_uq[jm].reshape(lq, heads, MLA_DN + MLA_DR)
            wq = jnp.concatenate([wq, jnp.zeros((lq, heads, pad_r), F32)], axis=2)
            wq = wq.reshape(lq, heads * 2 * MLA_DN).astype(BF16)
            wkv = mla_w_ukv[jm].reshape(lkv, heads, MLA_DN + MLA_DV)
            wk = wkv[:, :, :MLA_DN].reshape(lkv, heads * MLA_DN).astype(BF16)
            wv = wkv[:, :, MLA_DN:].reshape(lkv, heads * MLA_DV).T.astype(BF16)
            tpb = geom[0]
            n_lat_tiles = tpb * n_batch
            pos_idx = lambda r: (jnp.where(r < n_lat_tiles, r % tpb, tpb + (r - n_lat_tiles)), 0)
            full = lambda a: pl.BlockSpec(a.shape, lambda r: (0,) * a.ndim)
            gq = mla_g_q[jm].reshape(1, lq)
            gkv = mla_g_kv[jm].reshape(1, lkv)
            scale = float((MLA_DN + MLA_DR) ** -0.5 * LOG2_E)
            q, k, vt = pl.pallas_call(
                functools.partial(_mla_up_kernel, heads=heads, scale=scale),
                grid=(m // TM,),
                in_specs=[pl.BlockSpec((TM, n_dn), lambda r: (r, 0)), full(gq), full(gkv), full(wq),
                          full(wk), full(wv), pl.BlockSpec((TM, MLA_DN), pos_idx),
                          pl.BlockSpec((TM, MLA_DN), pos_idx)],
                out_specs=[pl.BlockSpec((TM, heads * 2 * MLA_DN), lambda r: (r, 0)),
                           pl.BlockSpec((TM, heads * 2 * MLA_DN), lambda r: (r, 0)),
                           pl.BlockSpec((heads * MLA_DV, TM), lambda r: (0, r))],
                out_shape=[jax.ShapeDtypeStruct((m, heads * 2 * MLA_DN), BF16),
                           jax.ShapeDtypeStruct((m, heads * 2 * MLA_DN), BF16),
                           jax.ShapeDtypeStruct((heads * MLA_DV, m), BF16)],
                compiler_params=_cparams(("arbitrary",)),
                name="mla_up",
            )(dn, gq, gkv, wq, wk, wv, mla_cos, mla_sin)
            ao = _mla_attention(q, k, vt, n_batch, n_lat, n_ctx, heads)
            a_specs = [pl.BlockSpec((TM, heads * MLA_DV), lambda r: (r, 0))]
            xt = _outproj([ao], a_specs, mla_w_o[jm].astype(BF16), xt, mods, i, 2,
                          n_rows=rows_out, geom=geom)
        else:
            uv = _proj(xt, norm_mix_g[i], mods, i, 0, sgu_w_in[jm].astype(BF16), mode="gelu", bn=2048,
                       out_dtype=BF16, n_rows=m, geom=geom)
            xt = _sgu(uv, sgu_g_v[jm], sgu_w_s[jm].astype(BF16), sgu_b_s[jm], sgu_w_out[jm].astype(BF16),
                      xt, mods, i, geom)
        f = i // 2
        if i % 2 == 0:
            dff = ffn_w_down.shape[1]
            wgu = ffn_w_gu[f].astype(BF16)
            xt = _dense_ffn(xt, norm_ffn_g[i], mods, i, wgu[:, :dff], wgu[:, dff:],
                            ffn_w_down[f].astype(BF16), n_rows=rows_out, geom=geom)
        else:
            xt = _moe_layer(xt, norm_ffn_g[i], mods, i, moe_w_router[f], moe_w_gu, moe_w_down, f,
                            final_norm_g, n_rows=rows_out, geom=geom, final=last)
    if not (depth - 1) % 2:
        raise NotImplementedError("final norm is fused into the expert combine of the last layer")
    return xt.reshape(n_batch, n_lat, d)
```

```python
import functools

import jax
import jax.numpy as jnp
from jax import lax
from jax.experimental import pallas as pl
from jax.experimental.pallas import tpu as pltpu

F32 = jnp.float32
BF16 = jnp.bfloat16
HIGHEST = lax.Precision.HIGHEST
LOG2_E = 1.4426950408889634

GRID_W = 64
ROPE_BASE = 10000.0
EPS = 1e-6
RET_DK = 256
RET_DV = 512
MLA_DN = 128
MLA_DR = 64
MLA_DV = 128
SGU_GROUPS = 8
SGU_CHUNK = 128
N_EXPERTS = 8
N_MIXERS = 3

TM = 1024
ROW_CHUNK = 256
RET_CHUNK = 256
RET_HEADS_PER_STEP = 2
RET_IN_STEPS = 4
ATTN_TQ = 512
ATTN_TK = 1024
ATTN_ONES_ROWS = 16
ATTN_MAX_LAG_LOG2 = 60.0
MOE_TM = 1024
MOE_BF = 512
CUMSUM_BLOCK = 256
VMEM_LIMIT = 56 * 1024 * 1024


def _cparams(sem):
    return pltpu.CompilerParams(dimension_semantics=sem, vmem_limit_bytes=VMEM_LIMIT)


def _silu(x):
    return x * jax.nn.sigmoid(x)


def _rope(x, cos, sin_signed, half):
    w = x.shape[-1]
    lane = lax.broadcasted_iota(jnp.int32, x.shape, 1)
    first = (lane % (2 * half)) < half
    rot = jnp.where(first, pltpu.roll(x, w - half, 1), pltpu.roll(x, half, 1))
    return x * cos + rot * sin_signed


def _rope_tables(rows, n_ctx_rows, rot_dim):
    row = jnp.repeat(jnp.arange(rows, dtype=F32), GRID_W)
    col = jnp.tile(jnp.arange(GRID_W, dtype=F32), rows)
    axis_dim = rot_dim // 2
    inv_freq = ROPE_BASE ** (-jnp.arange(0, axis_dim, 2, dtype=F32) / axis_dim)
    ang_r = row[:, None] * inv_freq
    ang_c = col[:, None] * inv_freq
    ang = jnp.concatenate([ang_r, ang_r, ang_c, ang_c], axis=-1)
    quarter = rot_dim // 4
    sign = jnp.where((jnp.arange(rot_dim) % (2 * quarter)) < quarter, -1.0, 1.0).astype(F32)
    cos = jnp.concatenate([jnp.cos(ang), jnp.ones((n_ctx_rows, rot_dim), F32)], axis=0)
    sin = jnp.concatenate([jnp.sin(ang) * sign, jnp.zeros((n_ctx_rows, rot_dim), F32)], axis=0)
    return cos, sin


def _adaln_kernel(cond_ref, w_ref, b_ref, o_ref):
    s = _silu(cond_ref[...])
    o_ref[...] = jnp.dot(s, w_ref[...], precision=HIGHEST, preferred_element_type=F32) + b_ref[...]


def _adaln(cond, ada_w, ada_b):
    depth, d, n6 = ada_w.shape
    bn = n6 // 4
    return pl.pallas_call(
        _adaln_kernel,
        grid=(depth, n6 // bn),
        in_specs=[
            pl.BlockSpec((8, d), lambda l, j: (0, 0)),
            pl.BlockSpec((None, d, bn), lambda l, j: (l, 0, j)),
            pl.BlockSpec((None, 1, bn), lambda l, j: (l, 0, j)),
        ],
        out_specs=pl.BlockSpec((None, 8, bn), lambda l, j: (l, 0, j)),
        out_shape=jax.ShapeDtypeStruct((depth, 8, n6), F32),
        compiler_params=_cparams(("arbitrary", "arbitrary")),
        name="adaln",
    )(cond, ada_w, ada_b.reshape(depth, 1, n6))


def _row_specs(src, width, n_main_tiles):
    if isinstance(src, tuple):
        specs = [pl.BlockSpec((TM, width), lambda i, *_: (jnp.minimum(i, n_main_tiles - 1), 0)),
                 pl.BlockSpec((TM, width), lambda i, *_: (jnp.maximum(i - n_main_tiles, 0), 0))]
        return specs, list(src)
    return [pl.BlockSpec((TM, width), lambda i, *_: (i, 0))], [src]


def _row_tile(refs, i, n_main_tiles):
    if len(refs) == 2:
        return jnp.where(i < n_main_tiles, refs[0][...], refs[1][...])
    return refs[0][...]


def _modnorm(xf, g_ref, shift_ref, scale_ref):
    y = xf * lax.rsqrt(jnp.mean(xf * xf, axis=-1, keepdims=True) + EPS)
    return (y * g_ref[...]) * (1.0 + scale_ref[...]) + shift_ref[...]


def _chunk(c):
    return slice(c * ROW_CHUNK, (c + 1) * ROW_CHUNK)


def _pipelined_chunks(n_chunks, matmul, epilogue):
    pending = matmul(0)
    for c in range(n_chunks):
        following = matmul(c + 1) if c + 1 < n_chunks else None
        epilogue(c, pending)
        pending = following


def _ret_in_kernel(*refs, n_q, x_count, n_main_tiles):
    x_refs = refs[:x_count]
    (g_ref, sh_ref, sc_ref, wqk_ref, wv_ref, wg_ref, cos_ref, sin_ref,
     qk_ref, v_ref, gate_ref, h_scr) = refs[x_count:]
    j = pl.program_id(1)

    @pl.when(j == 0)
    def _():
        xf = _row_tile(x_refs, pl.program_id(0), n_main_tiles)
        h_scr[...] = _modnorm(xf, g_ref, sh_ref, sc_ref).astype(BF16)

    kscale = jnp.where(j >= n_q, RET_DK ** -0.5, 1.0).astype(F32)

    def matmul(c):
        h = h_scr[_chunk(c), :]
        return tuple(jnp.dot(h, w[...], preferred_element_type=F32) for w in (wqk_ref, wv_ref, wg_ref))

    def epilogue(c, ys):
        r = _chunk(c)
        for hs in range(ys[0].shape[1] // RET_DK):
            cols = slice(hs * RET_DK, (hs + 1) * RET_DK)
            qk = _rope(ys[0][:, cols], cos_ref[r, :], sin_ref[r, :], RET_DK // 4)
            qk_ref[r, cols] = (qk * kscale).astype(qk_ref.dtype)
        v_ref[r, :] = ys[1].astype(v_ref.dtype)
        gate_ref[r, :] = _silu(ys[2]).astype(gate_ref.dtype)

    _pipelined_chunks(h_scr.shape[0] // ROW_CHUNK, matmul, epilogue)


def _ret_in(x, g, mods, layer, w_in, rope, *, geom):
    x_parts = list(x) if isinstance(x, tuple) else [x]
    m = sum(p.shape[0] for p in x_parts)
    d = x_parts[0].shape[1]
    heads = d // RET_DK
    nj = RET_IN_STEPS
    n_q = nj // 2
    qk_cols = (2 * heads * RET_DK) // nj
    wv_cols = (heads * RET_DV) // nj
    wg_cols = (2 * heads * RET_DV) // nj
    assert wv_cols == qk_cols and wg_cols == 2 * qk_cols and qk_cols % RET_DK == 0
    tpb, n_batch = geom
    sh_spec, sc_spec = _mod_specs(layer, 0, tpb, n_batch)
    n_lat_tiles = tpb * n_batch
    cos, sin = rope

    def pos_idx(i, j):
        return (jnp.where(i < n_lat_tiles, i % tpb, tpb + (i - n_lat_tiles)), 0)

    x_specs, x_arrs = _row_specs(x, d, n_lat_tiles)
    return pl.pallas_call(
        functools.partial(_ret_in_kernel, n_q=n_q, x_count=len(x_arrs), n_main_tiles=n_lat_tiles),
        grid=(m // TM, nj),
        in_specs=x_specs + [
            pl.BlockSpec((1, d), lambda i, j: (0, 0)),
            sh_spec, sc_spec,
            pl.BlockSpec((d, qk_cols), lambda i, j: (0, j)),
            pl.BlockSpec((d, qk_cols), lambda i, j: (0, nj + j)),
            pl.BlockSpec((d, 2 * qk_cols), lambda i, j: (0, nj + j)),
            pl.BlockSpec((TM, RET_DK), pos_idx),
            pl.BlockSpec((TM, RET_DK), pos_idx),
        ],
        out_specs=[
            pl.BlockSpec((TM, qk_cols), lambda i, j: (i, j)),
            pl.BlockSpec((TM, qk_cols), lambda i, j: (i, j)),
            pl.BlockSpec((TM, 2 * qk_cols), lambda i, j: (i, j)),
        ],
        out_shape=[
            jax.ShapeDtypeStruct((m, 2 * heads * RET_DK), BF16),
            jax.ShapeDtypeStruct((m, heads * RET_DV), BF16),
            jax.ShapeDtypeStruct((m, 2 * heads * RET_DV), BF16),
        ],
        scratch_shapes=[pltpu.VMEM((TM, d), BF16)],
        compiler_params=_cparams(("arbitrary", "arbitrary")),
        name="ret_in",
    )(*x_arrs, g.reshape(1, d), mods, mods, w_in, w_in, w_in, cos, sin)


def _proj_kernel(x_ref, g_ref, sh_ref, sc_ref, w_ref, o_ref, h_scr, *, mode):
    j = pl.program_id(1)

    @pl.when(j == 0)
    def _():
        h_scr[...] = _modnorm(x_ref[...], g_ref, sh_ref, sc_ref).astype(BF16)

    def matmul(c):
        return jnp.dot(h_scr[_chunk(c), :], w_ref[...], preferred_element_type=F32)

    def epilogue(c, y):
        if mode == "f32":
            o_ref[_chunk(c), :] = y
        elif mode == "gelu":
            o_ref[_chunk(c), :] = jax.nn.gelu(y).astype(o_ref.dtype)

    _pipelined_chunks(h_scr.shape[0] // ROW_CHUNK, matmul, epilogue)


def _mod_specs(layer, k_shift, tiles_per_batch, n_batch):
    def spec(k):
        return pl.BlockSpec((None, None, 1, 1024),
                            lambda i, j: (layer * 6 + k, jnp.minimum(i // tiles_per_batch, n_batch), 0, 0))
    return spec(k_shift), spec(k_shift + 1)


def _proj(x, g, mods, layer, k_shift, w, *, mode, bn, out_dtype, n_rows, geom):
    d = x.shape[1]
    n = w.shape[1]
    nt = n_rows // TM
    tpb, n_batch = geom
    sh_spec, sc_spec = _mod_specs(layer, k_shift, tpb, n_batch)
    in_specs = [
        pl.BlockSpec((TM, d), lambda i, j: (i, 0)),
        pl.BlockSpec((1, d), lambda i, j: (0, 0)),
        sh_spec, sc_spec,
        pl.BlockSpec((d, bn), lambda i, j: (0, j)),
    ]
    args = [x, g.reshape(1, d), mods, mods, w]
    return pl.pallas_call(
        functools.partial(_proj_kernel, mode=mode),
        grid=(nt, n // bn),
        in_specs=in_specs,
        out_specs=pl.BlockSpec((TM, bn), lambda i, j: (i, j)),
        out_shape=jax.ShapeDtypeStruct((n_rows, n), out_dtype),
        scratch_shapes=[pltpu.VMEM((TM, d), BF16)],
        compiler_params=_cparams(("arbitrary", "arbitrary")),
        name="proj_" + mode,
    )(*args)


def _dense_ffn_kernel(x_ref, g_ref, sh_ref, sc_ref, gate_ref, wg_ref, wu_ref, wd_ref, o_ref, h_scr):
    h_scr[...] = _modnorm(x_ref[...], g_ref, sh_ref, sc_ref).astype(BF16)

    def matmul(c):
        h = h_scr[_chunk(c), :]
        return (jnp.dot(h, wg_ref[...], preferred_element_type=F32),
                jnp.dot(h, wu_ref[...], preferred_element_type=F32))

    def epilogue(c, gu):
        a = (_silu(gu[0]) * gu[1]).astype(BF16)
        y = jnp.dot(a, wd_ref[...], preferred_element_type=F32)
        o_ref[_chunk(c), :] = x_ref[_chunk(c), :] + gate_ref[...] * y

    _pipelined_chunks(h_scr.shape[0] // ROW_CHUNK, matmul, epilogue)


def _dense_ffn(x, g, mods, layer, w_g, w_u, w_d, *, n_rows, geom):
    d = x.shape[1]
    dff = w_d.shape[0]
    tpb, n_batch = geom

    def mod_spec(k):
        return pl.BlockSpec((None, None, 1, d),
                            lambda i: (layer * 6 + k, jnp.minimum(i // tpb, n_batch), 0, 0))

    def resident(shape):
        return pl.BlockSpec(shape, lambda i: (0, 0), pipeline_mode=pl.Buffered(1))

    return pl.pallas_call(
        _dense_ffn_kernel,
        grid=(n_rows // TM,),
        in_specs=[
            pl.BlockSpec((TM, d), lambda i: (i, 0)),
            pl.BlockSpec((1, d), lambda i: (0, 0)),
            mod_spec(3), mod_spec(4), mod_spec(5),
            resident((d, dff)), resident((d, dff)), resident((dff, d)),
        ],
        out_specs=pl.BlockSpec((TM, d), lambda i: (i, 0)),
        out_shape=jax.ShapeDtypeStruct(x.shape, F32),
        scratch_shapes=[pltpu.VMEM((TM, d), BF16)],
        input_output_aliases={0: 0},
        compiler_params=_cparams(("arbitrary",)),
        name="dense_ffn",
    )(x, g.reshape(1, d), mods, mods, mods, w_g, w_u, w_d)


def _outproj_kernel(*refs, a_counts, x_count, n_main_tiles):
    i = pl.program_id(0)
    refs = list(refs)
    a_tiles = []
    for cnt in a_counts:
        a_tiles.append(_row_tile(refs[:cnt], i, n_main_tiles))
        refs = refs[cnt:]
    w_ref = refs[0]
    x = _row_tile(refs[1:1 + x_count], i, n_main_tiles)
    gate_ref, o_ref = refs[1 + x_count:]
    if len(a_tiles) == 1:
        a = a_tiles[0]
    else:
        a = (a_tiles[0].astype(F32) + a_tiles[1].astype(F32)).astype(BF16)
    y = jnp.dot(a, w_ref[...], preferred_element_type=F32)
    o_ref[...] = x + gate_ref[...] * y


def _outproj(a_list, w, x, mods, layer, k_gate, *, n_rows, geom):
    k, d = w.shape
    nt = n_rows // TM
    tpb, n_batch = geom
    n_main = tpb * n_batch
    gate_spec = pl.BlockSpec((None, None, 1, d),
                             lambda i: (layer * 6 + k_gate, jnp.minimum(i // tpb, n_batch), 0, 0))
    in_specs, args, a_counts = [], [], []
    for a in a_list:
        specs, arrs = _row_specs(a, k, n_main)
        in_specs += specs
        args += arrs
        a_counts.append(len(arrs))
    x_specs, x_arrs = _row_specs(x, d, n_main)
    total_rows = sum(xa.shape[0] for xa in x_arrs)
    return pl.pallas_call(
        functools.partial(_outproj_kernel, a_counts=tuple(a_counts), x_count=len(x_arrs), n_main_tiles=n_main),
        grid=(nt,),
        in_specs=in_specs + [pl.BlockSpec((k, d), lambda i: (0, 0))] + x_specs + [gate_spec],
        out_specs=pl.BlockSpec((TM, d), lambda i: (i, 0)),
        out_shape=jax.ShapeDtypeStruct((total_rows, d), F32),
        input_output_aliases={len(args) + 1: 0} if len(x_arrs) == 1 else {},
        compiler_params=_cparams(("arbitrary",)),
        name="outproj",
    )(*args, w, *x_arrs, mods)


def _ret_kernel(lg_ref, qf_ref, kf_ref, vf_ref, gf_ref, qb_ref, kb_ref, vb_ref, gb_ref, gnf_ref, gnb_ref,
                of_ref, ob_ref, s_scr, intra_scr, dec_scr):
    hp = pl.program_id(1)
    s = pl.program_id(2)
    c = RET_CHUNK
    nh = RET_HEADS_PER_STEP
    chains = [(d, j) for d in range(2) for j in range(nh)]

    @pl.when(s == 0)
    def _():
        s_scr[...] = jnp.zeros_like(s_scr)
        ii = lax.broadcasted_iota(jnp.int32, (c, c), 0)
        jj = lax.broadcasted_iota(jnp.int32, (c, c), 1)
        pos = lax.broadcasted_iota(jnp.int32, (c, 1), 0).astype(F32)
        for n, (d, j) in enumerate(chains):
            lg = lg_ref[d, hp * nh + j]
            diff = ((ii - jj) if d == 0 else (jj - ii)).astype(F32)
            intra_scr[n] = jnp.where(diff >= 0, jnp.exp(lg * jnp.maximum(diff, 0.0)), 0.0)
            fpos = pos if d == 0 else (c - 1.0) - pos
            dec_scr[2 * n] = jnp.exp(lg * (fpos + 1.0))
            dec_scr[2 * n + 1] = jnp.exp(lg * ((c - 1.0) - fpos))

    refs = ((qf_ref, kf_ref, vf_ref, gf_ref, gnf_ref, of_ref), (qb_ref, kb_ref, vb_ref, gb_ref, gnb_ref, ob_ref))
    qcols = [slice(j * RET_DK, (j + 1) * RET_DK) for _, j in chains]
    vcols = [slice(j * RET_DV, (j + 1) * RET_DV) for _, j in chains]
    n_chains = range(len(chains))
    q = [refs[d][0][:, qcols[n]] for n, (d, _) in enumerate(chains)]
    k = [refs[d][1][:, qcols[n]] for n, (d, _) in enumerate(chains)]
    v = [refs[d][2][:, vcols[n]] for n, (d, _) in enumerate(chains)]
    contract_last = (((1,), (1,)), ((), ()))
    contract_rows = (((0,), (0,)), ((), ()))
    scores = [lax.dot_general(q[n], k[n], contract_last, preferred_element_type=F32) for n in n_chains]
    masked = [(scores[n] * intra_scr[n]).astype(BF16) for n in n_chains]
    state = [s_scr[n] for n in n_chains]
    qd = [(q[n].astype(F32) * dec_scr[2 * n]).astype(BF16) for n in n_chains]
    o = [jnp.dot(masked[n], v[n], preferred_element_type=F32)
         + jnp.dot(qd[n], state[n].astype(BF16), preferred_element_type=F32) for n in n_chains]
    kd = [(k[n].astype(F32) * dec_scr[2 * n + 1]).astype(BF16) for n in n_chains]
    for n, (d, j) in enumerate(chains):
        chunk_decay = jnp.exp(lg_ref[d, hp * nh + j] * float(c))
        s_scr[n] = state[n] * chunk_decay + lax.dot_general(kd[n], v[n], contract_rows,
                                                           preferred_element_type=F32)
    for n, (d, _) in enumerate(chains):
        gate_ref, gn_ref, o_ref = refs[d][3:]
        mu = jnp.mean(o[n], axis=-1, keepdims=True)
        oc = o[n] - mu
        var = jnp.mean(oc * oc, axis=-1, keepdims=True)
        y = oc * lax.rsqrt(var + EPS) * gn_ref[:, vcols[n]]
        o_ref[:, vcols[n]] = (gate_ref[:, vcols[n]].astype(F32) * y).astype(o_ref.dtype)


def _retention(qk, v, gates, lg, gn, n_batch, n_lat, n_ctx):
    m = qk.shape[0]
    heads = v.shape[1] // RET_DV
    c = RET_CHUNK
    lat_chunks = n_lat // c
    assert n_ctx == c
    steps = 1 + lat_chunks
    ctx_blk0 = (n_batch * n_lat) // c

    def row_blk(d, b, s):
        t = (s - 1) if d == 0 else (lat_chunks - s)
        return jnp.where(s == 0, ctx_blk0 + b, b * lat_chunks + t)

    nh = RET_HEADS_PER_STEP
    groups = heads // nh
    qw, vw = nh * RET_DK, nh * RET_DV

    def specs(d):
        return [
            pl.BlockSpec((c, qw), lambda b, h, s: (row_blk(d, b, s), h)),
            pl.BlockSpec((c, qw), lambda b, h, s: (row_blk(d, b, s), groups + h)),
            pl.BlockSpec((c, vw), lambda b, h, s: (row_blk(d, b, s), h)),
            pl.BlockSpec((c, vw), lambda b, h, s: (row_blk(d, b, s), groups * d + h)),
        ]

    def gn_spec(d):
        return pl.BlockSpec((None, 1, vw), lambda b, h, s: (d, 0, h))

    shape = jax.ShapeDtypeStruct((m, heads * RET_DV), BF16)
    yf, yb = pl.pallas_call(
        _ret_kernel,
        grid=(n_batch, groups, steps),
        in_specs=[pl.BlockSpec(memory_space=pltpu.SMEM)] + specs(0) + specs(1) + [gn_spec(0), gn_spec(1)],
        out_specs=[pl.BlockSpec((c, vw), lambda b, h, s: (row_blk(0, b, s), h)),
                   pl.BlockSpec((c, vw), lambda b, h, s: (row_blk(1, b, s), h))],
        out_shape=[shape, shape],
        scratch_shapes=[pltpu.VMEM((2 * nh, RET_DK, RET_DV), F32), pltpu.VMEM((2 * nh, c, c), F32),
                        pltpu.VMEM((4 * nh, c, 1), F32)],
        compiler_params=_cparams(("arbitrary",) * 3),
        name="retention",
    )(lg, qk, qk, v, gates, qk, qk, v, gates, gn, gn)
    return yf, yb


def _mla_up_kernel(dn_ref, gq_ref, gkv_ref, wq_ref, wk_ref, wv_ref, cos_ref, sin_ref,
                   q_ref, k_ref, v_ref, *, heads, scale):
    dn = dn_ref[...]
    lq = gq_ref.shape[1]
    lkv = gkv_ref.shape[1]

    def rms(a, g):
        return (a * lax.rsqrt(jnp.mean(a * a, axis=-1, keepdims=True) + EPS) * g).astype(BF16)

    cq = rms(dn[:, :lq], gq_ref[...])
    ckv = rms(dn[:, lq:lq + lkv], gkv_ref[...])
    cos = cos_ref[...]
    sin = sin_ref[...]
    half = MLA_DR // 4
    q = jnp.dot(cq, wq_ref[...], preferred_element_type=F32)
    kn = jnp.dot(ckv, wk_ref[...], preferred_element_type=F32)
    v_ref[...] = lax.dot_general(wv_ref[...], ckv, (((1,), (1,)), ((), ())),
                                 preferred_element_type=F32).astype(v_ref.dtype)
    kr = _rope(dn[:, lq + lkv:], cos, sin, half).astype(k_ref.dtype)
    for h in range(heads):
        o = h * 2 * MLA_DN
        q_ref[:, o:o + MLA_DN] = (q[:, o:o + MLA_DN] * scale).astype(q_ref.dtype)
        qr = _rope(q[:, o + MLA_DN:o + 2 * MLA_DN], cos, sin, half)
        q_ref[:, o + MLA_DN:o + 2 * MLA_DN] = (qr * scale).astype(q_ref.dtype)
        k_ref[:, o:o + MLA_DN] = kn[:, h * MLA_DN:(h + 1) * MLA_DN].astype(k_ref.dtype)
        k_ref[:, o + MLA_DN:o + 2 * MLA_DN] = kr


def _attn_lat_kernel(q_ref, kl_ref, vtl_ref, kc_ref, vtc_ref, o_ref):
    q = q_ref[...]
    tq = q.shape[0]
    dv = vtl_ref.shape[0]
    n_chunks = kl_ref.shape[0] // ATTN_TK

    def scores(k):
        return lax.dot_general(k, q, (((1,), (1,)), ((), ())), preferred_element_type=F32)

    def weighted(vt, p):
        lhs = jnp.concatenate([vt, jnp.ones((ATTN_ONES_ROWS, vt.shape[1]), BF16)], axis=0)
        return jnp.dot(lhs, p, preferred_element_type=F32)

    def exact_max_block(s, vt, m, acc, risk):
        m_new = jnp.maximum(m, jnp.max(s, axis=0, keepdims=True))
        p = jnp.exp2(s - m_new).astype(BF16)
        return m_new, jnp.exp2(m - m_new) * acc + weighted(vt, p), risk

    def lagged_max_block(s, vt, m, acc, risk):
        p = jnp.exp2(s - m).astype(BF16)
        bmax = jnp.max(s, axis=0, keepdims=True)
        m_new = jnp.maximum(m, bmax)
        return m_new, (acc + weighted(vt, p)) * jnp.exp2(m - m_new), jnp.maximum(risk, bmax - m)

    def run(later_block):
        state = (jnp.full((1, tq), -jnp.inf, F32), jnp.zeros((dv + ATTN_ONES_ROWS, tq), F32),
                 jnp.zeros((1, tq), F32))
        s_cur = scores(kc_ref[...])
        vt_cur = vtc_ref[...]
        for c in range(n_chunks):
            r = slice(c * ATTN_TK, (c + 1) * ATTN_TK)
            s_next = scores(kl_ref[r, :])
            state = (exact_max_block if c == 0 else later_block)(s_cur, vt_cur, *state)
            s_cur, vt_cur = s_next, vtl_ref[:, r]
        _, acc, risk = later_block(s_cur, vt_cur, *state)
        o_t = acc[:dv] / acc[dv:dv + 1]
        o_ref[...] = o_t.T.astype(o_ref.dtype)
        return risk

    risk = run(lagged_max_block)

    @pl.when(jnp.max(risk) > ATTN_MAX_LAG_LOG2)
    def _():
        run(exact_max_block)


def _attn_ctx_kernel(q_ref, kc_ref, vtc_ref, o_ref):
    s = lax.dot_general(q_ref[...], kc_ref[...], (((1,), (1,)), ((), ())), preferred_element_type=F32)
    p = jnp.exp2(s - jnp.max(s, axis=-1, keepdims=True))
    l = jnp.sum(p, axis=-1, keepdims=True)
    o = lax.dot_general(p.astype(BF16), vtc_ref[...], (((1,), (1,)), ((), ())), preferred_element_type=F32)
    o_ref[...] = (o / l).astype(o_ref.dtype)


def _mla_attention(q, k, vt, n_batch, n_lat, n_ctx, heads):
    m = q.shape[0]
    hd = 2 * MLA_DN
    nq = n_lat // ATTN_TQ
    ctx_blk0 = (n_batch * n_lat) // n_ctx
    kl = pl.BlockSpec((n_lat, hd), lambda b, h, t: (b, h))
    vl = pl.BlockSpec((MLA_DV, n_lat), lambda b, h, t: (h, b))
    kc = pl.BlockSpec((n_ctx, hd), lambda b, h, t: (ctx_blk0 + b, h))
    vc = pl.BlockSpec((MLA_DV, n_ctx), lambda b, h, t: (h, ctx_blk0 + b))
    out = pl.pallas_call(
        _attn_lat_kernel,
        grid=(n_batch, heads, nq),
        in_specs=[pl.BlockSpec((ATTN_TQ, hd), lambda b, h, t: (b * nq + t, h)), kl, vl, kc, vc],
        out_specs=pl.BlockSpec((ATTN_TQ, MLA_DV), lambda b, h, t: (b * nq + t, h)),
        out_shape=jax.ShapeDtypeStruct((n_batch * n_lat, heads * MLA_DV), BF16),
        compiler_params=_cparams(("arbitrary",) * 3),
        name="mla_attn_latent",
    )(q, k, vt, k, vt)
    out_ctx = pl.pallas_call(
        _attn_ctx_kernel,
        grid=(n_batch, heads),
        in_specs=[
            pl.BlockSpec((n_ctx, hd), lambda b, h: (ctx_blk0 + b, h)),
            pl.BlockSpec((n_ctx, hd), lambda b, h: (ctx_blk0 + b, h)),
            pl.BlockSpec((MLA_DV, n_ctx), lambda b, h: (h, ctx_blk0 + b)),
        ],
        out_specs=pl.BlockSpec((n_ctx, MLA_DV), lambda b, h: (b, h)),
        out_shape=jax.ShapeDtypeStruct((n_batch * n_ctx, heads * MLA_DV), BF16),
        compiler_params=_cparams(("arbitrary",) * 2),
        name="mla_attn_ctx",
    )(q, k, vt)
    return out, out_ctx


def _sgu_kernel(u_ref, v_ref, gv_ref, ws_ref, bs_ref, wo_ref, x_ref, gate_ref, o_ref, z_scr):
    sd = v_ref.shape[1]
    gw = sd // SGU_GROUPS
    for ch in range(v_ref.shape[0] // SGU_CHUNK):
        rows = slice(ch * SGU_CHUNK, (ch + 1) * SGU_CHUNK)
        v = v_ref[rows, :].astype(F32)
        vn = (v * lax.rsqrt(jnp.mean(v * v, axis=-1, keepdims=True) + EPS) * gv_ref[...]).astype(BF16)
        for g in range(SGU_GROUPS):
            cols = slice(g * gw, (g + 1) * gw)
            z = jnp.dot(ws_ref[g], vn[:, cols], preferred_element_type=F32) + bs_ref[g]
            z_scr[rows, cols] = (u_ref[rows, cols].astype(F32) * z).astype(BF16)
    y = jnp.dot(z_scr[...], wo_ref[...], preferred_element_type=F32)
    o_ref[...] = x_ref[...] + gate_ref[...] * y


def _sgu(uv, g_v, w_s, b_s, w_out, x, mods, layer, geom):
    m, d = x.shape
    sd = w_out.shape[0]
    tpb, n_batch = geom
    tms = TM // 2
    scale = TM // tms
    gate_spec = pl.BlockSpec((None, None, 1, d),
                             lambda i: (layer * 6 + 2, jnp.minimum(i // (tpb * scale), n_batch), 0, 0))
    return pl.pallas_call(
        _sgu_kernel,
        grid=(m // tms,),
        in_specs=[
            pl.BlockSpec((tms, sd), lambda i: (i, 0)),
            pl.BlockSpec((tms, sd), lambda i: (i, 1)),
            pl.BlockSpec((1, sd), lambda i: (0, 0)),
            pl.BlockSpec((SGU_GROUPS, SGU_CHUNK, SGU_CHUNK), lambda i: (0, 0, 0)),
            pl.BlockSpec((SGU_GROUPS, SGU_CHUNK, 1), lambda i: (0, 0, 0)),
            pl.BlockSpec((sd, d), lambda i: (0, 0)),
            pl.BlockSpec((tms, d), lambda i: (i, 0)),
            gate_spec,
        ],
        out_specs=pl.BlockSpec((tms, d), lambda i: (i, 0)),
        out_shape=jax.ShapeDtypeStruct(x.shape, F32),
        scratch_shapes=[pltpu.VMEM((tms, sd), BF16)],
        input_output_aliases={6: 0},
        compiler_params=_cparams(("arbitrary",)),
        name="sgu",
    )(uv, uv, g_v.reshape(1, sd), w_s, b_s.reshape(SGU_GROUPS, SGU_CHUNK, 1), w_out, x, mods)


def _router_kernel(x_ref, g_ref, sh_ref, sc_ref, wr_ref, h_ref, idx_ref, wgt_ref, *, n_tiles):
    i = pl.program_id(0)

    @pl.when(i < n_tiles)
    def _():
        h = _modnorm(x_ref[...], g_ref, sh_ref, sc_ref)
        h_ref[...] = h.astype(h_ref.dtype)
        w = wr_ref[...]
        h_hi, w_hi = h.astype(BF16), w.astype(BF16)
        h_lo = (h - h_hi.astype(F32)).astype(BF16)
        w_lo = (w - w_hi.astype(F32)).astype(BF16)
        logits = (jnp.dot(h_hi, w_hi, preferred_element_type=F32)
                  + jnp.dot(h_lo, w_hi, preferred_element_type=F32)
                  + jnp.dot(h_hi, w_lo, preferred_element_type=F32))
        lane = lax.broadcasted_iota(jnp.int32, logits.shape, 1).astype(F32)
        neg = jnp.float32(-jnp.inf)
        big = jnp.float32(logits.shape[1])
        logits = jnp.where(lane < N_EXPERTS, logits, neg)
        m1 = jnp.max(logits, axis=-1, keepdims=True)
        i1 = jnp.min(jnp.where(logits == m1, lane, big), axis=-1, keepdims=True)
        rest = jnp.where(lane == i1, neg, logits)
        m2 = jnp.max(rest, axis=-1, keepdims=True)
        i2 = jnp.min(jnp.where(rest == m2, lane, big), axis=-1, keepdims=True)
        e2 = jnp.exp(m2 - m1)
        w1 = 1.0 / (1.0 + e2)
        w2 = e2 / (1.0 + e2)
        idx_ref[...] = jnp.where(lane == 0, i1, jnp.where(lane == 1, i2, 0.0)).astype(jnp.int32)
        wgt_ref[...] = jnp.where(lane == 0, w1, jnp.where(lane == 1, w2, 0.0))

    @pl.when(i >= n_tiles)
    def _():
        h_ref[...] = jnp.zeros_like(h_ref)


def _router(x, g, mods, layer, w_router_pad, *, n_rows, h_rows, geom):
    d = x.shape[1]
    tpb, n_batch = geom
    sh_spec, sc_spec = _mod_specs(layer, 3, tpb, n_batch)
    nt = n_rows // TM
    real = lambda i: jnp.minimum(i, nt - 1)
    one = lambda f: (lambda i: f(real(i), 0))
    lanes = w_router_pad.shape[1]
    return pl.pallas_call(
        functools.partial(_router_kernel, n_tiles=nt),
        grid=(h_rows // TM,),
        in_specs=[
            pl.BlockSpec((TM, d), lambda i: (real(i), 0)),
            pl.BlockSpec((1, d), lambda i: (0, 0)),
            pl.BlockSpec(sh_spec.block_shape, one(sh_spec.index_map)),
            pl.BlockSpec(sc_spec.block_shape, one(sc_spec.index_map)),
            pl.BlockSpec((d, lanes), lambda i: (0, 0)),
        ],
        out_specs=[
            pl.BlockSpec((TM, d), lambda i: (i, 0)),
            pl.BlockSpec((TM, lanes), lambda i: (real(i), 0)),
            pl.BlockSpec((TM, lanes), lambda i: (real(i), 0)),
        ],
        out_shape=[
            jax.ShapeDtypeStruct((h_rows, d), BF16),
            jax.ShapeDtypeStruct((n_rows, lanes), jnp.int32),
            jax.ShapeDtypeStruct((n_rows, lanes), F32),
        ],
        compiler_params=_cparams(("arbitrary",)),
        name="moe_router",
    )(x, g.reshape(1, d), mods, mods, w_router_pad)


def _moe_ffn_kernel(te_ref, tr_ref, tf_ref, x_ref, wg32_ref, wu32_ref, wd32_ref, o_ref, acc_ref,
                    wg_ref, wu_ref, wd_ref):
    del te_ref
    t = pl.program_id(0)
    f = pl.program_id(1)
    n_rows = tr_ref[t]
    half_rows = MOE_TM // 2

    @pl.when(f == 0)
    def _():
        acc_ref[...] = jnp.zeros_like(acc_ref)

    @pl.when(tf_ref[t] == 1)
    def _():
        wg_ref[f] = wg32_ref[...].astype(BF16)
        wu_ref[f] = wu32_ref[...].astype(BF16)
        wd_ref[f] = wd32_ref[...].astype(BF16)

    def matmul(c):
        x = x_ref[_chunk(c), :]
        return (jnp.dot(x, wg_ref[f], preferred_element_type=F32),
                jnp.dot(x, wu_ref[f], preferred_element_type=F32))

    def epilogue(c, gu):
        a = (_silu(gu[0]) * gu[1]).astype(BF16)
        acc_ref[_chunk(c), :] += jnp.dot(a, wd_ref[f], preferred_element_type=F32)

    @pl.when(n_rows > half_rows)
    def _():
        _pipelined_chunks(MOE_TM // ROW_CHUNK, matmul, epilogue)

    @pl.when((n_rows > 0) & (n_rows <= half_rows))
    def _():
        _pipelined_chunks(half_rows // ROW_CHUNK, matmul, epilogue)

    @pl.when(f == pl.num_programs(1) - 1)
    def _():
        o_ref[...] = acc_ref[...].astype(o_ref.dtype)


def _moe_ffn(xs, tile_expert, tile_rows, tile_first, w_gu, w_down, lf):
    r, d = xs.shape
    ff = w_down.shape[2]
    nf = ff // MOE_BF
    n_tiles = r // MOE_TM

    def fidx(t, f, tf):
        return jnp.where(tf[t] == 1, f, nf - 1)

    grid_spec = pltpu.PrefetchScalarGridSpec(
        num_scalar_prefetch=3,
        grid=(n_tiles, nf),
        in_specs=[
            pl.BlockSpec((MOE_TM, d), lambda t, f, te, tr, tf: (t, 0)),
            pl.BlockSpec((None, None, d, MOE_BF), lambda t, f, te, tr, tf: (lf, te[t], 0, fidx(t, f, tf))),
            pl.BlockSpec((None, None, d, MOE_BF),
                         lambda t, f, te, tr, tf: (lf, te[t], 0, nf + fidx(t, f, tf))),
            pl.BlockSpec((None, None, MOE_BF, d), lambda t, f, te, tr, tf: (lf, te[t], fidx(t, f, tf), 0)),
        ],
        out_specs=pl.BlockSpec((MOE_TM, d), lambda t, f, te, tr, tf: (t, 0)),
        scratch_shapes=[pltpu.VMEM((MOE_TM, d), F32), pltpu.VMEM((nf, d, MOE_BF), BF16),
                        pltpu.VMEM((nf, d, MOE_BF), BF16), pltpu.VMEM((nf, MOE_BF, d), BF16)],
    )
    return pl.pallas_call(
        _moe_ffn_kernel,
        grid_spec=grid_spec,
        out_shape=jax.ShapeDtypeStruct((r, d), BF16),
        compiler_params=_cparams(("arbitrary", "arbitrary")),
        name="moe_ffn",
    )(tile_expert, tile_rows, tile_first, xs, w_gu, w_gu, w_down)


def _combine_kernel(x_ref, y0_ref, y1_ref, wgt_ref, gate_ref, gf_ref, o_ref, *, final):
    w = wgt_ref[...]
    y = w[:, 0:1] * y0_ref[...].astype(F32) + w[:, 1:2] * y1_ref[...].astype(F32)
    xn = x_ref[...] + gate_ref[...] * y
    if final:
        xn = xn * lax.rsqrt(jnp.mean(xn * xn, axis=-1, keepdims=True) + EPS) * gf_ref[...]
    o_ref[...] = xn


def _combine(x, y0, y1, wgt, mods, layer, g_final, *, n_rows, geom, final):
    d = x.shape[1]
    tpb, n_batch = geom
    lanes = wgt.shape[1]
    gate_spec = pl.BlockSpec((None, None, 1, d),
                             lambda i: (layer * 6 + 5, jnp.minimum(i // tpb, n_batch), 0, 0))
    row = lambda w: pl.BlockSpec((TM, w), lambda i: (i, 0))
    out_rows = n_rows if final else x.shape[0]
    return pl.pallas_call(
        functools.partial(_combine_kernel, final=final),
        grid=(n_rows // TM,),
        in_specs=[row(d), row(d), row(d), row(lanes), gate_spec, pl.BlockSpec((1, d), lambda i: (0, 0))],
        out_specs=row(d),
        out_shape=jax.ShapeDtypeStruct((out_rows, d), F32),
        input_output_aliases={} if final else {0: 0},
        compiler_params=_cparams(("arbitrary",)),
        name="moe_combine",
    )(x, y0, y1, wgt, mods, g_final.reshape(1, d))


def _moe_layer(x, g, mods, layer, w_router, w_gu, w_down, lf, g_final, *, n_rows, geom, final):
    d = x.shape[1]
    wr_pad = jnp.zeros((d, 128), F32).at[:, :N_EXPERTS].set(w_router)
    n_assign = 2 * n_rows
    r_rows = n_assign + (N_EXPERTS - 1) * MOE_TM
    h, idx, wgt = _router(x, g, mods, layer, wr_pad, n_rows=n_rows, h_rows=r_rows, geom=geom)
    e_flat = idx[:, :2].reshape(-1)
    onehot = (e_flat[:, None] == jnp.arange(N_EXPERTS, dtype=jnp.int32)[None, :]).astype(F32)
    blk = onehot.reshape(n_assign // CUMSUM_BLOCK, CUMSUM_BLOCK, N_EXPERTS)
    tril = jnp.tril(jnp.ones((CUMSUM_BLOCK, CUMSUM_BLOCK), F32))
    within = jnp.einsum("ij,bjk->bik", tril, blk)
    blk_tot = within[:, -1, :]
    blk_off = jnp.cumsum(blk_tot, axis=0) - blk_tot
    csum = (within + blk_off[:, None, :]).reshape(n_assign, N_EXPERTS)
    rank = (jnp.sum(onehot * csum, axis=1) - 1.0).astype(jnp.int32)
    counts = jnp.sum(blk_tot, axis=0).astype(jnp.int32)
    padded = ((counts + MOE_TM - 1) // MOE_TM) * MOE_TM
    ends = jnp.cumsum(padded)
    starts = ends - padded
    dest = starts[e_flat] + rank
    n_tiles = r_rows // MOE_TM
    src = jnp.zeros((r_rows,), jnp.int32).at[dest].add(jnp.arange(n_assign, dtype=jnp.int32) // 2)
    tile_start = jnp.arange(n_tiles, dtype=jnp.int32) * MOE_TM
    last_tile = jnp.maximum(ends[-1] // MOE_TM - 1, 0)
    ts_used = jnp.minimum(tile_start, last_tile * MOE_TM)
    tile_expert = jnp.sum((ends[None, :] <= ts_used[:, None]).astype(jnp.int32), axis=1)
    tile_expert = jnp.minimum(tile_expert, N_EXPERTS - 1)
    real_end = (starts + counts)[tile_expert]
    used = tile_start < ends[-1]
    tile_rows = jnp.where(used, jnp.clip(real_end - tile_start, 0, MOE_TM), 0).astype(jnp.int32)
    tile_first = (used & (tile_start == starts[tile_expert])).astype(jnp.int32)
    xs = jnp.take(h, src, axis=0, mode="clip")
    ys = _moe_ffn(xs, tile_expert, tile_rows, tile_first, w_gu, w_down, lf)
    d2 = dest.reshape(n_rows, 2)
    y0 = jnp.take(ys, d2[:, 0], axis=0, mode="clip")
    y1 = jnp.take(ys, d2[:, 1], axis=0, mode="clip")
    return _combine(x, y0, y1, wgt, mods, layer, g_final, n_rows=n_rows, geom=geom, final=final)


def kernel(x, c, ctx, c_ctx, ada_w, ada_b, norm_mix_g, norm_ffn_g, final_norm_g, ret_w_in, ret_w_o, ret_decay_f, ret_decay_b, ret_gn_f, ret_gn_b, mla_w_down, mla_g_q, mla_g_kv, mla_w_uq, mla_w_ukv, mla_w_o, sgu_w_in, sgu_g_v, sgu_w_s, sgu_b_s, sgu_w_out, ffn_w_gu, ffn_w_down, moe_w_router, moe_w_gu, moe_w_down):
    n_batch, n_lat, d = x.shape
    n_ctx = ctx.shape[1]
    depth = ada_w.shape[0]
    rows_lat = n_batch * n_lat
    m = rows_lat + n_batch * n_ctx
    geom = (n_lat // TM, n_batch)
    assert n_batch * n_ctx == TM and d == 1024 and n_batch + 1 <= 8

    xt = (x.reshape(rows_lat, d), ctx.reshape(n_batch * n_ctx, d))
    cond = jnp.concatenate([c, c_ctx[None, :], jnp.zeros((8 - n_batch - 1, d), F32)], axis=0)
    mods = _adaln(cond, ada_w, ada_b)
    mods = mods.reshape(depth, 8, 6, d).transpose(0, 2, 1, 3).reshape(depth * 6, 8, 1, d)

    grid_rows = n_lat // GRID_W
    ret_cos, ret_sin = _rope_tables(grid_rows, n_batch * n_ctx, RET_DK)
    mla_cos, mla_sin = _rope_tables(grid_rows, n_batch * n_ctx, MLA_DR)
    pad_r = MLA_DN - MLA_DR
    mla_cos = jnp.concatenate([mla_cos, jnp.ones((mla_cos.shape[0], pad_r), F32)], axis=1)
    mla_sin = jnp.concatenate([mla_sin, jnp.zeros((mla_sin.shape[0], pad_r), F32)], axis=1)

    for i in range(depth):
        last = i == depth - 1
        kind, jm = i % N_MIXERS, i // N_MIXERS
        rows_out = rows_lat if last else m
        if kind == 0:
            qk, vv, gates = _ret_in(xt, norm_mix_g[i], mods, i, ret_w_in[jm].astype(BF16),
                                    (ret_cos, ret_sin), geom=geom)
            lg = jnp.stack([jax.nn.log_sigmoid(ret_decay_f[jm].astype(F32)),
                            jax.nn.log_sigmoid(ret_decay_b[jm].astype(F32))])
            hv = vv.shape[1]
            gn = jnp.stack([ret_gn_f[jm], ret_gn_b[jm]]).reshape(2, 1, hv)
            yf, yb = _retention(qk, vv, gates, lg, gn, n_batch, n_lat, n_ctx)
            xt = _outproj([yf, yb], ret_w_o[jm].astype(BF16), xt, mods, i, 2, n_rows=rows_out, geom=geom)
        elif kind == 1:
            heads = mla_w_o.shape[1] // MLA_DV
            lq, lkv = mla_g_q.shape[1], mla_g_kv.shape[1]
            n_dn = lq + lkv + MLA_DN
            wd = jnp.zeros((d, n_dn), F32).at[:, :lq + lkv + MLA_DR].set(mla_w_down[jm]).astype(BF16)
            dn = _proj(xt, norm_mix_g[i], mods, i, 0, wd, mode="f32", bn=n_dn, out_dtype=F32,
                       n_rows=m, geom=geom)
            wq = mla_w_uq[jm].reshape(lq, heads, MLA_DN + MLA_DR)
            wq = jnp.concatenate([wq, jnp.zeros((lq, heads, pad_r), F32)], axis=2)
            wq = wq.reshape(lq, heads * 2 * MLA_DN).astype(BF16)
            wkv = mla_w_ukv[jm].reshape(lkv, heads, MLA_DN + MLA_DV)
            wk = wkv[:, :, :MLA_DN].reshape(lkv, heads * MLA_DN).astype(BF16)
            wv = wkv[:, :, MLA_DN:].reshape(lkv, heads * MLA_DV).T.astype(BF16)
            tpb = geom[0]
            n_lat_tiles = tpb * n_batch
            pos_idx = lambda r: (jnp.where(r < n_lat_tiles, r % tpb, tpb + (r - n_lat_tiles)), 0)
            full = lambda a: pl.BlockSpec(a.shape, lambda r: (0,) * a.ndim)
            gq = mla_g_q[jm].reshape(1, lq)
            gkv = mla_g_kv[jm].reshape(1, lkv)
            scale = float((MLA_DN + MLA_DR) ** -0.5 * LOG2_E)
            q, k, vt = pl.pallas_call(
                functools.partial(_mla_up_kernel, heads=heads, scale=scale),
                grid=(m // TM,),
                in_specs=[pl.BlockSpec((TM, n_dn), lambda r: (r, 0)), full(gq), full(gkv), full(wq),
                          full(wk), full(wv), pl.BlockSpec((TM, MLA_DN), pos_idx),
                          pl.BlockSpec((TM, MLA_DN), pos_idx)],
                out_specs=[pl.BlockSpec((TM, heads * 2 * MLA_DN), lambda r: (r, 0)),
                           pl.BlockSpec((TM, heads * 2 * MLA_DN), lambda r: (r, 0)),
                           pl.BlockSpec((heads * MLA_DV, TM), lambda r: (0, r))],
                out_shape=[jax.ShapeDtypeStruct((m, heads * 2 * MLA_DN), BF16),
                           jax.ShapeDtypeStruct((m, heads * 2 * MLA_DN), BF16),
                           jax.ShapeDtypeStruct((heads * MLA_DV, m), BF16)],
                compiler_params=_cparams(("arbitrary",)),
                name="mla_up",
            )(dn, gq, gkv, wq, wk, wv, mla_cos, mla_sin)
            ao = _mla_attention(q, k, vt, n_batch, n_lat, n_ctx, heads)
            xt = _outproj([ao], mla_w_o[jm].astype(BF16), xt, mods, i, 2, n_rows=rows_out, geom=geom)
        else:
            uv = _proj(xt, norm_mix_g[i], mods, i, 0, sgu_w_in[jm].astype(BF16), mode="gelu", bn=2048,
                       out_dtype=BF16, n_rows=m, geom=geom)
            xt = _sgu(uv, sgu_g_v[jm], sgu_w_s[jm].astype(BF16), sgu_b_s[jm], sgu_w_out[jm].astype(BF16),
                      xt, mods, i, geom)
        f = i // 2
        if i % 2 == 0:
            dff = ffn_w_down.shape[1]
            wgu = ffn_w_gu[f].astype(BF16)
            xt = _dense_ffn(xt, norm_ffn_g[i], mods, i, wgu[:, :dff], wgu[:, dff:],
                            ffn_w_down[f].astype(BF16), n_rows=rows_out, geom=geom)
        else:
            xt = _moe_layer(xt, norm_ffn_g[i], mods, i, moe_w_router[f], moe_w_gu, moe_w_down, f,
                            final_norm_g, n_rows=rows_out, geom=geom, final=last)
    if not (depth - 1) % 2:
        raise NotImplementedError("final norm is fused into the expert combine of the last layer")
    return xt.reshape(n_batch, n_lat, d)
```

```python
import functools

import jax
import jax.numpy as jnp
from jax import lax
from jax.experimental import pallas as pl
from jax.experimental.pallas import tpu as pltpu

F32 = jnp.float32
BF16 = jnp.bfloat16
HIGHEST = lax.Precision.HIGHEST
LOG2_E = 1.4426950408889634

GRID_W = 64
ROPE_BASE = 10000.0
EPS = 1e-6
RET_DK = 256
RET_DV = 512
MLA_DN = 128
MLA_DR = 64
MLA_DV = 128
SGU_GROUPS = 8
SGU_CHUNK = 128
N_EXPERTS = 8
N_MIXERS = 3

TM = 1024
ROW_CHUNK = 256
RET_CHUNK = 256
RET_HEADS_PER_STEP = 2
RET_IN_STEPS = 4
ATTN_TQ = 512
ATTN_TK = 1024
ATTN_ONES_ROWS = 16
ATTN_MAX_LAG_LOG2 = 60.0
MOE_TM = 1024
MOE_BF = 512
CUMSUM_BLOCK = 256
VMEM_LIMIT = 56 * 1024 * 1024


def _cparams(sem):
    return pltpu.CompilerParams(dimension_semantics=sem, vmem_limit_bytes=VMEM_LIMIT)


def _silu(x):
    return x * jax.nn.sigmoid(x)


def _rope(x, cos, sin_signed, half):
    w = x.shape[-1]
    lane = lax.broadcasted_iota(jnp.int32, x.shape, 1)
    first = (lane % (2 * half)) < half
    rot = jnp.where(first, pltpu.roll(x, w - half, 1), pltpu.roll(x, half, 1))
    return x * cos + rot * sin_signed


def _rope_tables(rows, n_ctx_rows, rot_dim):
    row = jnp.repeat(jnp.arange(rows, dtype=F32), GRID_W)
    col = jnp.tile(jnp.arange(GRID_W, dtype=F32), rows)
    axis_dim = rot_dim // 2
    inv_freq = ROPE_BASE ** (-jnp.arange(0, axis_dim, 2, dtype=F32) / axis_dim)
    ang_r = row[:, None] * inv_freq
    ang_c = col[:, None] * inv_freq
    ang = jnp.concatenate([ang_r, ang_r, ang_c, ang_c], axis=-1)
    quarter = rot_dim // 4
    sign = jnp.where((jnp.arange(rot_dim) % (2 * quarter)) < quarter, -1.0, 1.0).astype(F32)
    cos = jnp.concatenate([jnp.cos(ang), jnp.ones((n_ctx_rows, rot_dim), F32)], axis=0)
    sin = jnp.concatenate([jnp.sin(ang) * sign, jnp.zeros((n_ctx_rows, rot_dim), F32)], axis=0)
    return cos, sin


def _adaln_kernel(cond_ref, w_ref, b_ref, o_ref):
    s = _silu(cond_ref[...])
    o_ref[...] = jnp.dot(s, w_ref[...], precision=HIGHEST, preferred_element_type=F32) + b_ref[...]


def _adaln(cond, ada_w, ada_b):
    depth, d, n6 = ada_w.shape
    bn = n6 // 4
    return pl.pallas_call(
        _adaln_kernel,
        grid=(depth, n6 // bn),
        in_specs=[
            pl.BlockSpec((8, d), lambda l, j: (0, 0)),
            pl.BlockSpec((None, d, bn), lambda l, j: (l, 0, j)),
            pl.BlockSpec((None, 1, bn), lambda l, j: (l, 0, j)),
        ],
        out_specs=pl.BlockSpec((None, 8, bn), lambda l, j: (l, 0, j)),
        out_shape=jax.ShapeDtypeStruct((depth, 8, n6), F32),
        compiler_params=_cparams(("arbitrary", "arbitrary")),
        name="adaln",
    )(cond, ada_w, ada_b.reshape(depth, 1, n6))


def _row_specs(src, width, n_main_tiles):
    if isinstance(src, tuple):
        specs = [pl.BlockSpec((TM, width), lambda i, *_: (jnp.minimum(i, n_main_tiles - 1), 0)),
                 pl.BlockSpec((TM, width), lambda i, *_: (jnp.maximum(i - n_main_tiles, 0), 0))]
        return specs, list(src)
    return [pl.BlockSpec((TM, width), lambda i, *_: (i, 0))], [src]


def _row_tile(refs, i, n_main_tiles):
    if len(refs) == 2:
        return jnp.where(i < n_main_tiles, refs[0][...], refs[1][...])
    return refs[0][...]


def _modnorm(xf, g_ref, shift_ref, scale_ref):
    y = xf * lax.rsqrt(jnp.mean(xf * xf, axis=-1, keepdims=True) + EPS)
    return (y * g_ref[...]) * (1.0 + scale_ref[...]) + shift_ref[...]


def _chunk(c):
    return slice(c * ROW_CHUNK, (c + 1) * ROW_CHUNK)


def _pipelined_chunks(n_chunks, matmul, epilogue):
    pending = matmul(0)
    for c in range(n_chunks):
        following = matmul(c + 1) if c + 1 < n_chunks else None
        epilogue(c, pending)
        pending = following


def _ret_in_kernel(*refs, n_q, x_count, n_main_tiles):
    x_refs = refs[:x_count]
    (g_ref, sh_ref, sc_ref, wqk_ref, wv_ref, wg_ref, cos_ref, sin_ref,
     qk_ref, v_ref, gate_ref, h_scr) = refs[x_count:]
    j = pl.program_id(1)

    @pl.when(j == 0)
    def _():
        xf = _row_tile(x_refs, pl.program_id(0), n_main_tiles)
        h_scr[...] = _modnorm(xf, g_ref, sh_ref, sc_ref).astype(BF16)

    kscale = jnp.where(j >= n_q, RET_DK ** -0.5, 1.0).astype(F32)

    def matmul(c):
        h = h_scr[_chunk(c), :]
        return tuple(jnp.dot(h, w[...], preferred_element_type=F32) for w in (wqk_ref, wv_ref, wg_ref))

    def epilogue(c, ys):
        r = _chunk(c)
        for hs in range(ys[0].shape[1] // RET_DK):
            cols = slice(hs * RET_DK, (hs + 1) * RET_DK)
            qk = _rope(ys[0][:, cols], cos_ref[r, :], sin_ref[r, :], RET_DK // 4)
            qk_ref[r, cols] = (qk * kscale).astype(qk_ref.dtype)
        v_ref[r, :] = ys[1].astype(v_ref.dtype)
        gate_ref[r, :] = _silu(ys[2]).astype(gate_ref.dtype)

    _pipelined_chunks(h_scr.shape[0] // ROW_CHUNK, matmul, epilogue)


def _ret_in(x, g, mods, layer, w_in, rope, *, geom):
    x_parts = list(x) if isinstance(x, tuple) else [x]
    m = sum(p.shape[0] for p in x_parts)
    d = x_parts[0].shape[1]
    heads = d // RET_DK
    nj = RET_IN_STEPS
    n_q = nj // 2
    qk_cols = (2 * heads * RET_DK) // nj
    wv_cols = (heads * RET_DV) // nj
    wg_cols = (2 * heads * RET_DV) // nj
    assert wv_cols == qk_cols and wg_cols == 2 * qk_cols and qk_cols % RET_DK == 0
    tpb, n_batch = geom
    sh_spec, sc_spec = _mod_specs(layer, 0, tpb, n_batch)
    n_lat_tiles = tpb * n_batch
    cos, sin = rope

    def pos_idx(i, j):
        return (jnp.where(i < n_lat_tiles, i % tpb, tpb + (i - n_lat_tiles)), 0)

    x_specs, x_arrs = _row_specs(x, d, n_lat_tiles)
    return pl.pallas_call(
        functools.partial(_ret_in_kernel, n_q=n_q, x_count=len(x_arrs), n_main_tiles=n_lat_tiles),
        grid=(m // TM, nj),
        in_specs=x_specs + [
            pl.BlockSpec((1, d), lambda i, j: (0, 0)),
            sh_spec, sc_spec,
            pl.BlockSpec((d, qk_cols), lambda i, j: (0, j)),
            pl.BlockSpec((d, qk_cols), lambda i, j: (0, nj + j)),
            pl.BlockSpec((d, 2 * qk_cols), lambda i, j: (0, nj + j)),
            pl.BlockSpec((TM, RET_DK), pos_idx),
            pl.BlockSpec((TM, RET_DK), pos_idx),
        ],
        out_specs=[
            pl.BlockSpec((TM, qk_cols), lambda i, j: (i, j)),
            pl.BlockSpec((TM, qk_cols), lambda i, j: (i, j)),
            pl.BlockSpec((TM, 2 * qk_cols), lambda i, j: (i, j)),
        ],
        out_shape=[
            jax.ShapeDtypeStruct((m, 2 * heads * RET_DK), BF16),
            jax.ShapeDtypeStruct((m, heads * RET_DV), BF16),
            jax.ShapeDtypeStruct((m, 2 * heads * RET_DV), BF16),
        ],
        scratch_shapes=[pltpu.VMEM((TM, d), BF16)],
        compiler_params=_cparams(("arbitrary", "arbitrary")),
        name="ret_in",
    )(*x_arrs, g.reshape(1, d), mods, mods, w_in, w_in, w_in, cos, sin)


def _proj_kernel(x_ref, g_ref, sh_ref, sc_ref, w_ref, o_ref, h_scr, *, mode):
    j = pl.program_id(1)

    @pl.when(j == 0)
    def _():
        h_scr[...] = _modnorm(x_ref[...], g_ref, sh_ref, sc_ref).astype(BF16)

    def matmul(c):
        return jnp.dot(h_scr[_chunk(c), :], w_ref[...], preferred_element_type=F32)

    def epilogue(c, y):
        if mode == "f32":
            o_ref[_chunk(c), :] = y
        elif mode == "gelu":
            o_ref[_chunk(c), :] = jax.nn.gelu(y).astype(o_ref.dtype)

    _pipelined_chunks(h_scr.shape[0] // ROW_CHUNK, matmul, epilogue)


def _mod_specs(layer, k_shift, tiles_per_batch, n_batch):
    def spec(k):
        return pl.BlockSpec((None, None, 1, 1024),
                            lambda i, j: (layer * 6 + k, jnp.minimum(i // tiles_per_batch, n_batch), 0, 0))
    return spec(k_shift), spec(k_shift + 1)


def _proj(x, g, mods, layer, k_shift, w, *, mode, bn, out_dtype, n_rows, geom):
    d = x.shape[1]
    n = w.shape[1]
    nt = n_rows // TM
    tpb, n_batch = geom
    sh_spec, sc_spec = _mod_specs(layer, k_shift, tpb, n_batch)
    in_specs = [
        pl.BlockSpec((TM, d), lambda i, j: (i, 0)),
        pl.BlockSpec((1, d), lambda i, j: (0, 0)),
        sh_spec, sc_spec,
        pl.BlockSpec((d, bn), lambda i, j: (0, j)),
    ]
    args = [x, g.reshape(1, d), mods, mods, w]
    return pl.pallas_call(
        functools.partial(_proj_kernel, mode=mode),
        grid=(nt, n // bn),
        in_specs=in_specs,
        out_specs=pl.BlockSpec((TM, bn), lambda i, j: (i, j)),
        out_shape=jax.ShapeDtypeStruct((n_rows, n), out_dtype),
        scratch_shapes=[pltpu.VMEM((TM, d), BF16)],
        compiler_params=_cparams(("arbitrary", "arbitrary")),
        name="proj_" + mode,
    )(*args)


def _dense_ffn_kernel(x_ref, g_ref, sh_ref, sc_ref, gate_ref, wg_ref, wu_ref, wd_ref, o_ref, h_scr):
    h_scr[...] = _modnorm(x_ref[...], g_ref, sh_ref, sc_ref).astype(BF16)

    def matmul(c):
        h = h_scr[_chunk(c), :]
        return (jnp.dot(h, wg_ref[...], preferred_element_type=F32),
                jnp.dot(h, wu_ref[...], preferred_element_type=F32))

    def epilogue(c, gu):
        a = (_silu(gu[0]) * gu[1]).astype(BF16)
        y = jnp.dot(a, wd_ref[...], preferred_element_type=F32)
        o_ref[_chunk(c), :] = x_ref[_chunk(c), :] + gate_ref[...] * y

    _pipelined_chunks(h_scr.shape[0] // ROW_CHUNK, matmul, epilogue)


def _dense_ffn(x, g, mods, layer, w_g, w_u, w_d, *, n_rows, geom):
    d = x.shape[1]
    dff = w_d.shape[0]
    tpb, n_batch = geom

    def mod_spec(k):
        return pl.BlockSpec((None, None, 1, d),
                            lambda i: (layer * 6 + k, jnp.minimum(i // tpb, n_batch), 0, 0))

    def resident(shape):
        return pl.BlockSpec(shape, lambda i: (0, 0), pipeline_mode=pl.Buffered(1))

    return pl.pallas_call(
        _dense_ffn_kernel,
        grid=(n_rows // TM,),
        in_specs=[
            pl.BlockSpec((TM, d), lambda i: (i, 0)),
            pl.BlockSpec((1, d), lambda i: (0, 0)),
            mod_spec(3), mod_spec(4), mod_spec(5),
            resident((d, dff)), resident((d, dff)), resident((dff, d)),
        ],
        out_specs=pl.BlockSpec((TM, d), lambda i: (i, 0)),
        out_shape=jax.ShapeDtypeStruct(x.shape, F32),
        scratch_shapes=[pltpu.VMEM((TM, d), BF16)],
        input_output_aliases={0: 0},
        compiler_params=_cparams(("arbitrary",)),
        name="dense_ffn",
    )(x, g.reshape(1, d), mods, mods, mods, w_g, w_u, w_d)


def _outproj_kernel(*refs, a_counts, x_count, n_main_tiles):
    i = pl.program_id(0)
    refs = list(refs)
    a_tiles = []
    for cnt in a_counts:
        a_tiles.append(_row_tile(refs[:cnt], i, n_main_tiles))
        refs = refs[cnt:]
    w_ref = refs[0]
    x = _row_tile(refs[1:1 + x_count], i, n_main_tiles)
    gate_ref, o_ref = refs[1 + x_count:]
    if len(a_tiles) == 1:
        a = a_tiles[0]
    else:
        a = (a_tiles[0].astype(F32) + a_tiles[1].astype(F32)).astype(BF16)
    y = jnp.dot(a, w_ref[...], preferred_element_type=F32)
    o_ref[...] = x + gate_ref[...] * y


def _outproj(a_list, w, x, mods, layer, k_gate, *, n_rows, geom):
    k, d = w.shape
    nt = n_rows // TM
    tpb, n_batch = geom
    n_main = tpb * n_batch
    gate_spec = pl.BlockSpec((None, None, 1, d),
                             lambda i: (layer * 6 + k_gate, jnp.minimum(i // tpb, n_batch), 0, 0))
    in_specs, args, a_counts = [], [], []
    for a in a_list:
        specs, arrs = _row_specs(a, k, n_main)
        in_specs += specs
        args += arrs
        a_counts.append(len(arrs))
    x_specs, x_arrs = _row_specs(x, d, n_main)
    total_rows = sum(xa.shape[0] for xa in x_arrs)
    return pl.pallas_call(
        functools.partial(_outproj_kernel, a_counts=tuple(a_counts), x_count=len(x_arrs), n_main_tiles=n_main),
        grid=(nt,),
        in_specs=in_specs + [pl.BlockSpec((k, d), lambda i: (0, 0))] + x_specs + [gate_spec],
        out_specs=pl.BlockSpec((TM, d), lambda i: (i, 0)),
        out_shape=jax.ShapeDtypeStruct((total_rows, d), F32),
        input_output_aliases={len(args) + 1: 0} if len(x_arrs) == 1 else {},
        compiler_params=_cparams(("arbitrary",)),
        name="outproj",
    )(*args, w, *x_arrs, mods)


def _ret_kernel(lg_ref, qf_ref, kf_ref, vf_ref, gf_ref, qb_ref, kb_ref, vb_ref, gb_ref, gnf_ref, gnb_ref,
                of_ref, ob_ref, s_scr, intra_scr, dec_scr):
    hp = pl.program_id(1)
    s = pl.program_id(2)
    c = RET_CHUNK
    nh = RET_HEADS_PER_STEP
    chains = [(d, j) for d in range(2) for j in range(nh)]

    @pl.when(s == 0)
    def _():
        s_scr[...] = jnp.zeros_like(s_scr)
        ii = lax.broadcasted_iota(jnp.int32, (c, c), 0)
        jj = lax.broadcasted_iota(jnp.int32, (c, c), 1)
        pos = lax.broadcasted_iota(jnp.int32, (c, 1), 0).astype(F32)
        for n, (d, j) in enumerate(chains):
            lg = lg_ref[d, hp * nh + j]
            diff = ((ii - jj) if d == 0 else (jj - ii)).astype(F32)
            intra_scr[n] = jnp.where(diff >= 0, jnp.exp(lg * jnp.maximum(diff, 0.0)), 0.0)
            fpos = pos if d == 0 else (c - 1.0) - pos
            dec_scr[2 * n] = jnp.exp(lg * (fpos + 1.0))
            dec_scr[2 * n + 1] = jnp.exp(lg * ((c - 1.0) - fpos))

    refs = ((qf_ref, kf_ref, vf_ref, gf_ref, gnf_ref, of_ref), (qb_ref, kb_ref, vb_ref, gb_ref, gnb_ref, ob_ref))
    qcols = [slice(j * RET_DK, (j + 1) * RET_DK) for _, j in chains]
    vcols = [slice(j * RET_DV, (j + 1) * RET_DV) for _, j in chains]
    n_chains = range(len(chains))
    q = [refs[d][0][:, qcols[n]] for n, (d, _) in enumerate(chains)]
    k = [refs[d][1][:, qcols[n]] for n, (d, _) in enumerate(chains)]
    v = [refs[d][2][:, vcols[n]] for n, (d, _) in enumerate(chains)]
    contract_last = (((1,), (1,)), ((), ()))
    contract_rows = (((0,), (0,)), ((), ()))
    scores = [lax.dot_general(q[n], k[n], contract_last, preferred_element_type=F32) for n in n_chains]
    masked = [(scores[n] * intra_scr[n]).astype(BF16) for n in n_chains]
    state = [s_scr[n] for n in n_chains]
    qd = [(q[n].astype(F32) * dec_scr[2 * n]).astype(BF16) for n in n_chains]
    o = [jnp.dot(masked[n], v[n], preferred_element_type=F32)
         + jnp.dot(qd[n], state[n].astype(BF16), preferred_element_type=F32) for n in n_chains]
    kd = [(k[n].astype(F32) * dec_scr[2 * n + 1]).astype(BF16) for n in n_chains]
    for n, (d, j) in enumerate(chains):
        chunk_decay = jnp.exp(lg_ref[d, hp * nh + j] * float(c))
        s_scr[n] = state[n] * chunk_decay + lax.dot_general(kd[n], v[n], contract_rows,
                                                           preferred_element_type=F32)
    for n, (d, _) in enumerate(chains):
        gate_ref, gn_ref, o_ref = refs[d][3:]
        mu = jnp.mean(o[n], axis=-1, keepdims=True)
        oc = o[n] - mu
        var = jnp.mean(oc * oc, axis=-1, keepdims=True)
        y = oc * lax.rsqrt(var + EPS) * gn_ref[:, vcols[n]]
        o_ref[:, vcols[n]] = (gate_ref[:, vcols[n]].astype(F32) * y).astype(o_ref.dtype)


def _retention(qk, v, gates, lg, gn, n_batch, n_lat, n_ctx):
    m = qk.shape[0]
    heads = v.shape[1] // RET_DV
    c = RET_CHUNK
    lat_chunks = n_lat // c
    assert n_ctx == c
    steps = 1 + lat_chunks
    ctx_blk0 = (n_batch * n_lat) // c

    def row_blk(d, b, s):
        t = (s - 1) if d == 0 else (lat_chunks - s)
        return jnp.where(s == 0, ctx_blk0 + b, b * lat_chunks + t)

    nh = RET_HEADS_PER_STEP
    groups = heads // nh
    qw, vw = nh * RET_DK, nh * RET_DV

    def specs(d):
        return [
            pl.BlockSpec((c, qw), lambda b, h, s: (row_blk(d, b, s), h)),
            pl.BlockSpec((c, qw), lambda b, h, s: (row_blk(d, b, s), groups + h)),
            pl.BlockSpec((c, vw), lambda b, h, s: (row_blk(d, b, s), h)),
            pl.BlockSpec((c, vw), lambda b, h, s: (row_blk(d, b, s), groups * d + h)),
        ]

    def gn_spec(d):
        return pl.BlockSpec((None, 1, vw), lambda b, h, s: (d, 0, h))

    shape = jax.ShapeDtypeStruct((m, heads * RET_DV), BF16)
    yf, yb = pl.pallas_call(
        _ret_kernel,
        grid=(n_batch, groups, steps),
        in_specs=[pl.BlockSpec(memory_space=pltpu.SMEM)] + specs(0) + specs(1) + [gn_spec(0), gn_spec(1)],
        out_specs=[pl.BlockSpec((c, vw), lambda b, h, s: (row_blk(0, b, s), h)),
                   pl.BlockSpec((c, vw), lambda b, h, s: (row_blk(1, b, s), h))],
        out_shape=[shape, shape],
        scratch_shapes=[pltpu.VMEM((2 * nh, RET_DK, RET_DV), F32), pltpu.VMEM((2 * nh, c, c), F32),
                        pltpu.VMEM((4 * nh, c, 1), F32)],
        compiler_params=_cparams(("arbitrary",) * 3),
        name="retention",
    )(lg, qk, qk, v, gates, qk, qk, v, gates, gn, gn)
    return yf, yb


def _mla_up_kernel(dn_ref, gq_ref, gkv_ref, wq_ref, wk_ref, wv_ref, cos_ref, sin_ref,
                   q_ref, k_ref, v_ref, *, heads, scale):
    dn = dn_ref[...]
    lq = gq_ref.shape[1]
    lkv = gkv_ref.shape[1]

    def rms(a, g):
        return (a * lax.rsqrt(jnp.mean(a * a, axis=-1, keepdims=True) + EPS) * g).astype(BF16)

    cq = rms(dn[:, :lq], gq_ref[...])
    ckv = rms(dn[:, lq:lq + lkv], gkv_ref[...])
    cos = cos_ref[...]
    sin = sin_ref[...]
    half = MLA_DR // 4
    q = jnp.dot(cq, wq_ref[...], preferred_element_type=F32)
    kn = jnp.dot(ckv, wk_ref[...], preferred_element_type=F32)
    v_ref[...] = lax.dot_general(wv_ref[...], ckv, (((1,), (1,)), ((), ())),
                                 preferred_element_type=F32).astype(v_ref.dtype)
    kr = _rope(dn[:, lq + lkv:], cos, sin, half).astype(k_ref.dtype)
    for h in range(heads):
        o = h * 2 * MLA_DN
        q_ref[:, o:o + MLA_DN] = (q[:, o:o + MLA_DN] * scale).astype(q_ref.dtype)
        qr = _rope(q[:, o + MLA_DN:o + 2 * MLA_DN], cos, sin, half)
        q_ref[:, o + MLA_DN:o + 2 * MLA_DN] = (qr * scale).astype(q_ref.dtype)
        k_ref[:, o:o + MLA_DN] = kn[:, h * MLA_DN:(h + 1) * MLA_DN].astype(k_ref.dtype)
        k_ref[:, o + MLA_DN:o + 2 * MLA_DN] = kr


def _attn_lat_kernel(q_ref, kl_ref, vtl_ref, kc_ref, vtc_ref, o_ref):
    q = q_ref[...]
    tq = q.shape[0]
    dv = vtl_ref.shape[0]
    n_chunks = kl_ref.shape[0] // ATTN_TK

    def scores(k):
        return lax.dot_general(k, q, (((1,), (1,)), ((), ())), preferred_element_type=F32)

    def weighted(vt, p):
        lhs = jnp.concatenate([vt, jnp.ones((ATTN_ONES_ROWS, vt.shape[1]), BF16)], axis=0)
        return jnp.dot(lhs, p, preferred_element_type=F32)

    def exact_max_block(s, vt, m, acc, risk):
        m_new = jnp.maximum(m, jnp.max(s, axis=0, keepdims=True))
        p = jnp.exp2(s - m_new).astype(BF16)
        return m_new, jnp.exp2(m - m_new) * acc + weighted(vt, p), risk

    def lagged_max_block(s, vt, m, acc, risk):
        p = jnp.exp2(s - m).astype(BF16)
        bmax = jnp.max(s, axis=0, keepdims=True)
        m_new = jnp.maximum(m, bmax)
        return m_new, (acc + weighted(vt, p)) * jnp.exp2(m - m_new), jnp.maximum(risk, bmax - m)

    def run(later_block):
        state = (jnp.full((1, tq), -jnp.inf, F32), jnp.zeros((dv + ATTN_ONES_ROWS, tq), F32),
                 jnp.zeros((1, tq), F32))
        s_cur = scores(kc_ref[...])
        vt_cur = vtc_ref[...]
        for c in range(n_chunks):
            r = slice(c * ATTN_TK, (c + 1) * ATTN_TK)
            s_next = scores(kl_ref[r, :])
            state = (exact_max_block if c == 0 else later_block)(s_cur, vt_cur, *state)
            s_cur, vt_cur = s_next, vtl_ref[:, r]
        _, acc, risk = later_block(s_cur, vt_cur, *state)
        o_t = acc[:dv] / acc[dv:dv + 1]
        o_ref[...] = o_t.T.astype(o_ref.dtype)
        return risk

    risk = run(lagged_max_block)

    @pl.when(jnp.max(risk) > ATTN_MAX_LAG_LOG2)
    def _():
        run(exact_max_block)


def _attn_ctx_kernel(q_ref, kc_ref, vtc_ref, o_ref):
    s = lax.dot_general(q_ref[...], kc_ref[...], (((1,), (1,)), ((), ())), preferred_element_type=F32)
    p = jnp.exp2(s - jnp.max(s, axis=-1, keepdims=True))
    l = jnp.sum(p, axis=-1, keepdims=True)
    o = lax.dot_general(p.astype(BF16), vtc_ref[...], (((1,), (1,)), ((), ())), preferred_element_type=F32)
    o_ref[...] = (o / l).astype(o_ref.dtype)


def _mla_attention(q, k, vt, n_batch, n_lat, n_ctx, heads):
    m = q.shape[0]
    hd = 2 * MLA_DN
    nq = n_lat // ATTN_TQ
    ctx_blk0 = (n_batch * n_lat) // n_ctx
    kl = pl.BlockSpec((n_lat, hd), lambda b, h, t: (b, h))
    vl = pl.BlockSpec((MLA_DV, n_lat), lambda b, h, t: (h, b))
    kc = pl.BlockSpec((n_ctx, hd), lambda b, h, t: (ctx_blk0 + b, h))
    vc = pl.BlockSpec((MLA_DV, n_ctx), lambda b, h, t: (h, ctx_blk0 + b))
    out = pl.pallas_call(
        _attn_lat_kernel,
        grid=(n_batch, heads, nq),
        in_specs=[pl.BlockSpec((ATTN_TQ, hd), lambda b, h, t: (b * nq + t, h)), kl, vl, kc, vc],
        out_specs=pl.BlockSpec((ATTN_TQ, MLA_DV), lambda b, h, t: (b * nq + t, h)),
        out_shape=jax.ShapeDtypeStruct((n_batch * n_lat, heads * MLA_DV), BF16),
        compiler_params=_cparams(("arbitrary",) * 3),
        name="mla_attn_latent",
    )(q, k, vt, k, vt)
    out_ctx = pl.pallas_call(
        _attn_ctx_kernel,
        grid=(n_batch, heads),
        in_specs=[
            pl.BlockSpec((n_ctx, hd), lambda b, h: (ctx_blk0 + b, h)),
            pl.BlockSpec((n_ctx, hd), lambda b, h: (ctx_blk0 + b, h)),
            pl.BlockSpec((MLA_DV, n_ctx), lambda b, h: (h, ctx_blk0 + b)),
        ],
        out_specs=pl.BlockSpec((n_ctx, MLA_DV), lambda b, h: (b, h)),
        out_shape=jax.ShapeDtypeStruct((n_batch * n_ctx, heads * MLA_DV), BF16),
        compiler_params=_cparams(("arbitrary",) * 2),
        name="mla_attn_ctx",
    )(q, k, vt)
    return out, out_ctx


def _sgu_kernel(u_ref, v_ref, gv_ref, ws_ref, bs_ref, wo_ref, x_ref, gate_ref, o_ref, z_scr):
    sd = v_ref.shape[1]
    gw = sd // SGU_GROUPS
    for ch in range(v_ref.shape[0] // SGU_CHUNK):
        rows = slice(ch * SGU_CHUNK, (ch + 1) * SGU_CHUNK)
        v = v_ref[rows, :].astype(F32)
        vn = (v * lax.rsqrt(jnp.mean(v * v, axis=-1, keepdims=True) + EPS) * gv_ref[...]).astype(BF16)
        for g in range(SGU_GROUPS):
            cols = slice(g * gw, (g + 1) * gw)
            z = jnp.dot(ws_ref[g], vn[:, cols], preferred_element_type=F32) + bs_ref[g]
            z_scr[rows, cols] = (u_ref[rows, cols].astype(F32) * z).astype(BF16)
    y = jnp.dot(z_scr[...], wo_ref[...], preferred_element_type=F32)
    o_ref[...] = x_ref[...] + gate_ref[...] * y


def _sgu(uv, g_v, w_s, b_s, w_out, x, mods, layer, geom):
    m, d = x.shape
    sd = w_out.shape[0]
    tpb, n_batch = geom
    tms = TM // 2
    scale = TM // tms
    gate_spec = pl.BlockSpec((None, None, 1, d),
                             lambda i: (layer * 6 + 2, jnp.minimum(i // (tpb * scale), n_batch), 0, 0))
    return pl.pallas_call(
        _sgu_kernel,
        grid=(m // tms,),
        in_specs=[
            pl.BlockSpec((tms, sd), lambda i: (i, 0)),
            pl.BlockSpec((tms, sd), lambda i: (i, 1)),
            pl.BlockSpec((1, sd), lambda i: (0, 0)),
            pl.BlockSpec((SGU_GROUPS, SGU_CHUNK, SGU_CHUNK), lambda i: (0, 0, 0)),
            pl.BlockSpec((SGU_GROUPS, SGU_CHUNK, 1), lambda i: (0, 0, 0)),
            pl.BlockSpec((sd, d), lambda i: (0, 0)),
            pl.BlockSpec((tms, d), lambda i: (i, 0)),
            gate_spec,
        ],
        out_specs=pl.BlockSpec((tms, d), lambda i: (i, 0)),
        out_shape=jax.ShapeDtypeStruct(x.shape, F32),
        scratch_shapes=[pltpu.VMEM((tms, sd), BF16)],
        input_output_aliases={6: 0},
        compiler_params=_cparams(("arbitrary",)),
        name="sgu",
    )(uv, uv, g_v.reshape(1, sd), w_s, b_s.reshape(SGU_GROUPS, SGU_CHUNK, 1), w_out, x, mods)


def _router_kernel(x_ref, g_ref, sh_ref, sc_ref, wr_ref, h_ref, idx_ref, wgt_ref, *, n_tiles):
    i = pl.program_id(0)

    @pl.when(i < n_tiles)
    def _():
        h = _modnorm(x_ref[...], g_ref, sh_ref, sc_ref)
        h_ref[...] = h.astype(h_ref.dtype)
        w = wr_ref[...]
        h_hi, w_hi = h.astype(BF16), w.astype(BF16)
        h_lo = (h - h_hi.astype(F32)).astype(BF16)
        w_lo = (w - w_hi.astype(F32)).astype(BF16)
        logits = (jnp.dot(h_hi, w_hi, preferred_element_type=F32)
                  + jnp.dot(h_lo, w_hi, preferred_element_type=F32)
                  + jnp.dot(h_hi, w_lo, preferred_element_type=F32))
        lane = lax.broadcasted_iota(jnp.int32, logits.shape, 1).astype(F32)
        neg = jnp.float32(-jnp.inf)
        big = jnp.float32(logits.shape[1])
        logits = jnp.where(lane < N_EXPERTS, logits, neg)
        m1 = jnp.max(logits, axis=-1, keepdims=True)
        i1 = jnp.min(jnp.where(logits == m1, lane, big), axis=-1, keepdims=True)
        rest = jnp.where(lane == i1, neg, logits)
        m2 = jnp.max(rest, axis=-1, keepdims=True)
        i2 = jnp.min(jnp.where(rest == m2, lane, big), axis=-1, keepdims=True)
        e2 = jnp.exp(m2 - m1)
        w1 = 1.0 / (1.0 + e2)
        w2 = e2 / (1.0 + e2)
        idx_ref[...] = jnp.where(lane == 0, i1, jnp.where(lane == 1, i2, 0.0)).astype(jnp.int32)
        wgt_ref[...] = jnp.where(lane == 0, w1, jnp.where(lane == 1, w2, 0.0))

    @pl.when(i >= n_tiles)
    def _():
        h_ref[...] = jnp.zeros_like(h_ref)


def _router(x, g, mods, layer, w_router_pad, *, n_rows, h_rows, geom):
    d = x.shape[1]
    tpb, n_batch = geom
    sh_spec, sc_spec = _mod_specs(layer, 3, tpb, n_batch)
    nt = n_rows // TM
    real = lambda i: jnp.minimum(i, nt - 1)
    one = lambda f: (lambda i: f(real(i), 0))
    lanes = w_router_pad.shape[1]
    return pl.pallas_call(
        functools.partial(_router_kernel, n_tiles=nt),
        grid=(h_rows // TM,),
        in_specs=[
            pl.BlockSpec((TM, d), lambda i: (real(i), 0)),
            pl.BlockSpec((1, d), lambda i: (0, 0)),
            pl.BlockSpec(sh_spec.block_shape, one(sh_spec.index_map)),
            pl.BlockSpec(sc_spec.block_shape, one(sc_spec.index_map)),
            pl.BlockSpec((d, lanes), lambda i: (0, 0)),
        ],
        out_specs=[
            pl.BlockSpec((TM, d), lambda i: (i, 0)),
            pl.BlockSpec((TM, lanes), lambda i: (real(i), 0)),
            pl.BlockSpec((TM, lanes), lambda i: (real(i), 0)),
        ],
        out_shape=[
            jax.ShapeDtypeStruct((h_rows, d), BF16),
            jax.ShapeDtypeStruct((n_rows, lanes), jnp.int32),
            jax.ShapeDtypeStruct((n_rows, lanes), F32),
        ],
        compiler_params=_cparams(("arbitrary",)),
        name="moe_router",
    )(x, g.reshape(1, d), mods, mods, w_router_pad)


def _moe_ffn_kernel(te_ref, tr_ref, tf_ref, x_ref, wg32_ref, wu32_ref, wd32_ref, o_ref, acc_ref,
                    wg_ref, wu_ref, wd_ref):
    del te_ref
    t = pl.program_id(0)
    f = pl.program_id(1)
    n_rows = tr_ref[t]

    @pl.when(f == 0)
    def _():
        acc_ref[...] = jnp.zeros_like(acc_ref)

    @pl.when(tf_ref[t] == 1)
    def _():
        wg_ref[f] = wg32_ref[...].astype(BF16)
        wu_ref[f] = wu32_ref[...].astype(BF16)
        wd_ref[f] = wd32_ref[...].astype(BF16)

    def matmul(c):
        x = x_ref[_chunk(c), :]
        return (jnp.dot(x, wg_ref[f], preferred_element_type=F32),
                jnp.dot(x, wu_ref[f], preferred_element_type=F32))

    def epilogue(c, gu):
        a = (_silu(gu[0]) * gu[1]).astype(BF16)
        acc_ref[_chunk(c), :] += jnp.dot(a, wd_ref[f], preferred_element_type=F32)

    for n_chunks in range(1, MOE_TM // ROW_CHUNK + 1):
        @pl.when((n_rows > (n_chunks - 1) * ROW_CHUNK) & (n_rows <= n_chunks * ROW_CHUNK))
        def _(n_chunks=n_chunks):
            _pipelined_chunks(n_chunks, matmul, epilogue)

    @pl.when(f == pl.num_programs(1) - 1)
    def _():
        o_ref[...] = acc_ref[...].astype(o_ref.dtype)


def _moe_ffn(xs, tile_expert, tile_rows, tile_first, w_gu, w_down, lf):
    r, d = xs.shape
    ff = w_down.shape[2]
    nf = ff // MOE_BF
    n_tiles = r // MOE_TM

    def fidx(t, f, tf):
        return jnp.where(tf[t] == 1, f, nf - 1)

    grid_spec = pltpu.PrefetchScalarGridSpec(
        num_scalar_prefetch=3,
        grid=(n_tiles, nf),
        in_specs=[
            pl.BlockSpec((MOE_TM, d), lambda t, f, te, tr, tf: (t, 0)),
            pl.BlockSpec((None, None, d, MOE_BF), lambda t, f, te, tr, tf: (lf, te[t], 0, fidx(t, f, tf))),
            pl.BlockSpec((None, None, d, MOE_BF),
                         lambda t, f, te, tr, tf: (lf, te[t], 0, nf + fidx(t, f, tf))),
            pl.BlockSpec((None, None, MOE_BF, d), lambda t, f, te, tr, tf: (lf, te[t], fidx(t, f, tf), 0)),
        ],
        out_specs=pl.BlockSpec((MOE_TM, d), lambda t, f, te, tr, tf: (t, 0)),
        scratch_shapes=[pltpu.VMEM((MOE_TM, d), F32), pltpu.VMEM((nf, d, MOE_BF), BF16),
                        pltpu.VMEM((nf, d, MOE_BF), BF16), pltpu.VMEM((nf, MOE_BF, d), BF16)],
    )
    return pl.pallas_call(
        _moe_ffn_kernel,
        grid_spec=grid_spec,
        out_shape=jax.ShapeDtypeStruct((r, d), BF16),
        compiler_params=_cparams(("arbitrary", "arbitrary")),
        name="moe_ffn",
    )(tile_expert, tile_rows, tile_first, xs, w_gu, w_gu, w_down)


def _combine_kernel(x_ref, y0_ref, y1_ref, wgt_ref, gate_ref, gf_ref, o_ref, *, final):
    w = wgt_ref[...]
    y = w[:, 0:1] * y0_ref[...].astype(F32) + w[:, 1:2] * y1_ref[...].astype(F32)
    xn = x_ref[...] + gate_ref[...] * y
    if final:
        xn = xn * lax.rsqrt(jnp.mean(xn * xn, axis=-1, keepdims=True) + EPS) * gf_ref[...]
    o_ref[...] = xn


def _combine(x, y0, y1, wgt, mods, layer, g_final, *, n_rows, geom, final):
    d = x.shape[1]
    tpb, n_batch = geom
    lanes = wgt.shape[1]
    gate_spec = pl.BlockSpec((None, None, 1, d),
                             lambda i: (layer * 6 + 5, jnp.minimum(i // tpb, n_batch), 0, 0))
    row = lambda w: pl.BlockSpec((TM, w), lambda i: (i, 0))
    out_rows = n_rows if final else x.shape[0]
    return pl.pallas_call(
        functools.partial(_combine_kernel, final=final),
        grid=(n_rows // TM,),
        in_specs=[row(d), row(d), row(d), row(lanes), gate_spec, pl.BlockSpec((1, d), lambda i: (0, 0))],
        out_specs=row(d),
        out_shape=jax.ShapeDtypeStruct((out_rows, d), F32),
        input_output_aliases={} if final else {0: 0},
        compiler_params=_cparams(("arbitrary",)),
        name="moe_combine",
    )(x, y0, y1, wgt, mods, g_final.reshape(1, d))


def _moe_layer(x, g, mods, layer, w_router, w_gu, w_down, lf, g_final, *, n_rows, geom, final):
    d = x.shape[1]
    wr_pad = jnp.zeros((d, 128), F32).at[:, :N_EXPERTS].set(w_router)
    n_assign = 2 * n_rows
    r_rows = n_assign + (N_EXPERTS - 1) * MOE_TM
    h, idx, wgt = _router(x, g, mods, layer, wr_pad, n_rows=n_rows, h_rows=r_rows, geom=geom)
    e_flat = idx[:, :2].reshape(-1)
    onehot = (e_flat[:, None] == jnp.arange(N_EXPERTS, dtype=jnp.int32)[None, :]).astype(F32)
    blk = onehot.reshape(n_assign // CUMSUM_BLOCK, CUMSUM_BLOCK, N_EXPERTS)
    tril = jnp.tril(jnp.ones((CUMSUM_BLOCK, CUMSUM_BLOCK), F32))
    within = jnp.einsum("ij,bjk->bik", tril, blk)
    blk_tot = within[:, -1, :]
    blk_off = jnp.cumsum(blk_tot, axis=0) - blk_tot
    csum = (within + blk_off[:, None, :]).reshape(n_assign, N_EXPERTS)
    rank = (jnp.sum(onehot * csum, axis=1) - 1.0).astype(jnp.int32)
    counts = jnp.sum(blk_tot, axis=0).astype(jnp.int32)
    padded = ((counts + MOE_TM - 1) // MOE_TM) * MOE_TM
    ends = jnp.cumsum(padded)
    starts = ends - padded
    dest = starts[e_flat] + rank
    n_tiles = r_rows // MOE_TM
    hit = jnp.zeros((r_rows,), jnp.int32).at[dest].add(jnp.arange(n_assign, dtype=jnp.int32) // 2 + 1)
    src = jnp.where(hit > 0, hit - 1, jnp.arange(r_rows, dtype=jnp.int32) % n_rows)
    tile_start = jnp.arange(n_tiles, dtype=jnp.int32) * MOE_TM
    last_tile = jnp.maximum(ends[-1] // MOE_TM - 1, 0)
    ts_used = jnp.minimum(tile_start, last_tile * MOE_TM)
    tile_expert = jnp.sum((ends[None, :] <= ts_used[:, None]).astype(jnp.int32), axis=1)
    tile_expert = jnp.minimum(tile_expert, N_EXPERTS - 1)
    real_end = (starts + counts)[tile_expert]
    used = tile_start < ends[-1]
    tile_rows = jnp.where(used, jnp.clip(real_end - tile_start, 0, MOE_TM), 0).astype(jnp.int32)
    tile_first = (used & (tile_start == starts[tile_expert])).astype(jnp.int32)
    xs = jnp.take(h, src, axis=0, mode="clip")
    ys = _moe_ffn(xs, tile_expert, tile_rows, tile_first, w_gu, w_down, lf)
    d2 = dest.reshape(n_rows, 2)
    y0 = jnp.take(ys, d2[:, 0], axis=0, mode="clip")
    y1 = jnp.take(ys, d2[:, 1], axis=0, mode="clip")
    return _combine(x, y0, y1, wgt, mods, layer, g_final, n_rows=n_rows, geom=geom, final=final)


def kernel(x, c, ctx, c_ctx, ada_w, ada_b, norm_mix_g, norm_ffn_g, final_norm_g, ret_w_in, ret_w_o, ret_decay_f, ret_decay_b, ret_gn_f, ret_gn_b, mla_w_down, mla_g_q, mla_g_kv, mla_w_uq, mla_w_ukv, mla_w_o, sgu_w_in, sgu_g_v, sgu_w_s, sgu_b_s, sgu_w_out, ffn_w_gu, ffn_w_down, moe_w_router, moe_w_gu, moe_w_down):
    n_batch, n_lat, d = x.shape
    n_ctx = ctx.shape[1]
    depth = ada_w.shape[0]
    rows_lat = n_batch * n_lat
    m = rows_lat + n_batch * n_ctx
    geom = (n_lat // TM, n_batch)
    assert n_batch * n_ctx == TM and d == 1024 and n_batch + 1 <= 8

    xt = (x.reshape(rows_lat, d), ctx.reshape(n_batch * n_ctx, d))
    cond = jnp.concatenate([c, c_ctx[None, :], jnp.zeros((8 - n_batch - 1, d), F32)], axis=0)
    mods = _adaln(cond, ada_w, ada_b)
    mods = mods.reshape(depth, 8, 6, d).transpose(0, 2, 1, 3).reshape(depth * 6, 8, 1, d)

    grid_rows = n_lat // GRID_W
    ret_cos, ret_sin = _rope_tables(grid_rows, n_batch * n_ctx, RET_DK)
    mla_cos, mla_sin = _rope_tables(grid_rows, n_batch * n_ctx, MLA_DR)
    pad_r = MLA_DN - MLA_DR
    mla_cos = jnp.concatenate([mla_cos, jnp.ones((mla_cos.shape[0], pad_r), F32)], axis=1)
    mla_sin = jnp.concatenate([mla_sin, jnp.zeros((mla_sin.shape[0], pad_r), F32)], axis=1)

    for i in range(depth):
        last = i == depth - 1
        kind, jm = i % N_MIXERS, i // N_MIXERS
        rows_out = rows_lat if last else m
        if kind == 0:
            qk, vv, gates = _ret_in(xt, norm_mix_g[i], mods, i, ret_w_in[jm].astype(BF16),
                                    (ret_cos, ret_sin), geom=geom)
            lg = jnp.stack([jax.nn.log_sigmoid(ret_decay_f[jm].astype(F32)),
                            jax.nn.log_sigmoid(ret_decay_b[jm].astype(F32))])
            hv = vv.shape[1]
            gn = jnp.stack([ret_gn_f[jm], ret_gn_b[jm]]).reshape(2, 1, hv)
            yf, yb = _retention(qk, vv, gates, lg, gn, n_batch, n_lat, n_ctx)
            xt = _outproj([yf, yb], ret_w_o[jm].astype(BF16), xt, mods, i, 2, n_rows=rows_out, geom=geom)
        elif kind == 1:
            heads = mla_w_o.shape[1] // MLA_DV
            lq, lkv = mla_g_q.shape[1], mla_g_kv.shape[1]
            n_dn = lq + lkv + MLA_DN
            wd = jnp.zeros((d, n_dn), F32).at[:, :lq + lkv + MLA_DR].set(mla_w_down[jm]).astype(BF16)
            dn = _proj(xt, norm_mix_g[i], mods, i, 0, wd, mode="f32", bn=n_dn, out_dtype=F32,
                       n_rows=m, geom=geom)
            wq = mla_w_uq[jm].reshape(lq, heads, MLA_DN + MLA_DR)
            wq = jnp.concatenate([wq, jnp.zeros((lq, heads, pad_r), F32)], axis=2)
            wq = wq.reshape(lq, heads * 2 * MLA_DN).astype(BF16)
            wkv = mla_w_ukv[jm].reshape(lkv, heads, MLA_DN + MLA_DV)
            wk = wkv[:, :, :MLA_DN].reshape(lkv, heads * MLA_DN).astype(BF16)
            wv = wkv[:, :, MLA_DN:].reshape(lkv, heads * MLA_DV).T.astype(BF16)
            tpb = geom[0]
            n_lat_tiles = tpb * n_batch
            pos_idx = lambda r: (jnp.where(r < n_lat_tiles, r % tpb, tpb + (r - n_lat_tiles)), 0)
            full = lambda a: pl.BlockSpec(a.shape, lambda r: (0,) * a.ndim)
            gq = mla_g_q[jm].reshape(1, lq)
            gkv = mla_g_kv[jm].reshape(1, lkv)
            scale = float((MLA_DN + MLA_DR) ** -0.5 * LOG2_E)
            q, k, vt = pl.pallas_call(
                functools.partial(_mla_up_kernel, heads=heads, scale=scale),
                grid=(m // TM,),
                in_specs=[pl.BlockSpec((TM, n_dn), lambda r: (r, 0)), full(gq), full(gkv), full(wq),
                          full(wk), full(wv), pl.BlockSpec((TM, MLA_DN), pos_idx),
                          pl.BlockSpec((TM, MLA_DN), pos_idx)],
                out_specs=[pl.BlockSpec((TM, heads * 2 * MLA_DN), lambda r: (r, 0)),
                           pl.BlockSpec((TM, heads * 2 * MLA_DN), lambda r: (r, 0)),
                           pl.BlockSpec((heads * MLA_DV, TM), lambda r: (0, r))],
                out_shape=[jax.ShapeDtypeStruct((m, heads * 2 * MLA_DN), BF16),
                           jax.ShapeDtypeStruct((m, heads * 2 * MLA_DN), BF16),
                           jax.ShapeDtypeStruct((heads * MLA_DV, m), BF16)],
                compiler_params=_cparams(("arbitrary",)),
                name="mla_up",
            )(dn, gq, gkv, wq, wk, wv, mla_cos, mla_sin)
            ao = _mla_attention(q, k, vt, n_batch, n_lat, n_ctx, heads)
            xt = _outproj([ao], mla_w_o[jm].astype(BF16), xt, mods, i, 2, n_rows=rows_out, geom=geom)
        else:
            uv = _proj(xt, norm_mix_g[i], mods, i, 0, sgu_w_in[jm].astype(BF16), mode="gelu", bn=2048,
                       out_dtype=BF16, n_rows=m, geom=geom)
            xt = _sgu(uv, sgu_g_v[jm], sgu_w_s[jm].astype(BF16), sgu_b_s[jm], sgu_w_out[jm].astype(BF16),
                      xt, mods, i, geom)
        f = i // 2
        if i % 2 == 0:
            dff = ffn_w_down.shape[1]
            wgu = ffn_w_gu[f].astype(BF16)
            xt = _dense_ffn(xt, norm_ffn_g[i], mods, i, wgu[:, :dff], wgu[:, dff:],
                            ffn_w_down[f].astype(BF16), n_rows=rows_out, geom=geom)
        else:
            xt = _moe_layer(xt, norm_ffn_g[i], mods, i, moe_w_router[f], moe_w_gu, moe_w_down, f,
                            final_norm_g, n_rows=rows_out, geom=geom, final=last)
    if not (depth - 1) % 2:
        raise NotImplementedError("final norm is fused into the expert combine of the last layer")
    return xt.reshape(n_batch, n_lat, d)
```

```python
import functools

import jax
import jax.numpy as jnp
from jax import lax
from jax.experimental import pallas as pl
from jax.experimental.pallas import tpu as pltpu

F32 = jnp.float32
BF16 = jnp.bfloat16
HIGHEST = lax.Precision.HIGHEST
LOG2_E = 1.4426950408889634

GRID_W = 64
ROPE_BASE = 10000.0
EPS = 1e-6
RET_DK = 256
RET_DV = 512
MLA_DN = 128
MLA_DR = 64
MLA_DV = 128
SGU_GROUPS = 8
SGU_CHUNK = 128
N_EXPERTS = 8
N_MIXERS = 3

TM = 1024
ROW_CHUNK = 256
RET_CHUNK = 256
RET_HEADS_PER_STEP = 2
RET_IN_STEPS = 4
ATTN_TQ = 512
ATTN_TK = 1024
ATTN_ONES_ROWS = 16
ATTN_MAX_LAG_LOG2 = 60.0
MOE_TM = 1024
MOE_BF = 512
CUMSUM_BLOCK = 256
SGU_IN_COLS = 1024
VMEM_LIMIT = 56 * 1024 * 1024


def _cparams(sem):
    return pltpu.CompilerParams(dimension_semantics=sem, vmem_limit_bytes=VMEM_LIMIT)


def _silu(x):
    return x * jax.nn.sigmoid(x)


def _rope(x, cos, sin_signed, half):
    w = x.shape[-1]
    lane = lax.broadcasted_iota(jnp.int32, x.shape, 1)
    first = (lane % (2 * half)) < half
    rot = jnp.where(first, pltpu.roll(x, w - half, 1), pltpu.roll(x, half, 1))
    return x * cos + rot * sin_signed


def _rope_tables(rows, n_ctx_rows, rot_dim):
    row = jnp.repeat(jnp.arange(rows, dtype=F32), GRID_W)
    col = jnp.tile(jnp.arange(GRID_W, dtype=F32), rows)
    axis_dim = rot_dim // 2
    inv_freq = ROPE_BASE ** (-jnp.arange(0, axis_dim, 2, dtype=F32) / axis_dim)
    ang_r = row[:, None] * inv_freq
    ang_c = col[:, None] * inv_freq
    ang = jnp.concatenate([ang_r, ang_r, ang_c, ang_c], axis=-1)
    quarter = rot_dim // 4
    sign = jnp.where((jnp.arange(rot_dim) % (2 * quarter)) < quarter, -1.0, 1.0).astype(F32)
    cos = jnp.concatenate([jnp.cos(ang), jnp.ones((n_ctx_rows, rot_dim), F32)], axis=0)
    sin = jnp.concatenate([jnp.sin(ang) * sign, jnp.zeros((n_ctx_rows, rot_dim), F32)], axis=0)
    return cos, sin


def _adaln_kernel(cond_ref, w_ref, b_ref, o_ref):
    s = _silu(cond_ref[...])
    o_ref[...] = jnp.dot(s, w_ref[...], precision=HIGHEST, preferred_element_type=F32) + b_ref[...]


def _adaln(cond, ada_w, ada_b):
    depth, d, n6 = ada_w.shape
    bn = n6 // 4
    return pl.pallas_call(
        _adaln_kernel,
        grid=(depth, n6 // bn),
        in_specs=[
            pl.BlockSpec((8, d), lambda l, j: (0, 0)),
            pl.BlockSpec((None, d, bn), lambda l, j: (l, 0, j)),
            pl.BlockSpec((None, 1, bn), lambda l, j: (l, 0, j)),
        ],
        out_specs=pl.BlockSpec((None, 8, bn), lambda l, j: (l, 0, j)),
        out_shape=jax.ShapeDtypeStruct((depth, 8, n6), F32),
        compiler_params=_cparams(("arbitrary", "arbitrary")),
        name="adaln",
    )(cond, ada_w, ada_b.reshape(depth, 1, n6))


def _row_specs(src, width, n_main_tiles):
    if isinstance(src, tuple):
        specs = [pl.BlockSpec((TM, width), lambda i, *_: (jnp.minimum(i, n_main_tiles - 1), 0)),
                 pl.BlockSpec((TM, width), lambda i, *_: (jnp.maximum(i - n_main_tiles, 0), 0))]
        return specs, list(src)
    return [pl.BlockSpec((TM, width), lambda i, *_: (i, 0))], [src]


def _row_tile(refs, i, n_main_tiles):
    if len(refs) == 2:
        return jnp.where(i < n_main_tiles, refs[0][...], refs[1][...])
    return refs[0][...]


def _modnorm(xf, g_ref, shift_ref, scale_ref):
    y = xf * lax.rsqrt(jnp.mean(xf * xf, axis=-1, keepdims=True) + EPS)
    return (y * g_ref[...]) * (1.0 + scale_ref[...]) + shift_ref[...]


def _chunk(c):
    return slice(c * ROW_CHUNK, (c + 1) * ROW_CHUNK)


def _pipelined_chunks(n_chunks, matmul, epilogue):
    pending = matmul(0)
    for c in range(n_chunks):
        following = matmul(c + 1) if c + 1 < n_chunks else None
        epilogue(c, pending)
        pending = following


def _ret_in_kernel(*refs, n_q, x_count, n_main_tiles):
    x_refs = refs[:x_count]
    (g_ref, sh_ref, sc_ref, wqk_ref, wv_ref, wg_ref, cos_ref, sin_ref,
     qk_ref, v_ref, gate_ref, h_scr) = refs[x_count:]
    j = pl.program_id(1)

    @pl.when(j == 0)
    def _():
        xf = _row_tile(x_refs, pl.program_id(0), n_main_tiles)
        h_scr[...] = _modnorm(xf, g_ref, sh_ref, sc_ref).astype(BF16)

    kscale = jnp.where(j >= n_q, RET_DK ** -0.5, 1.0).astype(F32)

    def matmul(c):
        h = h_scr[_chunk(c), :]
        return tuple(jnp.dot(h, w[...], preferred_element_type=F32) for w in (wqk_ref, wv_ref, wg_ref))

    def epilogue(c, ys):
        r = _chunk(c)
        for hs in range(ys[0].shape[1] // RET_DK):
            cols = slice(hs * RET_DK, (hs + 1) * RET_DK)
            qk = _rope(ys[0][:, cols], cos_ref[r, :], sin_ref[r, :], RET_DK // 4)
            qk_ref[r, cols] = (qk * kscale).astype(qk_ref.dtype)
        v_ref[r, :] = ys[1].astype(v_ref.dtype)
        gate_ref[r, :] = _silu(ys[2]).astype(gate_ref.dtype)

    _pipelined_chunks(h_scr.shape[0] // ROW_CHUNK, matmul, epilogue)


def _ret_in(x, g, mods, layer, w_in, rope, *, geom):
    x_parts = list(x) if isinstance(x, tuple) else [x]
    m = sum(p.shape[0] for p in x_parts)
    d = x_parts[0].shape[1]
    heads = d // RET_DK
    nj = RET_IN_STEPS
    n_q = nj // 2
    qk_cols = (2 * heads * RET_DK) // nj
    wv_cols = (heads * RET_DV) // nj
    wg_cols = (2 * heads * RET_DV) // nj
    assert wv_cols == qk_cols and wg_cols == 2 * qk_cols and qk_cols % RET_DK == 0
    tpb, n_batch = geom
    sh_spec, sc_spec = _mod_specs(layer, 0, tpb, n_batch)
    n_lat_tiles = tpb * n_batch
    cos, sin = rope

    def pos_idx(i, j):
        return (jnp.where(i < n_lat_tiles, i % tpb, tpb + (i - n_lat_tiles)), 0)

    x_specs, x_arrs = _row_specs(x, d, n_lat_tiles)
    return pl.pallas_call(
        functools.partial(_ret_in_kernel, n_q=n_q, x_count=len(x_arrs), n_main_tiles=n_lat_tiles),
        grid=(m // TM, nj),
        in_specs=x_specs + [
            pl.BlockSpec((1, d), lambda i, j: (0, 0)),
            sh_spec, sc_spec,
            pl.BlockSpec((d, qk_cols), lambda i, j: (0, j)),
            pl.BlockSpec((d, qk_cols), lambda i, j: (0, nj + j)),
            pl.BlockSpec((d, 2 * qk_cols), lambda i, j: (0, nj + j)),
            pl.BlockSpec((TM, RET_DK), pos_idx),
            pl.BlockSpec((TM, RET_DK), pos_idx),
        ],
        out_specs=[
            pl.BlockSpec((TM, qk_cols), lambda i, j: (i, j)),
            pl.BlockSpec((TM, qk_cols), lambda i, j: (i, j)),
            pl.BlockSpec((TM, 2 * qk_cols), lambda i, j: (i, j)),
        ],
        out_shape=[
            jax.ShapeDtypeStruct((m, 2 * heads * RET_DK), BF16),
            jax.ShapeDtypeStruct((m, heads * RET_DV), BF16),
            jax.ShapeDtypeStruct((m, 2 * heads * RET_DV), BF16),
        ],
        scratch_shapes=[pltpu.VMEM((TM, d), BF16)],
        compiler_params=_cparams(("arbitrary", "arbitrary")),
        name="ret_in",
    )(*x_arrs, g.reshape(1, d), mods, mods, w_in, w_in, w_in, cos, sin)


def _proj_kernel(x_ref, g_ref, sh_ref, sc_ref, w_ref, o_ref, h_scr, *, mode):
    j = pl.program_id(1)

    @pl.when(j == 0)
    def _():
        h_scr[...] = _modnorm(x_ref[...], g_ref, sh_ref, sc_ref).astype(BF16)

    def matmul(c):
        return jnp.dot(h_scr[_chunk(c), :], w_ref[...], preferred_element_type=F32)

    def epilogue(c, y):
        assert mode == "f32"
        o_ref[_chunk(c), :] = y

    _pipelined_chunks(h_scr.shape[0] // ROW_CHUNK, matmul, epilogue)


def _mod_specs(layer, k_shift, tiles_per_batch, n_batch):
    def spec(k):
        return pl.BlockSpec((None, None, 1, 1024),
                            lambda i, j: (layer * 6 + k, jnp.minimum(i // tiles_per_batch, n_batch), 0, 0))
    return spec(k_shift), spec(k_shift + 1)


def _proj(x, g, mods, layer, k_shift, w, *, mode, bn, out_dtype, n_rows, geom):
    d = x.shape[1]
    n = w.shape[1]
    nt = n_rows // TM
    tpb, n_batch = geom
    sh_spec, sc_spec = _mod_specs(layer, k_shift, tpb, n_batch)
    in_specs = [
        pl.BlockSpec((TM, d), lambda i, j: (i, 0)),
        pl.BlockSpec((1, d), lambda i, j: (0, 0)),
        sh_spec, sc_spec,
        pl.BlockSpec((d, bn), lambda i, j: (0, j)),
    ]
    args = [x, g.reshape(1, d), mods, mods, w]
    return pl.pallas_call(
        functools.partial(_proj_kernel, mode=mode),
        grid=(nt, n // bn),
        in_specs=in_specs,
        out_specs=pl.BlockSpec((TM, bn), lambda i, j: (i, j)),
        out_shape=jax.ShapeDtypeStruct((n_rows, n), out_dtype),
        scratch_shapes=[pltpu.VMEM((TM, d), BF16)],
        compiler_params=_cparams(("arbitrary", "arbitrary")),
        name="proj_" + mode,
    )(*args)


def _dense_ffn_kernel(x_ref, g_ref, sh_ref, sc_ref, gate_ref, wg_ref, wu_ref, wd_ref, o_ref, h_scr):
    h_scr[...] = _modnorm(x_ref[...], g_ref, sh_ref, sc_ref).astype(BF16)

    def matmul(c):
        h = h_scr[_chunk(c), :]
        return (jnp.dot(h, wg_ref[...], preferred_element_type=F32),
                jnp.dot(h, wu_ref[...], preferred_element_type=F32))

    def epilogue(c, gu):
        a = (_silu(gu[0]) * gu[1]).astype(BF16)
        y = jnp.dot(a, wd_ref[...], preferred_element_type=F32)
        o_ref[_chunk(c), :] = x_ref[_chunk(c), :] + gate_ref[...] * y

    _pipelined_chunks(h_scr.shape[0] // ROW_CHUNK, matmul, epilogue)


def _dense_ffn(x, g, mods, layer, w_g, w_u, w_d, *, n_rows, geom):
    d = x.shape[1]
    dff = w_d.shape[0]
    tpb, n_batch = geom

    def mod_spec(k):
        return pl.BlockSpec((None, None, 1, d),
                            lambda i: (layer * 6 + k, jnp.minimum(i // tpb, n_batch), 0, 0))

    def resident(shape):
        return pl.BlockSpec(shape, lambda i: (0, 0), pipeline_mode=pl.Buffered(1))

    return pl.pallas_call(
        _dense_ffn_kernel,
        grid=(n_rows // TM,),
        in_specs=[
            pl.BlockSpec((TM, d), lambda i: (i, 0)),
            pl.BlockSpec((1, d), lambda i: (0, 0)),
            mod_spec(3), mod_spec(4), mod_spec(5),
            resident((d, dff)), resident((d, dff)), resident((dff, d)),
        ],
        out_specs=pl.BlockSpec((TM, d), lambda i: (i, 0)),
        out_shape=jax.ShapeDtypeStruct(x.shape, F32),
        scratch_shapes=[pltpu.VMEM((TM, d), BF16)],
        input_output_aliases={0: 0},
        compiler_params=_cparams(("arbitrary",)),
        name="dense_ffn",
    )(x, g.reshape(1, d), mods, mods, mods, w_g, w_u, w_d)


def _outproj_kernel(*refs, a_counts, x_count, n_main_tiles):
    i = pl.program_id(0)
    refs = list(refs)
    a_tiles = []
    for cnt in a_counts:
        a_tiles.append(_row_tile(refs[:cnt], i, n_main_tiles))
        refs = refs[cnt:]
    w_ref = refs[0]
    x = _row_tile(refs[1:1 + x_count], i, n_main_tiles)
    gate_ref, o_ref = refs[1 + x_count:]
    if len(a_tiles) == 1:
        a = a_tiles[0]
    else:
        a = (a_tiles[0].astype(F32) + a_tiles[1].astype(F32)).astype(BF16)
    y = jnp.dot(a, w_ref[...], preferred_element_type=F32)
    o_ref[...] = x + gate_ref[...] * y


def _outproj(a_list, w, x, mods, layer, k_gate, *, n_rows, geom):
    k, d = w.shape
    nt = n_rows // TM
    tpb, n_batch = geom
    n_main = tpb * n_batch
    gate_spec = pl.BlockSpec((None, None, 1, d),
                             lambda i: (layer * 6 + k_gate, jnp.minimum(i // tpb, n_batch), 0, 0))
    in_specs, args, a_counts = [], [], []
    for a in a_list:
        specs, arrs = _row_specs(a, k, n_main)
        in_specs += specs
        args += arrs
        a_counts.append(len(arrs))
    x_specs, x_arrs = _row_specs(x, d, n_main)
    total_rows = sum(xa.shape[0] for xa in x_arrs)
    return pl.pallas_call(
        functools.partial(_outproj_kernel, a_counts=tuple(a_counts), x_count=len(x_arrs), n_main_tiles=n_main),
        grid=(nt,),
        in_specs=in_specs + [pl.BlockSpec((k, d), lambda i: (0, 0))] + x_specs + [gate_spec],
        out_specs=pl.BlockSpec((TM, d), lambda i: (i, 0)),
        out_shape=jax.ShapeDtypeStruct((total_rows, d), F32),
        input_output_aliases={len(args) + 1: 0} if len(x_arrs) == 1 else {},
        compiler_params=_cparams(("arbitrary",)),
        name="outproj",
    )(*args, w, *x_arrs, mods)


def _ret_kernel(lg_ref, qf_ref, kf_ref, vf_ref, gf_ref, qb_ref, kb_ref, vb_ref, gb_ref, gnf_ref, gnb_ref,
                of_ref, ob_ref, s_scr, intra_scr, dec_scr):
    hp = pl.program_id(1)
    s = pl.program_id(2)
    c = RET_CHUNK
    nh = RET_HEADS_PER_STEP
    chains = [(d, j) for d in range(2) for j in range(nh)]

    @pl.when(s == 0)
    def _():
        s_scr[...] = jnp.zeros_like(s_scr)
        ii = lax.broadcasted_iota(jnp.int32, (c, c), 0)
        jj = lax.broadcasted_iota(jnp.int32, (c, c), 1)
        pos = lax.broadcasted_iota(jnp.int32, (c, 1), 0).astype(F32)
        for n, (d, j) in enumerate(chains):
            lg = lg_ref[d, hp * nh + j]
            diff = ((ii - jj) if d == 0 else (jj - ii)).astype(F32)
            intra_scr[n] = jnp.where(diff >= 0, jnp.exp(lg * jnp.maximum(diff, 0.0)), 0.0)
            fpos = pos if d == 0 else (c - 1.0) - pos
            dec_scr[2 * n] = jnp.exp(lg * (fpos + 1.0))
            dec_scr[2 * n + 1] = jnp.exp(lg * ((c - 1.0) - fpos))

    refs = ((qf_ref, kf_ref, vf_ref, gf_ref, gnf_ref, of_ref), (qb_ref, kb_ref, vb_ref, gb_ref, gnb_ref, ob_ref))
    qcols = [slice(j * RET_DK, (j + 1) * RET_DK) for _, j in chains]
    vcols = [slice(j * RET_DV, (j + 1) * RET_DV) for _, j in chains]
    n_chains = range(len(chains))
    q = [refs[d][0][:, qcols[n]] for n, (d, _) in enumerate(chains)]
    k = [refs[d][1][:, qcols[n]] for n, (d, _) in enumerate(chains)]
    v = [refs[d][2][:, vcols[n]] for n, (d, _) in enumerate(chains)]
    contract_last = (((1,), (1,)), ((), ()))
    contract_rows = (((0,), (0,)), ((), ()))
    scores = [lax.dot_general(q[n], k[n], contract_last, preferred_element_type=F32) for n in n_chains]
    masked = [(scores[n] * intra_scr[n]).astype(BF16) for n in n_chains]
    state = [s_scr[n] for n in n_chains]
    qd = [(q[n].astype(F32) * dec_scr[2 * n]).astype(BF16) for n in n_chains]
    o = [jnp.dot(masked[n], v[n], preferred_element_type=F32)
         + jnp.dot(qd[n], state[n].astype(BF16), preferred_element_type=F32) for n in n_chains]
    kd = [(k[n].astype(F32) * dec_scr[2 * n + 1]).astype(BF16) for n in n_chains]
    for n, (d, j) in enumerate(chains):
        chunk_decay = jnp.exp(lg_ref[d, hp * nh + j] * float(c))
        s_scr[n] = state[n] * chunk_decay + lax.dot_general(kd[n], v[n], contract_rows,
                                                           preferred_element_type=F32)
    for n, (d, _) in enumerate(chains):
        gate_ref, gn_ref, o_ref = refs[d][3:]
        mu = jnp.mean(o[n], axis=-1, keepdims=True)
        oc = o[n] - mu
        var = jnp.mean(oc * oc, axis=-1, keepdims=True)
        y = oc * lax.rsqrt(var + EPS) * gn_ref[:, vcols[n]]
        o_ref[:, vcols[n]] = (gate_ref[:, vcols[n]].astype(F32) * y).astype(o_ref.dtype)


def _retention(qk, v, gates, lg, gn, n_batch, n_lat, n_ctx):
    m = qk.shape[0]
    heads = v.shape[1] // RET_DV
    c = RET_CHUNK
    lat_chunks = n_lat // c
    assert n_ctx == c
    steps = 1 + lat_chunks
    ctx_blk0 = (n_batch * n_lat) // c

    def row_blk(d, b, s):
        t = (s - 1) if d == 0 else (lat_chunks - s)
        return jnp.where(s == 0, ctx_blk0 + b, b * lat_chunks + t)

    nh = RET_HEADS_PER_STEP
    groups = heads // nh
    qw, vw = nh * RET_DK, nh * RET_DV

    def specs(d):
        return [
            pl.BlockSpec((c, qw), lambda b, h, s: (row_blk(d, b, s), h)),
            pl.BlockSpec((c, qw), lambda b, h, s: (row_blk(d, b, s), groups + h)),
            pl.BlockSpec((c, vw), lambda b, h, s: (row_blk(d, b, s), h)),
            pl.BlockSpec((c, vw), lambda b, h, s: (row_blk(d, b, s), groups * d + h)),
        ]

    def gn_spec(d):
        return pl.BlockSpec((None, 1, vw), lambda b, h, s: (d, 0, h))

    shape = jax.ShapeDtypeStruct((m, heads * RET_DV), BF16)
    yf, yb = pl.pallas_call(
        _ret_kernel,
        grid=(n_batch, groups, steps),
        in_specs=[pl.BlockSpec(memory_space=pltpu.SMEM)] + specs(0) + specs(1) + [gn_spec(0), gn_spec(1)],
        out_specs=[pl.BlockSpec((c, vw), lambda b, h, s: (row_blk(0, b, s), h)),
                   pl.BlockSpec((c, vw), lambda b, h, s: (row_blk(1, b, s), h))],
        out_shape=[shape, shape],
        scratch_shapes=[pltpu.VMEM((2 * nh, RET_DK, RET_DV), F32), pltpu.VMEM((2 * nh, c, c), F32),
                        pltpu.VMEM((4 * nh, c, 1), F32)],
        compiler_params=_cparams(("arbitrary",) * 3),
        name="retention",
    )(lg, qk, qk, v, gates, qk, qk, v, gates, gn, gn)
    return yf, yb


def _mla_up_kernel(dn_ref, gq_ref, gkv_ref, wq_ref, wk_ref, wv_ref, cos_ref, sin_ref,
                   q_ref, k_ref, v_ref, *, heads, scale):
    dn = dn_ref[...]
    lq = gq_ref.shape[1]
    lkv = gkv_ref.shape[1]

    def rms(a, g):
        return (a * lax.rsqrt(jnp.mean(a * a, axis=-1, keepdims=True) + EPS) * g).astype(BF16)

    cq = rms(dn[:, :lq], gq_ref[...])
    ckv = rms(dn[:, lq:lq + lkv], gkv_ref[...])
    cos = cos_ref[...]
    sin = sin_ref[...]
    half = MLA_DR // 4
    q = jnp.dot(cq, wq_ref[...], preferred_element_type=F32)
    kn = jnp.dot(ckv, wk_ref[...], preferred_element_type=F32)
    v_ref[...] = lax.dot_general(wv_ref[...], ckv, (((1,), (1,)), ((), ())),
                                 preferred_element_type=F32).astype(v_ref.dtype)
    kr = _rope(dn[:, lq + lkv:], cos, sin, half).astype(k_ref.dtype)
    for h in range(heads):
        o = h * 2 * MLA_DN
        q_ref[:, o:o + MLA_DN] = (q[:, o:o + MLA_DN] * scale).astype(q_ref.dtype)
        qr = _rope(q[:, o + MLA_DN:o + 2 * MLA_DN], cos, sin, half)
        q_ref[:, o + MLA_DN:o + 2 * MLA_DN] = (qr * scale).astype(q_ref.dtype)
        k_ref[:, o:o + MLA_DN] = kn[:, h * MLA_DN:(h + 1) * MLA_DN].astype(k_ref.dtype)
        k_ref[:, o + MLA_DN:o + 2 * MLA_DN] = kr


def _attn_lat_kernel(q_ref, kl_ref, vtl_ref, kc_ref, vtc_ref, o_ref):
    tq = ATTN_TQ
    dv = vtl_ref.shape[0]
    n_chunks = kl_ref.shape[0] // ATTN_TK

    def scores(k, q):
        return lax.dot_general(k, q, (((1,), (1,)), ((), ())), preferred_element_type=F32)

    def weighted(vt, p):
        lhs = jnp.concatenate([vt, jnp.ones((ATTN_ONES_ROWS, vt.shape[1]), BF16)], axis=0)
        return jnp.dot(lhs, p, preferred_element_type=F32)

    def exact_max_block(s, vt, m, acc, risk):
        m_new = jnp.maximum(m, jnp.max(s, axis=0, keepdims=True))
        p = jnp.exp2(s - m_new).astype(BF16)
        return m_new, jnp.exp2(m - m_new) * acc + weighted(vt, p), risk

    def lagged_max_block(s, vt, m, acc, risk):
        p = jnp.exp2(s - m).astype(BF16)
        bmax = jnp.max(s, axis=0, keepdims=True)
        m_new = jnp.maximum(m, bmax)
        return m_new, (acc + weighted(vt, p)) * jnp.exp2(m - m_new), jnp.maximum(risk, bmax - m)

    def run(later_block, q_rows):
        q = q_ref[q_rows, :]
        state = (jnp.full((1, tq), -jnp.inf, F32), jnp.zeros((dv + ATTN_ONES_ROWS, tq), F32),
                 jnp.zeros((1, tq), F32))
        s_cur = scores(kc_ref[...], q)
        vt_cur = vtc_ref[...]
        for c in range(n_chunks):
            r = slice(c * ATTN_TK, (c + 1) * ATTN_TK)
            s_next = scores(kl_ref[r, :], q)
            state = (exact_max_block if c == 0 else later_block)(s_cur, vt_cur, *state)
            s_cur, vt_cur = s_next, vtl_ref[:, r]
        _, acc, risk = later_block(s_cur, vt_cur, *state)
        o_t = acc[:dv] / acc[dv:dv + 1]
        o_ref[q_rows, :] = o_t.T.astype(o_ref.dtype)
        return risk

    def query_tile(t, carry):
        q_rows = pl.ds(pl.multiple_of(t * tq, tq), tq)
        risk = run(lagged_max_block, q_rows)

        @pl.when(jnp.max(risk) > ATTN_MAX_LAG_LOG2)
        def _():
            run(exact_max_block, q_rows)

        return carry

    lax.fori_loop(0, q_ref.shape[0] // tq, query_tile, 0)


def _attn_ctx_kernel(q_ref, kc_ref, vtc_ref, o_ref):
    s = lax.dot_general(q_ref[...], kc_ref[...], (((1,), (1,)), ((), ())), preferred_element_type=F32)
    p = jnp.exp2(s - jnp.max(s, axis=-1, keepdims=True))
    l = jnp.sum(p, axis=-1, keepdims=True)
    o = lax.dot_general(p.astype(BF16), vtc_ref[...], (((1,), (1,)), ((), ())), preferred_element_type=F32)
    o_ref[...] = (o / l).astype(o_ref.dtype)


def _mla_attention(q, k, vt, n_batch, n_lat, n_ctx, heads):
    m = q.shape[0]
    hd = 2 * MLA_DN
    ctx_blk0 = (n_batch * n_lat) // n_ctx
    kl = pl.BlockSpec((n_lat, hd), lambda b, h: (b, h))
    vl = pl.BlockSpec((MLA_DV, n_lat), lambda b, h: (h, b))
    kc = pl.BlockSpec((n_ctx, hd), lambda b, h: (ctx_blk0 + b, h))
    vc = pl.BlockSpec((MLA_DV, n_ctx), lambda b, h: (h, ctx_blk0 + b))
    out = pl.pallas_call(
        _attn_lat_kernel,
        grid=(n_batch, heads),
        in_specs=[pl.BlockSpec((n_lat, hd), lambda b, h: (b, h)), kl, vl, kc, vc],
        out_specs=pl.BlockSpec((n_lat, MLA_DV), lambda b, h: (b, h)),
        out_shape=jax.ShapeDtypeStruct((n_batch * n_lat, heads * MLA_DV), BF16),
        compiler_params=_cparams(("arbitrary",) * 2),
        name="mla_attn_latent",
    )(q, k, vt, k, vt)
    out_ctx = pl.pallas_call(
        _attn_ctx_kernel,
        grid=(n_batch, heads),
        in_specs=[
            pl.BlockSpec((n_ctx, hd), lambda b, h: (ctx_blk0 + b, h)),
            pl.BlockSpec((n_ctx, hd), lambda b, h: (ctx_blk0 + b, h)),
            pl.BlockSpec((MLA_DV, n_ctx), lambda b, h: (h, ctx_blk0 + b)),
        ],
        out_specs=pl.BlockSpec((n_ctx, MLA_DV), lambda b, h: (b, h)),
        out_shape=jax.ShapeDtypeStruct((n_batch * n_ctx, heads * MLA_DV), BF16),
        compiler_params=_cparams(("arbitrary",) * 2),
        name="mla_attn_ctx",
    )(q, k, vt)
    return out, out_ctx


def _sgu_kernel(x_ref, g_ref, sh_ref, sc_ref, gate_ref, win_ref, gv_ref, ws_ref, bs_ref, wo_ref,
                o_ref, h_scr, uv_scr, z_scr):
    sd = wo_ref.shape[0]
    gw = sd // SGU_GROUPS
    n_rows = x_ref.shape[0]
    h_scr[...] = _modnorm(x_ref[...], g_ref, sh_ref, sc_ref).astype(BF16)

    n_col = (2 * sd) // SGU_IN_COLS
    pieces = [(c, slice(j * SGU_IN_COLS, (j + 1) * SGU_IN_COLS))
              for c in range(n_rows // ROW_CHUNK) for j in range(n_col)]

    def matmul(p):
        c, cols = pieces[p]
        return jnp.dot(h_scr[_chunk(c), :], win_ref[:, cols], preferred_element_type=F32)

    def epilogue(p, y):
        c, cols = pieces[p]
        uv_scr[_chunk(c), cols] = jax.nn.gelu(y).astype(BF16)

    _pipelined_chunks(len(pieces), matmul, epilogue)

    for ch in range(n_rows // SGU_CHUNK):
        rows = slice(ch * SGU_CHUNK, (ch + 1) * SGU_CHUNK)
        v = uv_scr[rows, sd:].astype(F32)
        vn = (v * lax.rsqrt(jnp.mean(v * v, axis=-1, keepdims=True) + EPS) * gv_ref[...]).astype(BF16)
        for g in range(SGU_GROUPS):
            cols = slice(g * gw, (g + 1) * gw)
            z = jnp.dot(ws_ref[g], vn[:, cols], preferred_element_type=F32) + bs_ref[g]
            z_scr[rows, cols] = (uv_scr[rows, cols].astype(F32) * z).astype(BF16)
    y = jnp.dot(z_scr[...], wo_ref[...], preferred_element_type=F32)
    o_ref[...] = x_ref[...] + gate_ref[...] * y


def _sgu(x, g, mods, layer, w_in, g_v, w_s, b_s, w_out, geom):
    m, d = x.shape
    sd = w_out.shape[0]
    tpb, n_batch = geom
    tms = TM // 2
    scale = TM // tms

    def mod_spec(k):
        return pl.BlockSpec((None, None, 1, d),
                            lambda i: (layer * 6 + k, jnp.minimum(i // (tpb * scale), n_batch), 0, 0))

    def resident(shape):
        return pl.BlockSpec(shape, lambda i: (0,) * len(shape), pipeline_mode=pl.Buffered(1))

    return pl.pallas_call(
        _sgu_kernel,
        grid=(m // tms,),
        in_specs=[
            pl.BlockSpec((tms, d), lambda i: (i, 0)),
            pl.BlockSpec((1, d), lambda i: (0, 0)),
            mod_spec(0), mod_spec(1), mod_spec(2),
            resident((d, 2 * sd)),
            pl.BlockSpec((1, sd), lambda i: (0, 0)),
            pl.BlockSpec((SGU_GROUPS, SGU_CHUNK, SGU_CHUNK), lambda i: (0, 0, 0)),
            pl.BlockSpec((SGU_GROUPS, SGU_CHUNK, 1), lambda i: (0, 0, 0)),
            resident((sd, d)),
        ],
        out_specs=pl.BlockSpec((tms, d), lambda i: (i, 0)),
        out_shape=jax.ShapeDtypeStruct(x.shape, F32),
        scratch_shapes=[pltpu.VMEM((tms, d), BF16), pltpu.VMEM((tms, 2 * sd), BF16),
                        pltpu.VMEM((tms, sd), BF16)],
        input_output_aliases={0: 0},
        compiler_params=_cparams(("arbitrary",)),
        name="sgu",
    )(x, g.reshape(1, d), mods, mods, mods, w_in, g_v.reshape(1, sd), w_s,
      b_s.reshape(SGU_GROUPS, SGU_CHUNK, 1), w_out)


def _router_kernel(x_ref, g_ref, sh_ref, sc_ref, wr_ref, h_ref, idx_ref, wgt_ref, *, n_tiles):
    i = pl.program_id(0)

    @pl.when(i < n_tiles)
    def _():
        h = _modnorm(x_ref[...], g_ref, sh_ref, sc_ref)
        h_ref[...] = h.astype(h_ref.dtype)
        w = wr_ref[...]
        h_hi, w_hi = h.astype(BF16), w.astype(BF16)
        h_lo = (h - h_hi.astype(F32)).astype(BF16)
        w_lo = (w - w_hi.astype(F32)).astype(BF16)
        logits = (jnp.dot(h_hi, w_hi, preferred_element_type=F32)
                  + jnp.dot(h_lo, w_hi, preferred_element_type=F32)
                  + jnp.dot(h_hi, w_lo, preferred_element_type=F32))
        lane = lax.broadcasted_iota(jnp.int32, logits.shape, 1).astype(F32)
        neg = jnp.float32(-jnp.inf)
        big = jnp.float32(logits.shape[1])
        logits = jnp.where(lane < N_EXPERTS, logits, neg)
        m1 = jnp.max(logits, axis=-1, keepdims=True)
        i1 = jnp.min(jnp.where(logits == m1, lane, big), axis=-1, keepdims=True)
        rest = jnp.where(lane == i1, neg, logits)
        m2 = jnp.max(rest, axis=-1, keepdims=True)
        i2 = jnp.min(jnp.where(rest == m2, lane, big), axis=-1, keepdims=True)
        e2 = jnp.exp(m2 - m1)
        w1 = 1.0 / (1.0 + e2)
        w2 = e2 / (1.0 + e2)
        idx_ref[...] = jnp.where(lane == 0, i1, jnp.where(lane == 1, i2, 0.0)).astype(jnp.int32)
        wgt_ref[...] = jnp.where(lane == 0, w1, jnp.where(lane == 1, w2, 0.0))

    @pl.when(i >= n_tiles)
    def _():
        h_ref[...] = jnp.zeros_like(h_ref)


def _router(x, g, mods, layer, w_router_pad, *, n_rows, h_rows, geom):
    d = x.shape[1]
    tpb, n_batch = geom
    sh_spec, sc_spec = _mod_specs(layer, 3, tpb, n_batch)
    nt = n_rows // TM
    real = lambda i: jnp.minimum(i, nt - 1)
    one = lambda f: (lambda i: f(real(i), 0))
    lanes = w_router_pad.shape[1]
    return pl.pallas_call(
        functools.partial(_router_kernel, n_tiles=nt),
        grid=(h_rows // TM,),
        in_specs=[
            pl.BlockSpec((TM, d), lambda i: (real(i), 0)),
            pl.BlockSpec((1, d), lambda i: (0, 0)),
            pl.BlockSpec(sh_spec.block_shape, one(sh_spec.index_map)),
            pl.BlockSpec(sc_spec.block_shape, one(sc_spec.index_map)),
            pl.BlockSpec((d, lanes), lambda i: (0, 0)),
        ],
        out_specs=[
            pl.BlockSpec((TM, d), lambda i: (i, 0)),
            pl.BlockSpec((TM, lanes), lambda i: (real(i), 0)),
            pl.BlockSpec((TM, lanes), lambda i: (real(i), 0)),
        ],
        out_shape=[
            jax.ShapeDtypeStruct((h_rows, d), BF16),
            jax.ShapeDtypeStruct((n_rows, lanes), jnp.int32),
            jax.ShapeDtypeStruct((n_rows, lanes), F32),
        ],
        compiler_params=_cparams(("arbitrary",)),
        name="moe_router",
    )(x, g.reshape(1, d), mods, mods, w_router_pad)


def _moe_ffn_kernel(te_ref, tr_ref, tf_ref, x_ref, wg32_ref, wu32_ref, wd32_ref, o_ref, acc_ref,
                    wg_ref, wu_ref, wd_ref):
    del te_ref
    t = pl.program_id(0)
    f = pl.program_id(1)
    n_rows = tr_ref[t]

    @pl.when(f == 0)
    def _():
        acc_ref[...] = jnp.zeros_like(acc_ref)

    @pl.when(tf_ref[t] == 1)
    def _():
        wg_ref[f] = wg32_ref[...].astype(BF16)
        wu_ref[f] = wu32_ref[...].astype(BF16)
        wd_ref[f] = wd32_ref[...].astype(BF16)

    def matmul(c):
        x = x_ref[_chunk(c), :]
        return (jnp.dot(x, wg_ref[f], preferred_element_type=F32),
                jnp.dot(x, wu_ref[f], preferred_element_type=F32))

    def epilogue(c, gu):
        a = (_silu(gu[0]) * gu[1]).astype(BF16)
        acc_ref[_chunk(c), :] += jnp.dot(a, wd_ref[f], preferred_element_type=F32)

    for n_chunks in range(1, MOE_TM // ROW_CHUNK + 1):
        @pl.when((n_rows > (n_chunks - 1) * ROW_CHUNK) & (n_rows <= n_chunks * ROW_CHUNK))
        def _(n_chunks=n_chunks):
            _pipelined_chunks(n_chunks, matmul, epilogue)

    @pl.when(f == pl.num_programs(1) - 1)
    def _():
        o_ref[...] = acc_ref[...].astype(o_ref.dtype)


def _moe_ffn(xs, tile_expert, tile_rows, tile_first, w_gu, w_down, lf):
    r, d = xs.shape
    ff = w_down.shape[2]
    nf = ff // MOE_BF
    n_tiles = r // MOE_TM

    def fidx(t, f, tf):
        return jnp.where(tf[t] == 1, f, nf - 1)

    grid_spec = pltpu.PrefetchScalarGridSpec(
        num_scalar_prefetch=3,
        grid=(n_tiles, nf),
        in_specs=[
            pl.BlockSpec((MOE_TM, d), lambda t, f, te, tr, tf: (t, 0)),
            pl.BlockSpec((None, None, d, MOE_BF), lambda t, f, te, tr, tf: (lf, te[t], 0, fidx(t, f, tf))),
            pl.BlockSpec((None, None, d, MOE_BF),
                         lambda t, f, te, tr, tf: (lf, te[t], 0, nf + fidx(t, f, tf))),
            pl.BlockSpec((None, None, MOE_BF, d), lambda t, f, te, tr, tf: (lf, te[t], fidx(t, f, tf), 0)),
        ],
        out_specs=pl.BlockSpec((MOE_TM, d), lambda t, f, te, tr, tf: (t, 0)),
        scratch_shapes=[pltpu.VMEM((MOE_TM, d), F32), pltpu.VMEM((nf, d, MOE_BF), BF16),
                        pltpu.VMEM((nf, d, MOE_BF), BF16), pltpu.VMEM((nf, MOE_BF, d), BF16)],
    )
    return pl.pallas_call(
        _moe_ffn_kernel,
        grid_spec=grid_spec,
        out_shape=jax.ShapeDtypeStruct((r, d), BF16),
        compiler_params=_cparams(("arbitrary", "arbitrary")),
        name="moe_ffn",
    )(tile_expert, tile_rows, tile_first, xs, w_gu, w_gu, w_down)


def _combine_kernel(x_ref, y0_ref, y1_ref, wgt_ref, gate_ref, gf_ref, o_ref, *, final):
    w = wgt_ref[...]
    y = w[:, 0:1] * y0_ref[...].astype(F32) + w[:, 1:2] * y1_ref[...].astype(F32)
    xn = x_ref[...] + gate_ref[...] * y
    if final:
        xn = xn * lax.rsqrt(jnp.mean(xn * xn, axis=-1, keepdims=True) + EPS) * gf_ref[...]
    o_ref[...] = xn


def _combine(x, y0, y1, wgt, mods, layer, g_final, *, n_rows, geom, final):
    d = x.shape[1]
    tpb, n_batch = geom
    lanes = wgt.shape[1]
    gate_spec = pl.BlockSpec((None, None, 1, d),
                             lambda i: (layer * 6 + 5, jnp.minimum(i // tpb, n_batch), 0, 0))
    row = lambda w: pl.BlockSpec((TM, w), lambda i: (i, 0))
    out_rows = n_rows if final else x.shape[0]
    return pl.pallas_call(
        functools.partial(_combine_kernel, final=final),
        grid=(n_rows // TM,),
        in_specs=[row(d), row(d), row(d), row(lanes), gate_spec, pl.BlockSpec((1, d), lambda i: (0, 0))],
        out_specs=row(d),
        out_shape=jax.ShapeDtypeStruct((out_rows, d), F32),
        input_output_aliases={} if final else {0: 0},
        compiler_params=_cparams(("arbitrary",)),
        name="moe_combine",
    )(x, y0, y1, wgt, mods, g_final.reshape(1, d))


def _moe_layer(x, g, mods, layer, w_router, w_gu, w_down, lf, g_final, *, n_rows, geom, final):
    d = x.shape[1]
    wr_pad = jnp.zeros((d, 128), F32).at[:, :N_EXPERTS].set(w_router)
    n_assign = 2 * n_rows
    r_rows = n_assign + (N_EXPERTS - 1) * MOE_TM
    h, idx, wgt = _router(x, g, mods, layer, wr_pad, n_rows=n_rows, h_rows=r_rows, geom=geom)
    e_flat = idx[:, :2].reshape(-1)
    onehot = (e_flat[:, None] == jnp.arange(N_EXPERTS, dtype=jnp.int32)[None, :]).astype(F32)
    blk = onehot.reshape(n_assign // CUMSUM_BLOCK, CUMSUM_BLOCK, N_EXPERTS)
    tril = jnp.tril(jnp.ones((CUMSUM_BLOCK, CUMSUM_BLOCK), F32))
    within = jnp.einsum("ij,bjk->bik", tril, blk)
    blk_tot = within[:, -1, :]
    blk_off = jnp.cumsum(blk_tot, axis=0) - blk_tot
    csum = (within + blk_off[:, None, :]).reshape(n_assign, N_EXPERTS)
    rank = (jnp.sum(onehot * csum, axis=1) - 1.0).astype(jnp.int32)
    counts = jnp.sum(blk_tot, axis=0).astype(jnp.int32)
    padded = ((counts + MOE_TM - 1) // MOE_TM) * MOE_TM
    ends = jnp.cumsum(padded)
    starts = ends - padded
    dest = starts[e_flat] + rank
    n_tiles = r_rows // MOE_TM
    hit = jnp.zeros((r_rows,), jnp.int32).at[dest].add(jnp.arange(n_assign, dtype=jnp.int32) // 2 + 1)
    src = jnp.where(hit > 0, hit - 1, jnp.arange(r_rows, dtype=jnp.int32) % n_rows)
    tile_start = jnp.arange(n_tiles, dtype=jnp.int32) * MOE_TM
    last_tile = jnp.maximum(ends[-1] // MOE_TM - 1, 0)
    ts_used = jnp.minimum(tile_start, last_tile * MOE_TM)
    tile_expert = jnp.sum((ends[None, :] <= ts_used[:, None]).astype(jnp.int32), axis=1)
    tile_expert = jnp.minimum(tile_expert, N_EXPERTS - 1)
    real_end = (starts + counts)[tile_expert]
    used = tile_start < ends[-1]
    tile_rows = jnp.where(used, jnp.clip(real_end - tile_start, 0, MOE_TM), 0).astype(jnp.int32)
    tile_first = (used & (tile_start == starts[tile_expert])).astype(jnp.int32)
    xs = jnp.take(h, src, axis=0, mode="clip")
    ys = _moe_ffn(xs, tile_expert, tile_rows, tile_first, w_gu, w_down, lf)
    d2 = dest.reshape(n_rows, 2)
    y0 = jnp.take(ys, d2[:, 0], axis=0, mode="clip")
    y1 = jnp.take(ys, d2[:, 1], axis=0, mode="clip")
    return _combine(x, y0, y1, wgt, mods, layer, g_final, n_rows=n_rows, geom=geom, final=final)


def kernel(x, c, ctx, c_ctx, ada_w, ada_b, norm_mix_g, norm_ffn_g, final_norm_g, ret_w_in, ret_w_o, ret_decay_f, ret_decay_b, ret_gn_f, ret_gn_b, mla_w_down, mla_g_q, mla_g_kv, mla_w_uq, mla_w_ukv, mla_w_o, sgu_w_in, sgu_g_v, sgu_w_s, sgu_b_s, sgu_w_out, ffn_w_gu, ffn_w_down, moe_w_router, moe_w_gu, moe_w_down):
    n_batch, n_lat, d = x.shape
    n_ctx = ctx.shape[1]
    depth = ada_w.shape[0]
    rows_lat = n_batch * n_lat
    m = rows_lat + n_batch * n_ctx
    geom = (n_lat // TM, n_batch)
    assert n_batch * n_ctx == TM and d == 1024 and n_batch + 1 <= 8

    xt = (x.reshape(rows_lat, d), ctx.reshape(n_batch * n_ctx, d))
    cond = jnp.concatenate([c, c_ctx[None, :], jnp.zeros((8 - n_batch - 1, d), F32)], axis=0)
    mods = _adaln(cond, ada_w, ada_b)
    mods = mods.reshape(depth, 8, 6, d).transpose(0, 2, 1, 3).reshape(depth * 6, 8, 1, d)

    grid_rows = n_lat // GRID_W
    ret_cos, ret_sin = _rope_tables(grid_rows, n_batch * n_ctx, RET_DK)
    mla_cos, mla_sin = _rope_tables(grid_rows, n_batch * n_ctx, MLA_DR)
    pad_r = MLA_DN - MLA_DR
    mla_cos = jnp.concatenate([mla_cos, jnp.ones((mla_cos.shape[0], pad_r), F32)], axis=1)
    mla_sin = jnp.concatenate([mla_sin, jnp.zeros((mla_sin.shape[0], pad_r), F32)], axis=1)

    for i in range(depth):
        last = i == depth - 1
        kind, jm = i % N_MIXERS, i // N_MIXERS
        rows_out = rows_lat if last else m
        if kind == 0:
            qk, vv, gates = _ret_in(xt, norm_mix_g[i], mods, i, ret_w_in[jm].astype(BF16),
                                    (ret_cos, ret_sin), geom=geom)
            lg = jnp.stack([jax.nn.log_sigmoid(ret_decay_f[jm].astype(F32)),
                            jax.nn.log_sigmoid(ret_decay_b[jm].astype(F32))])
            hv = vv.shape[1]
            gn = jnp.stack([ret_gn_f[jm], ret_gn_b[jm]]).reshape(2, 1, hv)
            yf, yb = _retention(qk, vv, gates, lg, gn, n_batch, n_lat, n_ctx)
            xt = _outproj([yf, yb], ret_w_o[jm].astype(BF16), xt, mods, i, 2, n_rows=rows_out, geom=geom)
        elif kind == 1:
            heads = mla_w_o.shape[1] // MLA_DV
            lq, lkv = mla_g_q.shape[1], mla_g_kv.shape[1]
            n_dn = lq + lkv + MLA_DN
            wd = jnp.zeros((d, n_dn), F32).at[:, :lq + lkv + MLA_DR].set(mla_w_down[jm]).astype(BF16)
            dn = _proj(xt, norm_mix_g[i], mods, i, 0, wd, mode="f32", bn=n_dn, out_dtype=F32,
                       n_rows=m, geom=geom)
            wq = mla_w_uq[jm].reshape(lq, heads, MLA_DN + MLA_DR)
            wq = jnp.concatenate([wq, jnp.zeros((lq, heads, pad_r), F32)], axis=2)
            wq = wq.reshape(lq, heads * 2 * MLA_DN).astype(BF16)
            wkv = mla_w_ukv[jm].reshape(lkv, heads, MLA_DN + MLA_DV)
            wk = wkv[:, :, :MLA_DN].reshape(lkv, heads * MLA_DN).astype(BF16)
            wv = wkv[:, :, MLA_DN:].reshape(lkv, heads * MLA_DV).T.astype(BF16)
            tpb = geom[0]
            n_lat_tiles = tpb * n_batch
            pos_idx = lambda r: (jnp.where(r < n_lat_tiles, r % tpb, tpb + (r - n_lat_tiles)), 0)
            full = lambda a: pl.BlockSpec(a.shape, lambda r: (0,) * a.ndim)
            gq = mla_g_q[jm].reshape(1, lq)
            gkv = mla_g_kv[jm].reshape(1, lkv)
            scale = float((MLA_DN + MLA_DR) ** -0.5 * LOG2_E)
            q, k, vt = pl.pallas_call(
                functools.partial(_mla_up_kernel, heads=heads, scale=scale),
                grid=(m // TM,),
                in_specs=[pl.BlockSpec((TM, n_dn), lambda r: (r, 0)), full(gq), full(gkv), full(wq),
                          full(wk), full(wv), pl.BlockSpec((TM, MLA_DN), pos_idx),
                          pl.BlockSpec((TM, MLA_DN), pos_idx)],
                out_specs=[pl.BlockSpec((TM, heads * 2 * MLA_DN), lambda r: (r, 0)),
                           pl.BlockSpec((TM, heads * 2 * MLA_DN), lambda r: (r, 0)),
                           pl.BlockSpec((heads * MLA_DV, TM), lambda r: (0, r))],
                out_shape=[jax.ShapeDtypeStruct((m, heads * 2 * MLA_DN), BF16),
                           jax.ShapeDtypeStruct((m, heads * 2 * MLA_DN), BF16),
                           jax.ShapeDtypeStruct((heads * MLA_DV, m), BF16)],
                compiler_params=_cparams(("arbitrary",)),
                name="mla_up",
            )(dn, gq, gkv, wq, wk, wv, mla_cos, mla_sin)
            ao = _mla_attention(q, k, vt, n_batch, n_lat, n_ctx, heads)
            xt = _outproj([ao], mla_w_o[jm].astype(BF16), xt, mods, i, 2, n_rows=rows_out, geom=geom)
        else:
            xt = _sgu(xt, norm_mix_g[i], mods, i, sgu_w_in[jm].astype(BF16), sgu_g_v[jm],
                      sgu_w_s[jm].astype(BF16), sgu_b_s[jm], sgu_w_out[jm].astype(BF16), geom)
        f = i // 2
        if i % 2 == 0:
            dff = ffn_w_down.shape[1]
            wgu = ffn_w_gu[f].astype(BF16)
            xt = _dense_ffn(xt, norm_ffn_g[i], mods, i, wgu[:, :dff], wgu[:, dff:],
                            ffn_w_down[f].astype(BF16), n_rows=rows_out, geom=geom)
        else:
            xt = _moe_layer(xt, norm_ffn_g[i], mods, i, moe_w_router[f], moe_w_gu, moe_w_down, f,
                            final_norm_g, n_rows=rows_out, geom=geom, final=last)
    if not (depth - 1) % 2:
        raise NotImplementedError("final norm is fused into the expert combine of the last layer")
    return xt.reshape(n_batch, n_lat, d)
```

```python
import functools

import jax
import jax.numpy as jnp
from jax import lax
from jax.experimental import pallas as pl
from jax.experimental.pallas import tpu as pltpu

F32 = jnp.float32
BF16 = jnp.bfloat16
HIGHEST = lax.Precision.HIGHEST
LOG2_E = 1.4426950408889634

GRID_W = 64
ROPE_BASE = 10000.0
EPS = 1e-6
RET_DK = 256
RET_DV = 512
MLA_DN = 128
MLA_DR = 64
MLA_DV = 128
SGU_GROUPS = 8
SGU_CHUNK = 128
N_EXPERTS = 8
N_MIXERS = 3

TM = 1024
ROW_CHUNK = 256
RET_CHUNK = 256
RET_HEADS_PER_STEP = 4
RET_IN_STEPS = 4
ATTN_TQ = 512
ATTN_TK = 1024
ATTN_ONES_ROWS = 16
ATTN_MAX_LAG_LOG2 = 60.0
MOE_TM = 1024
MOE_BF = 512
CUMSUM_BLOCK = 256
SGU_IN_COLS = 1024
VMEM_LIMIT = 56 * 1024 * 1024


def _cparams(sem):
    return pltpu.CompilerParams(dimension_semantics=sem, vmem_limit_bytes=VMEM_LIMIT)


def _silu(x):
    return x * jax.nn.sigmoid(x)


def _rope(x, cos, sin_signed, half):
    w = x.shape[-1]
    lane = lax.broadcasted_iota(jnp.int32, x.shape, 1)
    first = (lane % (2 * half)) < half
    rot = jnp.where(first, pltpu.roll(x, w - half, 1), pltpu.roll(x, half, 1))
    return x * cos + rot * sin_signed


def _rope_tables(rows, n_ctx_rows, rot_dim):
    row = jnp.repeat(jnp.arange(rows, dtype=F32), GRID_W)
    col = jnp.tile(jnp.arange(GRID_W, dtype=F32), rows)
    axis_dim = rot_dim // 2
    inv_freq = ROPE_BASE ** (-jnp.arange(0, axis_dim, 2, dtype=F32) / axis_dim)
    ang_r = row[:, None] * inv_freq
    ang_c = col[:, None] * inv_freq
    ang = jnp.concatenate([ang_r, ang_r, ang_c, ang_c], axis=-1)
    quarter = rot_dim // 4
    sign = jnp.where((jnp.arange(rot_dim) % (2 * quarter)) < quarter, -1.0, 1.0).astype(F32)
    cos = jnp.concatenate([jnp.cos(ang), jnp.ones((n_ctx_rows, rot_dim), F32)], axis=0)
    sin = jnp.concatenate([jnp.sin(ang) * sign, jnp.zeros((n_ctx_rows, rot_dim), F32)], axis=0)
    return cos, sin


def _adaln_kernel(cond_ref, w_ref, b_ref, o_ref):
    s = _silu(cond_ref[...])
    o_ref[...] = jnp.dot(s, w_ref[...], precision=HIGHEST, preferred_element_type=F32) + b_ref[...]


def _adaln(cond, ada_w, ada_b):
    depth, d, n6 = ada_w.shape
    bn = n6 // 4
    return pl.pallas_call(
        _adaln_kernel,
        grid=(depth, n6 // bn),
        in_specs=[
            pl.BlockSpec((8, d), lambda l, j: (0, 0)),
            pl.BlockSpec((None, d, bn), lambda l, j: (l, 0, j)),
            pl.BlockSpec((None, 1, bn), lambda l, j: (l, 0, j)),
        ],
        out_specs=pl.BlockSpec((None, 8, bn), lambda l, j: (l, 0, j)),
        out_shape=jax.ShapeDtypeStruct((depth, 8, n6), F32),
        compiler_params=_cparams(("arbitrary", "arbitrary")),
        name="adaln",
    )(cond, ada_w, ada_b.reshape(depth, 1, n6))


def _row_specs(src, width, n_main_tiles):
    if isinstance(src, tuple):
        specs = [pl.BlockSpec((TM, width), lambda i, *_: (jnp.minimum(i, n_main_tiles - 1), 0)),
                 pl.BlockSpec((TM, width), lambda i, *_: (jnp.maximum(i - n_main_tiles, 0), 0))]
        return specs, list(src)
    return [pl.BlockSpec((TM, width), lambda i, *_: (i, 0))], [src]


def _row_tile(refs, i, n_main_tiles):
    if len(refs) == 2:
        return jnp.where(i < n_main_tiles, refs[0][...], refs[1][...])
    return refs[0][...]


def _modnorm(xf, g_ref, shift_ref, scale_ref):
    y = xf * lax.rsqrt(jnp.mean(xf * xf, axis=-1, keepdims=True) + EPS)
    return (y * g_ref[...]) * (1.0 + scale_ref[...]) + shift_ref[...]


def _chunk(c):
    return slice(c * ROW_CHUNK, (c + 1) * ROW_CHUNK)


def _pipelined_chunks(n_chunks, matmul, epilogue):
    pending = matmul(0)
    for c in range(n_chunks):
        following = matmul(c + 1) if c + 1 < n_chunks else None
        epilogue(c, pending)
        pending = following


def _ret_in_kernel(*refs, n_q, x_count, n_main_tiles):
    x_refs = refs[:x_count]
    (g_ref, sh_ref, sc_ref, wqk_ref, wv_ref, wg_ref, cos_ref, sin_ref,
     qk_ref, v_ref, gate_ref, h_scr) = refs[x_count:]
    j = pl.program_id(1)

    @pl.when(j == 0)
    def _():
        xf = _row_tile(x_refs, pl.program_id(0), n_main_tiles)
        h_scr[...] = _modnorm(xf, g_ref, sh_ref, sc_ref).astype(BF16)

    kscale = jnp.where(j >= n_q, RET_DK ** -0.5, 1.0).astype(F32)

    def matmul(c):
        h = h_scr[_chunk(c), :]
        return tuple(jnp.dot(h, w[...], preferred_element_type=F32) for w in (wqk_ref, wv_ref, wg_ref))

    def epilogue(c, ys):
        r = _chunk(c)
        for hs in range(ys[0].shape[1] // RET_DK):
            cols = slice(hs * RET_DK, (hs + 1) * RET_DK)
            qk = _rope(ys[0][:, cols], cos_ref[r, :], sin_ref[r, :], RET_DK // 4)
            qk_ref[r, cols] = (qk * kscale).astype(qk_ref.dtype)
        v_ref[r, :] = ys[1].astype(v_ref.dtype)
        gate_ref[r, :] = _silu(ys[2]).astype(gate_ref.dtype)

    _pipelined_chunks(h_scr.shape[0] // ROW_CHUNK, matmul, epilogue)


def _ret_in(x, g, mods, layer, w_in, rope, *, geom):
    x_parts = list(x) if isinstance(x, tuple) else [x]
    m = sum(p.shape[0] for p in x_parts)
    d = x_parts[0].shape[1]
    heads = d // RET_DK
    nj = RET_IN_STEPS
    n_q = nj // 2
    qk_cols = (2 * heads * RET_DK) // nj
    wv_cols = (heads * RET_DV) // nj
    wg_cols = (2 * heads * RET_DV) // nj
    assert wv_cols == qk_cols and wg_cols == 2 * qk_cols and qk_cols % RET_DK == 0
    tpb, n_batch = geom
    sh_spec, sc_spec = _mod_specs(layer, 0, tpb, n_batch)
    n_lat_tiles = tpb * n_batch
    cos, sin = rope

    def pos_idx(i, j):
        return (jnp.where(i < n_lat_tiles, i % tpb, tpb + (i - n_lat_tiles)), 0)

    x_specs, x_arrs = _row_specs(x, d, n_lat_tiles)
    return pl.pallas_call(
        functools.partial(_ret_in_kernel, n_q=n_q, x_count=len(x_arrs), n_main_tiles=n_lat_tiles),
        grid=(m // TM, nj),
        in_specs=x_specs + [
            pl.BlockSpec((1, d), lambda i, j: (0, 0)),
            sh_spec, sc_spec,
            pl.BlockSpec((d, qk_cols), lambda i, j: (0, j)),
            pl.BlockSpec((d, qk_cols), lambda i, j: (0, nj + j)),
            pl.BlockSpec((d, 2 * qk_cols), lambda i, j: (0, nj + j)),
            pl.BlockSpec((TM, RET_DK), pos_idx),
            pl.BlockSpec((TM, RET_DK), pos_idx),
        ],
        out_specs=[
            pl.BlockSpec((TM, qk_cols), lambda i, j: (i, j)),
            pl.BlockSpec((TM, qk_cols), lambda i, j: (i, j)),
            pl.BlockSpec((TM, 2 * qk_cols), lambda i, j: (i, j)),
        ],
        out_shape=[
            jax.ShapeDtypeStruct((m, 2 * heads * RET_DK), BF16),
            jax.ShapeDtypeStruct((m, heads * RET_DV), BF16),
            jax.ShapeDtypeStruct((m, 2 * heads * RET_DV), BF16),
        ],
        scratch_shapes=[pltpu.VMEM((TM, d), BF16)],
        compiler_params=_cparams(("arbitrary", "arbitrary")),
        name="ret_in",
    )(*x_arrs, g.reshape(1, d), mods, mods, w_in, w_in, w_in, cos, sin)


def _proj_kernel(x_ref, g_ref, sh_ref, sc_ref, w_ref, o_ref, h_scr, *, mode):
    j = pl.program_id(1)

    @pl.when(j == 0)
    def _():
        h_scr[...] = _modnorm(x_ref[...], g_ref, sh_ref, sc_ref).astype(BF16)

    def matmul(c):
        return jnp.dot(h_scr[_chunk(c), :], w_ref[...], preferred_element_type=F32)

    def epilogue(c, y):
        assert mode == "f32"
        o_ref[_chunk(c), :] = y

    _pipelined_chunks(h_scr.shape[0] // ROW_CHUNK, matmul, epilogue)


def _mod_specs(layer, k_shift, tiles_per_batch, n_batch):
    def spec(k):
        return pl.BlockSpec((None, None, 1, 1024),
                            lambda i, j: (layer * 6 + k, jnp.minimum(i // tiles_per_batch, n_batch), 0, 0))
    return spec(k_shift), spec(k_shift + 1)


def _proj(x, g, mods, layer, k_shift, w, *, mode, bn, out_dtype, n_rows, geom):
    d = x.shape[1]
    n = w.shape[1]
    nt = n_rows // TM
    tpb, n_batch = geom
    sh_spec, sc_spec = _mod_specs(layer, k_shift, tpb, n_batch)
    in_specs = [
        pl.BlockSpec((TM, d), lambda i, j: (i, 0)),
        pl.BlockSpec((1, d), lambda i, j: (0, 0)),
        sh_spec, sc_spec,
        pl.BlockSpec((d, bn), lambda i, j: (0, j)),
    ]
    args = [x, g.reshape(1, d), mods, mods, w]
    return pl.pallas_call(
        functools.partial(_proj_kernel, mode=mode),
        grid=(nt, n // bn),
        in_specs=in_specs,
        out_specs=pl.BlockSpec((TM, bn), lambda i, j: (i, j)),
        out_shape=jax.ShapeDtypeStruct((n_rows, n), out_dtype),
        scratch_shapes=[pltpu.VMEM((TM, d), BF16)],
        compiler_params=_cparams(("arbitrary", "arbitrary")),
        name="proj_" + mode,
    )(*args)


def _dense_ffn_kernel(x_ref, g_ref, sh_ref, sc_ref, gate_ref, wg_ref, wu_ref, wd_ref, o_ref, h_scr):
    h_scr[...] = _modnorm(x_ref[...], g_ref, sh_ref, sc_ref).astype(BF16)

    def matmul(c):
        h = h_scr[_chunk(c), :]
        return (jnp.dot(h, wg_ref[...], preferred_element_type=F32),
                jnp.dot(h, wu_ref[...], preferred_element_type=F32))

    def epilogue(c, gu):
        a = (_silu(gu[0]) * gu[1]).astype(BF16)
        y = jnp.dot(a, wd_ref[...], preferred_element_type=F32)
        o_ref[_chunk(c), :] = x_ref[_chunk(c), :] + gate_ref[...] * y

    _pipelined_chunks(h_scr.shape[0] // ROW_CHUNK, matmul, epilogue)


def _dense_ffn(x, g, mods, layer, w_g, w_u, w_d, *, n_rows, geom):
    d = x.shape[1]
    dff = w_d.shape[0]
    tpb, n_batch = geom

    def mod_spec(k):
        return pl.BlockSpec((None, None, 1, d),
                            lambda i: (layer * 6 + k, jnp.minimum(i // tpb, n_batch), 0, 0))

    def resident(shape):
        return pl.BlockSpec(shape, lambda i: (0, 0), pipeline_mode=pl.Buffered(1))

    return pl.pallas_call(
        _dense_ffn_kernel,
        grid=(n_rows // TM,),
        in_specs=[
            pl.BlockSpec((TM, d), lambda i: (i, 0)),
            pl.BlockSpec((1, d), lambda i: (0, 0)),
            mod_spec(3), mod_spec(4), mod_spec(5),
            resident((d, dff)), resident((d, dff)), resident((dff, d)),
        ],
        out_specs=pl.BlockSpec((TM, d), lambda i: (i, 0)),
        out_shape=jax.ShapeDtypeStruct(x.shape, F32),
        scratch_shapes=[pltpu.VMEM((TM, d), BF16)],
        input_output_aliases={0: 0},
        compiler_params=_cparams(("arbitrary",)),
        name="dense_ffn",
    )(x, g.reshape(1, d), mods, mods, mods, w_g, w_u, w_d)


def _outproj_kernel(*refs, a_counts, x_count, n_main_tiles):
    i = pl.program_id(0)
    refs = list(refs)
    a_tiles = []
    for cnt in a_counts:
        a_tiles.append(_row_tile(refs[:cnt], i, n_main_tiles))
        refs = refs[cnt:]
    w_ref = refs[0]
    x = _row_tile(refs[1:1 + x_count], i, n_main_tiles)
    gate_ref, o_ref = refs[1 + x_count:]
    if len(a_tiles) == 1:
        a = a_tiles[0]
    else:
        a = (a_tiles[0].astype(F32) + a_tiles[1].astype(F32)).astype(BF16)
    y = jnp.dot(a, w_ref[...], preferred_element_type=F32)
    o_ref[...] = x + gate_ref[...] * y


def _outproj(a_list, w, x, mods, layer, k_gate, *, n_rows, geom):
    k, d = w.shape
    nt = n_rows // TM
    tpb, n_batch = geom
    n_main = tpb * n_batch
    gate_spec = pl.BlockSpec((None, None, 1, d),
                             lambda i: (layer * 6 + k_gate, jnp.minimum(i // tpb, n_batch), 0, 0))
    in_specs, args, a_counts = [], [], []
    for a in a_list:
        specs, arrs = _row_specs(a, k, n_main)
        in_specs += specs
        args += arrs
        a_counts.append(len(arrs))
    x_specs, x_arrs = _row_specs(x, d, n_main)
    total_rows = sum(xa.shape[0] for xa in x_arrs)
    return pl.pallas_call(
        functools.partial(_outproj_kernel, a_counts=tuple(a_counts), x_count=len(x_arrs), n_main_tiles=n_main),
        grid=(nt,),
        in_specs=in_specs + [pl.BlockSpec((k, d), lambda i: (0, 0))] + x_specs + [gate_spec],
        out_specs=pl.BlockSpec((TM, d), lambda i: (i, 0)),
        out_shape=jax.ShapeDtypeStruct((total_rows, d), F32),
        input_output_aliases={len(args) + 1: 0} if len(x_arrs) == 1 else {},
        compiler_params=_cparams(("arbitrary",)),
        name="outproj",
    )(*args, w, *x_arrs, mods)


def _ret_kernel(lg_ref, qf_ref, kf_ref, vf_ref, gf_ref, qb_ref, kb_ref, vb_ref, gb_ref, gnf_ref, gnb_ref,
                of_ref, ob_ref, s_scr, intra_scr, dec_scr):
    hp = pl.program_id(1)
    s = pl.program_id(2)
    c = RET_CHUNK
    nh = RET_HEADS_PER_STEP
    chains = [(d, j) for d in range(2) for j in range(nh)]

    @pl.when(s == 0)
    def _():
        s_scr[...] = jnp.zeros_like(s_scr)
        ii = lax.broadcasted_iota(jnp.int32, (c, c), 0)
        jj = lax.broadcasted_iota(jnp.int32, (c, c), 1)
        pos = lax.broadcasted_iota(jnp.int32, (c, 1), 0).astype(F32)
        for n, (d, j) in enumerate(chains):
            lg = lg_ref[d, hp * nh + j]
            diff = ((ii - jj) if d == 0 else (jj - ii)).astype(F32)
            intra_scr[n] = jnp.where(diff >= 0, jnp.exp(lg * jnp.maximum(diff, 0.0)), 0.0)
            fpos = pos if d == 0 else (c - 1.0) - pos
            dec_scr[2 * n] = jnp.exp(lg * (fpos + 1.0))
            dec_scr[2 * n + 1] = jnp.exp(lg * ((c - 1.0) - fpos))

    refs = ((qf_ref, kf_ref, vf_ref, gf_ref, gnf_ref, of_ref), (qb_ref, kb_ref, vb_ref, gb_ref, gnb_ref, ob_ref))
    qcols = [slice(j * RET_DK, (j + 1) * RET_DK) for _, j in chains]
    vcols = [slice(j * RET_DV, (j + 1) * RET_DV) for _, j in chains]
    n_chains = range(len(chains))
    q = [refs[d][0][:, qcols[n]] for n, (d, _) in enumerate(chains)]
    k = [refs[d][1][:, qcols[n]] for n, (d, _) in enumerate(chains)]
    v = [refs[d][2][:, vcols[n]] for n, (d, _) in enumerate(chains)]
    contract_last = (((1,), (1,)), ((), ()))
    contract_rows = (((0,), (0,)), ((), ()))
    scores = [lax.dot_general(q[n], k[n], contract_last, preferred_element_type=F32) for n in n_chains]
    masked = [(scores[n] * intra_scr[n]).astype(BF16) for n in n_chains]
    state = [s_scr[n] for n in n_chains]
    qd = [(q[n].astype(F32) * dec_scr[2 * n]).astype(BF16) for n in n_chains]
    o = [jnp.dot(masked[n], v[n], preferred_element_type=F32)
         + jnp.dot(qd[n], state[n].astype(BF16), preferred_element_type=F32) for n in n_chains]
    kd = [(k[n].astype(F32) * dec_scr[2 * n + 1]).astype(BF16) for n in n_chains]
    for n, (d, j) in enumerate(chains):
        chunk_decay = jnp.exp(lg_ref[d, hp * nh + j] * float(c))
        s_scr[n] = state[n] * chunk_decay + lax.dot_general(kd[n], v[n], contract_rows,
                                                           preferred_element_type=F32)
    for n, (d, _) in enumerate(chains):
        gate_ref, gn_ref, o_ref = refs[d][3:]
        mu = jnp.mean(o[n], axis=-1, keepdims=True)
        oc = o[n] - mu
        var = jnp.mean(oc * oc, axis=-1, keepdims=True)
        y = oc * lax.rsqrt(var + EPS) * gn_ref[:, vcols[n]]
        o_ref[:, vcols[n]] = (gate_ref[:, vcols[n]].astype(F32) * y).astype(o_ref.dtype)


def _retention(qk, v, gates, lg, gn, n_batch, n_lat, n_ctx):
    m = qk.shape[0]
    heads = v.shape[1] // RET_DV
    c = RET_CHUNK
    lat_chunks = n_lat // c
    assert n_ctx == c
    steps = 1 + lat_chunks
    ctx_blk0 = (n_batch * n_lat) // c

    def row_blk(d, b, s):
        t = (s - 1) if d == 0 else (lat_chunks - s)
        return jnp.where(s == 0, ctx_blk0 + b, b * lat_chunks + t)

    nh = RET_HEADS_PER_STEP
    groups = heads // nh
    qw, vw = nh * RET_DK, nh * RET_DV

    def specs(d):
        return [
            pl.BlockSpec((c, qw), lambda b, h, s: (row_blk(d, b, s), h)),
            pl.BlockSpec((c, qw), lambda b, h, s: (row_blk(d, b, s), groups + h)),
            pl.BlockSpec((c, vw), lambda b, h, s: (row_blk(d, b, s), h)),
            pl.BlockSpec((c, vw), lambda b, h, s: (row_blk(d, b, s), groups * d + h)),
        ]

    def gn_spec(d):
        return pl.BlockSpec((None, 1, vw), lambda b, h, s: (d, 0, h))

    shape = jax.ShapeDtypeStruct((m, heads * RET_DV), BF16)
    yf, yb = pl.pallas_call(
        _ret_kernel,
        grid=(n_batch, groups, steps),
        in_specs=[pl.BlockSpec(memory_space=pltpu.SMEM)] + specs(0) + specs(1) + [gn_spec(0), gn_spec(1)],
        out_specs=[pl.BlockSpec((c, vw), lambda b, h, s: (row_blk(0, b, s), h)),
                   pl.BlockSpec((c, vw), lambda b, h, s: (row_blk(1, b, s), h))],
        out_shape=[shape, shape],
        scratch_shapes=[pltpu.VMEM((2 * nh, RET_DK, RET_DV), F32), pltpu.VMEM((2 * nh, c, c), F32),
                        pltpu.VMEM((4 * nh, c, 1), F32)],
        compiler_params=_cparams(("arbitrary",) * 3),
        name="retention",
    )(lg, qk, qk, v, gates, qk, qk, v, gates, gn, gn)
    return yf, yb


def _mla_up_kernel(dn_ref, gq_ref, gkv_ref, wq_ref, wk_ref, wv_ref, cos_ref, sin_ref,
                   q_ref, k_ref, v_ref, *, heads, scale):
    dn = dn_ref[...]
    lq = gq_ref.shape[1]
    lkv = gkv_ref.shape[1]

    def rms(a, g):
        return (a * lax.rsqrt(jnp.mean(a * a, axis=-1, keepdims=True) + EPS) * g).astype(BF16)

    cq = rms(dn[:, :lq], gq_ref[...])
    ckv = rms(dn[:, lq:lq + lkv], gkv_ref[...])
    cos = cos_ref[...]
    sin = sin_ref[...]
    half = MLA_DR // 4
    q = jnp.dot(cq, wq_ref[...], preferred_element_type=F32)
    kn = jnp.dot(ckv, wk_ref[...], preferred_element_type=F32)
    v_ref[...] = lax.dot_general(wv_ref[...], ckv, (((1,), (1,)), ((), ())),
                                 preferred_element_type=F32).astype(v_ref.dtype)
    kr = _rope(dn[:, lq + lkv:], cos, sin, half).astype(k_ref.dtype)
    for h in range(heads):
        o = h * 2 * MLA_DN
        q_ref[:, o:o + MLA_DN] = (q[:, o:o + MLA_DN] * scale).astype(q_ref.dtype)
        qr = _rope(q[:, o + MLA_DN:o + 2 * MLA_DN], cos, sin, half)
        q_ref[:, o + MLA_DN:o + 2 * MLA_DN] = (qr * scale).astype(q_ref.dtype)
        k_ref[:, o:o + MLA_DN] = kn[:, h * MLA_DN:(h + 1) * MLA_DN].astype(k_ref.dtype)
        k_ref[:, o + MLA_DN:o + 2 * MLA_DN] = kr


def _attn_lat_kernel(q_ref, kl_ref, vtl_ref, kc_ref, vtc_ref, o_ref):
    tq = ATTN_TQ
    dv = vtl_ref.shape[0]
    n_chunks = kl_ref.shape[0] // ATTN_TK

    def scores(k, q):
        return lax.dot_general(k, q, (((1,), (1,)), ((), ())), preferred_element_type=F32)

    def weighted(vt, p):
        lhs = jnp.concatenate([vt, jnp.ones((ATTN_ONES_ROWS, vt.shape[1]), BF16)], axis=0)
        return jnp.dot(lhs, p, preferred_element_type=F32)

    def exact_max_block(s, vt, m, acc, risk):
        m_new = jnp.maximum(m, jnp.max(s, axis=0, keepdims=True))
        p = jnp.exp2(s - m_new).astype(BF16)
        return m_new, jnp.exp2(m - m_new) * acc + weighted(vt, p), risk

    def lagged_max_block(s, vt, m, acc, risk):
        p = jnp.exp2(s - m).astype(BF16)
        bmax = jnp.max(s, axis=0, keepdims=True)
        m_new = jnp.maximum(m, bmax)
        return m_new, (acc + weighted(vt, p)) * jnp.exp2(m - m_new), jnp.maximum(risk, bmax - m)

    def run(later_block, q_rows):
        q = q_ref[q_rows, :]
        state = (jnp.full((1, tq), -jnp.inf, F32), jnp.zeros((dv + ATTN_ONES_ROWS, tq), F32),
                 jnp.zeros((1, tq), F32))
        s_cur = scores(kc_ref[...], q)
        vt_cur = vtc_ref[...]
        for c in range(n_chunks):
            r = slice(c * ATTN_TK, (c + 1) * ATTN_TK)
            s_next = scores(kl_ref[r, :], q)
            state = (exact_max_block if c == 0 else later_block)(s_cur, vt_cur, *state)
            s_cur, vt_cur = s_next, vtl_ref[:, r]
        _, acc, risk = later_block(s_cur, vt_cur, *state)
        o_t = acc[:dv] / acc[dv:dv + 1]
        o_ref[q_rows, :] = o_t.T.astype(o_ref.dtype)
        return risk

    def query_tile(t, carry):
        q_rows = pl.ds(pl.multiple_of(t * tq, tq), tq)
        risk = run(lagged_max_block, q_rows)

        @pl.when(jnp.max(risk) > ATTN_MAX_LAG_LOG2)
        def _():
            run(exact_max_block, q_rows)

        return carry

    lax.fori_loop(0, q_ref.shape[0] // tq, query_tile, 0)


def _attn_ctx_kernel(q_ref, kc_ref, vtc_ref, o_ref):
    s = lax.dot_general(q_ref[...], kc_ref[...], (((1,), (1,)), ((), ())), preferred_element_type=F32)
    p = jnp.exp2(s - jnp.max(s, axis=-1, keepdims=True))
    l = jnp.sum(p, axis=-1, keepdims=True)
    o = lax.dot_general(p.astype(BF16), vtc_ref[...], (((1,), (1,)), ((), ())), preferred_element_type=F32)
    o_ref[...] = (o / l).astype(o_ref.dtype)


def _mla_attention(q, k, vt, n_batch, n_lat, n_ctx, heads):
    m = q.shape[0]
    hd = 2 * MLA_DN
    ctx_blk0 = (n_batch * n_lat) // n_ctx
    kl = pl.BlockSpec((n_lat, hd), lambda b, h: (b, h))
    vl = pl.BlockSpec((MLA_DV, n_lat), lambda b, h: (h, b))
    kc = pl.BlockSpec((n_ctx, hd), lambda b, h: (ctx_blk0 + b, h))
    vc = pl.BlockSpec((MLA_DV, n_ctx), lambda b, h: (h, ctx_blk0 + b))
    out = pl.pallas_call(
        _attn_lat_kernel,
        grid=(n_batch, heads),
        in_specs=[pl.BlockSpec((n_lat, hd), lambda b, h: (b, h)), kl, vl, kc, vc],
        out_specs=pl.BlockSpec((n_lat, MLA_DV), lambda b, h: (b, h)),
        out_shape=jax.ShapeDtypeStruct((n_batch * n_lat, heads * MLA_DV), BF16),
        compiler_params=_cparams(("arbitrary",) * 2),
        name="mla_attn_latent",
    )(q, k, vt, k, vt)
    out_ctx = pl.pallas_call(
        _attn_ctx_kernel,
        grid=(n_batch, heads),
        in_specs=[
            pl.BlockSpec((n_ctx, hd), lambda b, h: (ctx_blk0 + b, h)),
            pl.BlockSpec((n_ctx, hd), lambda b, h: (ctx_blk0 + b, h)),
            pl.BlockSpec((MLA_DV, n_ctx), lambda b, h: (h, ctx_blk0 + b)),
        ],
        out_specs=pl.BlockSpec((n_ctx, MLA_DV), lambda b, h: (b, h)),
        out_shape=jax.ShapeDtypeStruct((n_batch * n_ctx, heads * MLA_DV), BF16),
        compiler_params=_cparams(("arbitrary",) * 2),
        name="mla_attn_ctx",
    )(q, k, vt)
    return out, out_ctx


def _sgu_kernel(x_ref, g_ref, sh_ref, sc_ref, gate_ref, win_ref, gv_ref, ws_ref, bs_ref, wo_ref,
                o_ref, h_scr, uv_scr, z_scr):
    sd = wo_ref.shape[0]
    gw = sd // SGU_GROUPS
    n_rows = x_ref.shape[0]
    h_scr[...] = _modnorm(x_ref[...], g_ref, sh_ref, sc_ref).astype(BF16)

    n_col = (2 * sd) // SGU_IN_COLS
    pieces = [(c, slice(j * SGU_IN_COLS, (j + 1) * SGU_IN_COLS))
              for c in range(n_rows // ROW_CHUNK) for j in range(n_col)]

    def matmul(p):
        c, cols = pieces[p]
        return jnp.dot(h_scr[_chunk(c), :], win_ref[:, cols], preferred_element_type=F32)

    def epilogue(p, y):
        c, cols = pieces[p]
        uv_scr[_chunk(c), cols] = jax.nn.gelu(y).astype(BF16)

    _pipelined_chunks(len(pieces), matmul, epilogue)

    for ch in range(n_rows // SGU_CHUNK):
        rows = slice(ch * SGU_CHUNK, (ch + 1) * SGU_CHUNK)
        v = uv_scr[rows, sd:].astype(F32)
        vn = (v * lax.rsqrt(jnp.mean(v * v, axis=-1, keepdims=True) + EPS) * gv_ref[...]).astype(BF16)
        for g in range(SGU_GROUPS):
            cols = slice(g * gw, (g + 1) * gw)
            z = jnp.dot(ws_ref[g], vn[:, cols], preferred_element_type=F32) + bs_ref[g]
            z_scr[rows, cols] = (uv_scr[rows, cols].astype(F32) * z).astype(BF16)
    y = jnp.dot(z_scr[...], wo_ref[...], preferred_element_type=F32)
    o_ref[...] = x_ref[...] + gate_ref[...] * y


def _sgu(x, g, mods, layer, w_in, g_v, w_s, b_s, w_out, geom):
    m, d = x.shape
    sd = w_out.shape[0]
    tpb, n_batch = geom
    tms = TM // 2
    scale = TM // tms

    def mod_spec(k):
        return pl.BlockSpec((None, None, 1, d),
                            lambda i: (layer * 6 + k, jnp.minimum(i // (tpb * scale), n_batch), 0, 0))

    def resident(shape):
        return pl.BlockSpec(shape, lambda i: (0,) * len(shape), pipeline_mode=pl.Buffered(1))

    return pl.pallas_call(
        _sgu_kernel,
        grid=(m // tms,),
        in_specs=[
            pl.BlockSpec((tms, d), lambda i: (i, 0)),
            pl.BlockSpec((1, d), lambda i: (0, 0)),
            mod_spec(0), mod_spec(1), mod_spec(2),
            resident((d, 2 * sd)),
            pl.BlockSpec((1, sd), lambda i: (0, 0)),
            pl.BlockSpec((SGU_GROUPS, SGU_CHUNK, SGU_CHUNK), lambda i: (0, 0, 0)),
            pl.BlockSpec((SGU_GROUPS, SGU_CHUNK, 1), lambda i: (0, 0, 0)),
            resident((sd, d)),
        ],
        out_specs=pl.BlockSpec((tms, d), lambda i: (i, 0)),
        out_shape=jax.ShapeDtypeStruct(x.shape, F32),
        scratch_shapes=[pltpu.VMEM((tms, d), BF16), pltpu.VMEM((tms, 2 * sd), BF16),
                        pltpu.VMEM((tms, sd), BF16)],
        input_output_aliases={0: 0},
        compiler_params=_cparams(("arbitrary",)),
        name="sgu",
    )(x, g.reshape(1, d), mods, mods, mods, w_in, g_v.reshape(1, sd), w_s,
      b_s.reshape(SGU_GROUPS, SGU_CHUNK, 1), w_out)


def _router_kernel(x_ref, g_ref, sh_ref, sc_ref, wr_ref, h_ref, idx_ref, wgt_ref, *, n_tiles):
    i = pl.program_id(0)

    @pl.when(i < n_tiles)
    def _():
        h = _modnorm(x_ref[...], g_ref, sh_ref, sc_ref)
        h_ref[...] = h.astype(h_ref.dtype)
        w = wr_ref[...]
        h_hi, w_hi = h.astype(BF16), w.astype(BF16)
        h_lo = (h - h_hi.astype(F32)).astype(BF16)
        w_lo = (w - w_hi.astype(F32)).astype(BF16)
        logits = (jnp.dot(h_hi, w_hi, preferred_element_type=F32)
                  + jnp.dot(h_lo, w_hi, preferred_element_type=F32)
                  + jnp.dot(h_hi, w_lo, preferred_element_type=F32))
        lane = lax.broadcasted_iota(jnp.int32, logits.shape, 1).astype(F32)
        neg = jnp.float32(-jnp.inf)
        big = jnp.float32(logits.shape[1])
        logits = jnp.where(lane < N_EXPERTS, logits, neg)
        m1 = jnp.max(logits, axis=-1, keepdims=True)
        i1 = jnp.min(jnp.where(logits == m1, lane, big), axis=-1, keepdims=True)
        rest = jnp.where(lane == i1, neg, logits)
        m2 = jnp.max(rest, axis=-1, keepdims=True)
        i2 = jnp.min(jnp.where(rest == m2, lane, big), axis=-1, keepdims=True)
        e2 = jnp.exp(m2 - m1)
        w1 = 1.0 / (1.0 + e2)
        w2 = e2 / (1.0 + e2)
        idx_ref[...] = jnp.where(lane == 0, i1, jnp.where(lane == 1, i2, 0.0)).astype(jnp.int32)
        wgt_ref[...] = jnp.where(lane == 0, w1, jnp.where(lane == 1, w2, 0.0))

    @pl.when(i >= n_tiles)
    def _():
        h_ref[...] = jnp.zeros_like(h_ref)


def _router(x, g, mods, layer, w_router_pad, *, n_rows, h_rows, geom):
    d = x.shape[1]
    tpb, n_batch = geom
    sh_spec, sc_spec = _mod_specs(layer, 3, tpb, n_batch)
    nt = n_rows // TM
    real = lambda i: jnp.minimum(i, nt - 1)
    one = lambda f: (lambda i: f(real(i), 0))
    lanes = w_router_pad.shape[1]
    return pl.pallas_call(
        functools.partial(_router_kernel, n_tiles=nt),
        grid=(h_rows // TM,),
        in_specs=[
            pl.BlockSpec((TM, d), lambda i: (real(i), 0)),
            pl.BlockSpec((1, d), lambda i: (0, 0)),
            pl.BlockSpec(sh_spec.block_shape, one(sh_spec.index_map)),
            pl.BlockSpec(sc_spec.block_shape, one(sc_spec.index_map)),
            pl.BlockSpec((d, lanes), lambda i: (0, 0)),
        ],
        out_specs=[
            pl.BlockSpec((TM, d), lambda i: (i, 0)),
            pl.BlockSpec((TM, lanes), lambda i: (real(i), 0)),
            pl.BlockSpec((TM, lanes), lambda i: (real(i), 0)),
        ],
        out_shape=[
            jax.ShapeDtypeStruct((h_rows, d), BF16),
            jax.ShapeDtypeStruct((n_rows, lanes), jnp.int32),
            jax.ShapeDtypeStruct((n_rows, lanes), F32),
        ],
        compiler_params=_cparams(("arbitrary",)),
        name="moe_router",
    )(x, g.reshape(1, d), mods, mods, w_router_pad)


def _moe_ffn_kernel(te_ref, tr_ref, tf_ref, x_ref, wg32_ref, wu32_ref, wd32_ref, o_ref, acc_ref,
                    wg_ref, wu_ref, wd_ref):
    del te_ref
    t = pl.program_id(0)
    f = pl.program_id(1)
    n_rows = tr_ref[t]

    @pl.when(f == 0)
    def _():
        acc_ref[...] = jnp.zeros_like(acc_ref)

    @pl.when(tf_ref[t] == 1)
    def _():
        wg_ref[f] = wg32_ref[...].astype(BF16)
        wu_ref[f] = wu32_ref[...].astype(BF16)
        wd_ref[f] = wd32_ref[...].astype(BF16)

    def matmul(c):
        x = x_ref[_chunk(c), :]
        return (jnp.dot(x, wg_ref[f], preferred_element_type=F32),
                jnp.dot(x, wu_ref[f], preferred_element_type=F32))

    def epilogue(c, gu):
        a = (_silu(gu[0]) * gu[1]).astype(BF16)
        acc_ref[_chunk(c), :] += jnp.dot(a, wd_ref[f], preferred_element_type=F32)

    for n_chunks in range(1, MOE_TM // ROW_CHUNK + 1):
        @pl.when((n_rows > (n_chunks - 1) * ROW_CHUNK) & (n_rows <= n_chunks * ROW_CHUNK))
        def _(n_chunks=n_chunks):
            _pipelined_chunks(n_chunks, matmul, epilogue)

    @pl.when(f == pl.num_programs(1) - 1)
    def _():
        o_ref[...] = acc_ref[...].astype(o_ref.dtype)


def _moe_ffn(xs, tile_expert, tile_rows, tile_first, w_gu, w_down, lf):
    r, d = xs.shape
    ff = w_down.shape[2]
    nf = ff // MOE_BF
    n_tiles = r // MOE_TM

    def fidx(t, f, tf):
        return jnp.where(tf[t] == 1, f, nf - 1)

    grid_spec = pltpu.PrefetchScalarGridSpec(
        num_scalar_prefetch=3,
        grid=(n_tiles, nf),
        in_specs=[
            pl.BlockSpec((MOE_TM, d), lambda t, f, te, tr, tf: (t, 0)),
            pl.BlockSpec((None, None, d, MOE_BF), lambda t, f, te, tr, tf: (lf, te[t], 0, fidx(t, f, tf))),
            pl.BlockSpec((None, None, d, MOE_BF),
                         lambda t, f, te, tr, tf: (lf, te[t], 0, nf + fidx(t, f, tf))),
            pl.BlockSpec((None, None, MOE_BF, d), lambda t, f, te, tr, tf: (lf, te[t], fidx(t, f, tf), 0)),
        ],
        out_specs=pl.BlockSpec((MOE_TM, d), lambda t, f, te, tr, tf: (t, 0)),
        scratch_shapes=[pltpu.VMEM((MOE_TM, d), F32), pltpu.VMEM((nf, d, MOE_BF), BF16),
                        pltpu.VMEM((nf, d, MOE_BF), BF16), pltpu.VMEM((nf, MOE_BF, d), BF16)],
    )
    return pl.pallas_call(
        _moe_ffn_kernel,
        grid_spec=grid_spec,
        out_shape=jax.ShapeDtypeStruct((r, d), BF16),
        compiler_params=_cparams(("arbitrary", "arbitrary")),
        name="moe_ffn",
    )(tile_expert, tile_rows, tile_first, xs, w_gu, w_gu, w_down)


def _combine_kernel(x_ref, y0_ref, y1_ref, wgt_ref, gate_ref, gf_ref, o_ref, *, final):
    w = wgt_ref[...]
    y = w[:, 0:1] * y0_ref[...].astype(F32) + w[:, 1:2] * y1_ref[...].astype(F32)
    xn = x_ref[...] + gate_ref[...] * y
    if final:
        xn = xn * lax.rsqrt(jnp.mean(xn * xn, axis=-1, keepdims=True) + EPS) * gf_ref[...]
    o_ref[...] = xn


def _combine(x, y0, y1, wgt, mods, layer, g_final, *, n_rows, geom, final):
    d = x.shape[1]
    tpb, n_batch = geom
    lanes = wgt.shape[1]
    gate_spec = pl.BlockSpec((None, None, 1, d),
                             lambda i: (layer * 6 + 5, jnp.minimum(i // tpb, n_batch), 0, 0))
    row = lambda w: pl.BlockSpec((TM, w), lambda i: (i, 0))
    out_rows = n_rows if final else x.shape[0]
    return pl.pallas_call(
        functools.partial(_combine_kernel, final=final),
        grid=(n_rows // TM,),
        in_specs=[row(d), row(d), row(d), row(lanes), gate_spec, pl.BlockSpec((1, d), lambda i: (0, 0))],
        out_specs=row(d),
        out_shape=jax.ShapeDtypeStruct((out_rows, d), F32),
        input_output_aliases={} if final else {0: 0},
        compiler_params=_cparams(("arbitrary",)),
        name="moe_combine",
    )(x, y0, y1, wgt, mods, g_final.reshape(1, d))


def _moe_layer(x, g, mods, layer, w_router, w_gu, w_down, lf, g_final, *, n_rows, geom, final):
    d = x.shape[1]
    wr_pad = jnp.zeros((d, 128), F32).at[:, :N_EXPERTS].set(w_router)
    n_assign = 2 * n_rows
    r_rows = n_assign + (N_EXPERTS - 1) * MOE_TM
    h, idx, wgt = _router(x, g, mods, layer, wr_pad, n_rows=n_rows, h_rows=r_rows, geom=geom)
    e_flat = idx[:, :2].reshape(-1)
    onehot = (e_flat[:, None] == jnp.arange(N_EXPERTS, dtype=jnp.int32)[None, :]).astype(F32)
    blk = onehot.reshape(n_assign // CUMSUM_BLOCK, CUMSUM_BLOCK, N_EXPERTS)
    tril = jnp.tril(jnp.ones((CUMSUM_BLOCK, CUMSUM_BLOCK), F32))
    within = jnp.einsum("ij,bjk->bik", tril, blk)
    blk_tot = within[:, -1, :]
    blk_off = jnp.cumsum(blk_tot, axis=0) - blk_tot
    csum = (within + blk_off[:, None, :]).reshape(n_assign, N_EXPERTS)
    rank = (jnp.sum(onehot * csum, axis=1) - 1.0).astype(jnp.int32)
    counts = jnp.sum(blk_tot, axis=0).astype(jnp.int32)
    padded = ((counts + MOE_TM - 1) // MOE_TM) * MOE_TM
    ends = jnp.cumsum(padded)
    starts = ends - padded
    dest = starts[e_flat] + rank
    n_tiles = r_rows // MOE_TM
    hit = jnp.zeros((r_rows,), jnp.int32).at[dest].add(jnp.arange(n_assign, dtype=jnp.int32) // 2 + 1)
    src = jnp.where(hit > 0, hit - 1, jnp.arange(r_rows, dtype=jnp.int32) % n_rows)
    tile_start = jnp.arange(n_tiles, dtype=jnp.int32) * MOE_TM
    last_tile = jnp.maximum(ends[-1] // MOE_TM - 1, 0)
    ts_used = jnp.minimum(tile_start, last_tile * MOE_TM)
    tile_expert = jnp.sum((ends[None, :] <= ts_used[:, None]).astype(jnp.int32), axis=1)
    tile_expert = jnp.minimum(tile_expert, N_EXPERTS - 1)
    real_end = (starts + counts)[tile_expert]
    used = tile_start < ends[-1]
    tile_rows = jnp.where(used, jnp.clip(real_end - tile_start, 0, MOE_TM), 0).astype(jnp.int32)
    tile_first = (used & (tile_start == starts[tile_expert])).astype(jnp.int32)
    xs = jnp.take(h, src, axis=0, mode="clip")
    ys = _moe_ffn(xs, tile_expert, tile_rows, tile_first, w_gu, w_down, lf)
    d2 = dest.reshape(n_rows, 2)
    y0 = jnp.take(ys, d2[:, 0], axis=0, mode="clip")
    y1 = jnp.take(ys, d2[:, 1], axis=0, mode="clip")
    return _combine(x, y0, y1, wgt, mods, layer, g_final, n_rows=n_rows, geom=geom, final=final)


def kernel(x, c, ctx, c_ctx, ada_w, ada_b, norm_mix_g, norm_ffn_g, final_norm_g, ret_w_in, ret_w_o, ret_decay_f, ret_decay_b, ret_gn_f, ret_gn_b, mla_w_down, mla_g_q, mla_g_kv, mla_w_uq, mla_w_ukv, mla_w_o, sgu_w_in, sgu_g_v, sgu_w_s, sgu_b_s, sgu_w_out, ffn_w_gu, ffn_w_down, moe_w_router, moe_w_gu, moe_w_down):
    n_batch, n_lat, d = x.shape
    n_ctx = ctx.shape[1]
    depth = ada_w.shape[0]
    rows_lat = n_batch * n_lat
    m = rows_lat + n_batch * n_ctx
    geom = (n_lat // TM, n_batch)
    assert n_batch * n_ctx == TM and d == 1024 and n_batch + 1 <= 8

    xt = (x.reshape(rows_lat, d), ctx.reshape(n_batch * n_ctx, d))
    cond = jnp.concatenate([c, c_ctx[None, :], jnp.zeros((8 - n_batch - 1, d), F32)], axis=0)
    mods = _adaln(cond, ada_w, ada_b)
    mods = mods.reshape(depth, 8, 6, d).transpose(0, 2, 1, 3).reshape(depth * 6, 8, 1, d)

    grid_rows = n_lat // GRID_W
    ret_cos, ret_sin = _rope_tables(grid_rows, n_batch * n_ctx, RET_DK)
    mla_cos, mla_sin = _rope_tables(grid_rows, n_batch * n_ctx, MLA_DR)
    pad_r = MLA_DN - MLA_DR
    mla_cos = jnp.concatenate([mla_cos, jnp.ones((mla_cos.shape[0], pad_r), F32)], axis=1)
    mla_sin = jnp.concatenate([mla_sin, jnp.zeros((mla_sin.shape[0], pad_r), F32)], axis=1)

    for i in range(depth):
        last = i == depth - 1
        kind, jm = i % N_MIXERS, i // N_MIXERS
        rows_out = rows_lat if last else m
        if kind == 0:
            qk, vv, gates = _ret_in(xt, norm_mix_g[i], mods, i, ret_w_in[jm].astype(BF16),
                                    (ret_cos, ret_sin), geom=geom)
            lg = jnp.stack([jax.nn.log_sigmoid(ret_decay_f[jm].astype(F32)),
                            jax.nn.log_sigmoid(ret_decay_b[jm].astype(F32))])
            hv = vv.shape[1]
            gn = jnp.stack([ret_gn_f[jm], ret_gn_b[jm]]).reshape(2, 1, hv)
            yf, yb = _retention(qk, vv, gates, lg, gn, n_batch, n_lat, n_ctx)
            xt = _outproj([yf, yb], ret_w_o[jm].astype(BF16), xt, mods, i, 2, n_rows=rows_out, geom=geom)
        elif kind == 1:
            heads = mla_w_o.shape[1] // MLA_DV
            lq, lkv = mla_g_q.shape[1], mla_g_kv.shape[1]
            n_dn = lq + lkv + MLA_DN
            wd = jnp.zeros((d, n_dn), F32).at[:, :lq + lkv + MLA_DR].set(mla_w_down[jm]).astype(BF16)
            dn = _proj(xt, norm_mix_g[i], mods, i, 0, wd, mode="f32", bn=n_dn, out_dtype=F32,
                       n_rows=m, geom=geom)
            wq = mla_w_uq[jm].reshape(lq, heads, MLA_DN + MLA_DR)
            wq = jnp.concatenate([wq, jnp.zeros((lq, heads, pad_r), F32)], axis=2)
            wq = wq.reshape(lq, heads * 2 * MLA_DN).astype(BF16)
            wkv = mla_w_ukv[jm].reshape(lkv, heads, MLA_DN + MLA_DV)
            wk = wkv[:, :, :MLA_DN].reshape(lkv, heads * MLA_DN).astype(BF16)
            wv = wkv[:, :, MLA_DN:].reshape(lkv, heads * MLA_DV).T.astype(BF16)
            tpb = geom[0]
            n_lat_tiles = tpb * n_batch
            pos_idx = lambda r: (jnp.where(r < n_lat_tiles, r % tpb, tpb + (r - n_lat_tiles)), 0)
            full = lambda a: pl.BlockSpec(a.shape, lambda r: (0,) * a.ndim)
            gq = mla_g_q[jm].reshape(1, lq)
            gkv = mla_g_kv[jm].reshape(1, lkv)
            scale = float((MLA_DN + MLA_DR) ** -0.5 * LOG2_E)
            q, k, vt = pl.pallas_call(
                functools.partial(_mla_up_kernel, heads=heads, scale=scale),
                grid=(m // TM,),
                in_specs=[pl.BlockSpec((TM, n_dn), lambda r: (r, 0)), full(gq), full(gkv), full(wq),
                          full(wk), full(wv), pl.BlockSpec((TM, MLA_DN), pos_idx),
                          pl.BlockSpec((TM, MLA_DN), pos_idx)],
                out_specs=[pl.BlockSpec((TM, heads * 2 * MLA_DN), lambda r: (r, 0)),
                           pl.BlockSpec((TM, heads * 2 * MLA_DN), lambda r: (r, 0)),
                           pl.BlockSpec((heads * MLA_DV, TM), lambda r: (0, r))],
                out_shape=[jax.ShapeDtypeStruct((m, heads * 2 * MLA_DN), BF16),
                           jax.ShapeDtypeStruct((m, heads * 2 * MLA_DN), BF16),
                           jax.ShapeDtypeStruct((heads * MLA_DV, m), BF16)],
                compiler_params=_cparams(("arbitrary",)),
                name="mla_up",
            )(dn, gq, gkv, wq, wk, wv, mla_cos, mla_sin)
            ao = _mla_attention(q, k, vt, n_batch, n_lat, n_ctx, heads)
            xt = _outproj([ao], mla_w_o[jm].astype(BF16), xt, mods, i, 2, n_rows=rows_out, geom=geom)
        else:
            xt = _sgu(xt, norm_mix_g[i], mods, i, sgu_w_in[jm].astype(BF16), sgu_g_v[jm],
                      sgu_w_s[jm].astype(BF16), sgu_b_s[jm], sgu_w_out[jm].astype(BF16), geom)
        f = i // 2
        if i % 2 == 0:
            dff = ffn_w_down.shape[1]
            wgu = ffn_w_gu[f].astype(BF16)
            xt = _dense_ffn(xt, norm_ffn_g[i], mods, i, wgu[:, :dff], wgu[:, dff:],
                            ffn_w_down[f].astype(BF16), n_rows=rows_out, geom=geom)
        else:
            xt = _moe_layer(xt, norm_ffn_g[i], mods, i, moe_w_router[f], moe_w_gu, moe_w_down, f,
                            final_norm_g, n_rows=rows_out, geom=geom, final=last)
    if not (depth - 1) % 2:
        raise NotImplementedError("final norm is fused into the expert combine of the last layer")
    return xt.reshape(n_batch, n_lat, d)
```
